```python
import jax, jax.numpy as jnp
from jax import lax
import numpy as np

D_MODEL = 2048
BATCH = 4
SEQ = 4096
DEPTH = 1

HEAD_DIM = 64
ATTN_WIDTH = D_MODEL // 2
CONV_WIDTH_CH = D_MODEL - ATTN_WIDTH
N_Q_HEADS = ATTN_WIDTH // HEAD_DIM
N_KV_HEADS = 4
GROUP = N_Q_HEADS // N_KV_HEADS
KV_WIDTH = N_KV_HEADS * HEAD_DIM
N_CONV_GROUPS = CONV_WIDTH_CH // HEAD_DIM
CONV_WIDTH = 3
WINDOW = 128
ROT_DIM = HEAD_DIM // 4
ROPE_THETA = 500000.0
D_FF = ((8 * D_MODEL // 3 + 255) // 256) * 256
IN_WIDTH = ATTN_WIDTH + 2 * KV_WIDTH + 3 * CONV_WIDTH_CH
SPLITS = [ATTN_WIDTH,
          ATTN_WIDTH + KV_WIDTH,
          ATTN_WIDTH + 2 * KV_WIDTH,
          ATTN_WIDTH + 2 * KV_WIDTH + CONV_WIDTH_CH,
          ATTN_WIDTH + 2 * KV_WIDTH + 2 * CONV_WIDTH_CH]
ATTN_SCALE = HEAD_DIM ** -0.5
DEEPNORM_ALPHA = (2 * DEPTH) ** 0.25
DEEPNORM_BETA = (8 * DEPTH) ** -0.25
LN_EPS = 1e-5
RMS_EPS = 1e-6

kernel_name = "hymba_conv_swa_sink_deepnorm_layer"


def _layer_norm(x, g, b):
    xf = x.astype(jnp.float32)
    mu = jnp.mean(xf, axis=-1, keepdims=True)
    var = jnp.mean(jnp.square(xf - mu), axis=-1, keepdims=True)
    y = (xf - mu) * lax.rsqrt(var + LN_EPS)
    return (y * g.astype(jnp.float32) + b.astype(jnp.float32)).astype(x.dtype)


def _rms_norm(x, g):
    xf = x.astype(jnp.float32)
    y = xf * lax.rsqrt(jnp.mean(jnp.square(xf), axis=-1, keepdims=True) + RMS_EPS)
    return (y * g.astype(jnp.float32)).astype(x.dtype)


def _partial_rope(t, positions):
    inv_freq = ROPE_THETA ** (-jnp.arange(0, ROT_DIM, 2, dtype=jnp.float32) / ROT_DIM)
    ang = positions.astype(jnp.float32)[..., None] * inv_freq
    cos = jnp.cos(ang)[:, :, None, :]
    sin = jnp.sin(ang)[:, :, None, :]
    tr = t[..., :ROT_DIM].astype(jnp.float32)
    t1, t2 = tr[..., :ROT_DIM // 2], tr[..., ROT_DIM // 2:]
    rot = jnp.concatenate([t1 * cos - t2 * sin, t2 * cos + t1 * sin], axis=-1).astype(t.dtype)
    return jnp.concatenate([rot, t[..., ROT_DIM:]], axis=-1)


def _sliding_window_attention(q, k, v, sinks):
    b, s = q.shape[0], q.shape[1]
    nb = s // WINDOW
    qb = q.reshape(b, nb, WINDOW, N_KV_HEADS, GROUP, HEAD_DIM)
    kb = k.reshape(b, nb, WINDOW, N_KV_HEADS, HEAD_DIM)
    vb = v.reshape(b, nb, WINDOW, N_KV_HEADS, HEAD_DIM)

    def with_prev(t):
        prev = jnp.concatenate([jnp.zeros_like(t[:, :1]), t[:, :-1]], axis=1)
        return jnp.concatenate([prev, t], axis=2)

    kk, vv = with_prev(kb), with_prev(vb)
    scores = jnp.einsum('bnqhgd,bnkhd->bnhgqk', qb, kk,
                        preferred_element_type=jnp.float32) * ATTN_SCALE
    qi = jnp.arange(WINDOW)[:, None]
    kj = jnp.arange(2 * WINDOW)[None, :]
    rel = qi + WINDOW - kj
    band = (rel >= 0) & (rel < WINDOW)
    first_block = (jnp.arange(nb) == 0)[:, None, None]
    valid = band[None] & ~(first_block & (kj < WINDOW)[None])
    scores = jnp.where(valid[None, :, None, None], scores, -jnp.inf)
    sink_col = jnp.broadcast_to(
        sinks.astype(jnp.float32).reshape(N_KV_HEADS, GROUP)[None, None, :, :, None, None],
        scores.shape[:-1] + (1,))
    probs = jax.nn.softmax(jnp.concatenate([scores, sink_col], axis=-1), axis=-1)[..., :-1]
    out = jnp.einsum('bnhgqk,bnkhd->bnqhgd', probs.astype(v.dtype), vv)
    return out.reshape(b, s, N_Q_HEADS * HEAD_DIM)


def _short_gated_conv(c_gate, b_gate, u, conv_w):
    z = c_gate * u
    s = z.shape[1]
    zp = jnp.pad(z, ((0, 0), (CONV_WIDTH - 1, 0), (0, 0)))
    y = conv_w[0] * zp[:, 0:s] + conv_w[1] * zp[:, 1:s + 1] + conv_w[2] * zp[:, 2:s + 2]
    return b_gate * y


def setup_inputs(seed: int = 0) -> dict:
    key = jax.random.key(seed)
    ks = jax.random.split(key, 16)
    f32 = jnp.float32
    x = jax.random.normal(ks[0], (BATCH, SEQ, D_MODEL), f32)
    offset = jax.random.randint(ks[1], (BATCH, 1), 0, 1024, dtype=jnp.int32)
    positions = (offset + jnp.arange(SEQ, dtype=jnp.int32)[None, :]).astype(jnp.int32)
    w_in = jax.random.normal(ks[2], (DEPTH, D_MODEL, IN_WIDTH), f32) * D_MODEL ** -0.5
    conv_w = jax.random.normal(ks[3], (DEPTH, CONV_WIDTH, CONV_WIDTH_CH), f32) * CONV_WIDTH ** -0.5
    sinks = jax.random.normal(ks[4], (DEPTH, N_Q_HEADS), f32) * 0.5
    g_attn = 1.0 + 0.02 * jax.random.normal(ks[5], (DEPTH, ATTN_WIDTH), f32)
    g_conv = 1.0 + 0.02 * jax.random.normal(ks[6], (DEPTH, CONV_WIDTH_CH), f32)
    w_out = jax.random.normal(ks[7], (DEPTH, D_MODEL, D_MODEL), f32) * (DEEPNORM_BETA * D_MODEL ** -0.5)
    ln1_g = 1.0 + 0.02 * jax.random.normal(ks[8], (DEPTH, D_MODEL), f32)
    ln1_b = 0.02 * jax.random.normal(ks[9], (DEPTH, D_MODEL), f32)
    w_gate = jax.random.normal(ks[10], (DEPTH, D_MODEL, D_FF), f32) * D_MODEL ** -0.5
    w_up = jax.random.normal(ks[11], (DEPTH, D_MODEL, D_FF), f32) * D_MODEL ** -0.5
    w_down = jax.random.normal(ks[12], (DEPTH, D_FF, D_MODEL), f32) * (DEEPNORM_BETA * D_FF ** -0.5)
    ln2_g = 1.0 + 0.02 * jax.random.normal(ks[13], (DEPTH, D_MODEL), f32)
    ln2_b = 0.02 * jax.random.normal(ks[14], (DEPTH, D_MODEL), f32)
    return {"x": x, "positions": positions, "w_in": w_in, "conv_w": conv_w,
            "sinks": sinks, "g_attn": g_attn, "g_conv": g_conv, "w_out": w_out,
            "ln1_g": ln1_g, "ln1_b": ln1_b, "w_gate": w_gate, "w_up": w_up,
            "w_down": w_down, "ln2_g": ln2_g, "ln2_b": ln2_b}


def reference(x, positions, w_in, conv_w, sinks, g_attn, g_conv, w_out,
              ln1_g, ln1_b, w_gate, w_up, w_down, ln2_g, ln2_b):
    b, s, _ = x.shape
    h = x
    for l in range(DEPTH):
        proj = h @ w_in[l]
        q, k, v, c_gate, b_gate, u = jnp.split(proj, SPLITS, axis=-1)
        q = _partial_rope(q.reshape(b, s, N_Q_HEADS, HEAD_DIM), positions)
        k = _partial_rope(k.reshape(b, s, N_KV_HEADS, HEAD_DIM), positions)
        v = v.reshape(b, s, N_KV_HEADS, HEAD_DIM)
        attn = _sliding_window_attention(q, k, v, sinks[l])
        conv = _short_gated_conv(c_gate, b_gate, u, conv_w[l])
        mixed = jnp.concatenate([_rms_norm(attn, g_attn[l]), _rms_norm(conv, g_conv[l])], axis=-1)
        mix_out = mixed @ w_out[l]
        h = _layer_norm(DEEPNORM_ALPHA * h + mix_out, ln1_g[l], ln1_b[l])
        ffn = (jax.nn.silu(h @ w_gate[l]) * (h @ w_up[l])) @ w_down[l]
        h = _layer_norm(DEEPNORM_ALPHA * h + ffn, ln2_g[l], ln2_b[l])
    return h
```

```python
import functools

import numpy as np
import jax
import jax.numpy as jnp
from jax import lax
from jax.experimental import pallas as pl
from jax.experimental.pallas import tpu as pltpu

F32 = jnp.float32
BF16 = jnp.bfloat16

HEAD_DIM = 64
N_KV_HEADS = 4
WINDOW = 128
ROT_DIM = HEAD_DIM // 4
ROPE_THETA = 500000.0
CONV_WIDTH = 3
LN_EPS = 1e-5
RMS_EPS = 1e-6

LANES = 128
SUBLANES = 8
VMEM_LIMIT_BYTES = 60000 * 1024

TM_PROJ = 256
TM_ATTN = 256
TM_FFN = 512
TF_FFN = 512


def _resident(block_shape, index_map):
    return pl.BlockSpec(block_shape, index_map, pipeline_mode=pl.Buffered(1))


def _layer_norm(x, g, b):
    mu = jnp.mean(x, axis=-1, keepdims=True)
    xc = x - mu
    var = jnp.mean(xc * xc, axis=-1, keepdims=True)
    return xc * lax.rsqrt(var + LN_EPS) * g + b


def _rms_norm(x, g):
    ms = jnp.mean(x * x, axis=-1, keepdims=True)
    return x * lax.rsqrt(ms + RMS_EPS) * g


def _proj_body(x_ref, pos_ref, invf_ref, m1_ref, m2_ref, w_ref, cw_ref, gc_ref,
               q_ref, k_ref, v_ref, co_ref, zbuf_ref, *, dims, tiles_per_seq):
    aw, kvw, cw, attn_scale = dims
    tm = x_ref.shape[0]
    i = pl.program_id(0)
    xb = x_ref[...].astype(BF16)

    ang = pos_ref[...].astype(F32) * invf_ref[...]
    cosv = jnp.cos(ang)
    sinv = jnp.sin(ang)
    s_lo = sinv * m1_ref[...]
    s_hi = sinv * m2_ref[...]

    def rope(t):
        outs = []
        for c in range(t.shape[1] // LANES):
            tc = t[:, c * LANES:(c + 1) * LANES]
            up = pltpu.roll(tc, LANES - ROT_DIM // 2, 1)
            dn = pltpu.roll(tc, ROT_DIM // 2, 1)
            outs.append(tc * cosv + up * s_lo + dn * s_hi)
        return jnp.concatenate(outs, axis=1)

    def proj(lo, width):
        return jnp.dot(xb, w_ref[:, lo:lo + width], preferred_element_type=F32)

    q = proj(0, aw)
    q_ref[...] = (rope(q) * attn_scale).astype(BF16)
    k = proj(aw, kvw)
    k_ref[...] = rope(k).astype(BF16)
    v_ref[...] = proj(aw + kvw, kvw).astype(BF16)

    c0 = aw + 2 * kvw
    z = proj(c0, cw) * proj(c0 + 2 * cw, cw)

    @pl.when(i % tiles_per_seq == 0)
    def _():
        zbuf_ref[0:SUBLANES, :] = jnp.zeros((SUBLANES, cw), F32)

    zbuf_ref[SUBLANES:SUBLANES + tm, :] = z
    z1 = zbuf_ref[SUBLANES - 1:SUBLANES - 1 + tm, :]
    z2 = zbuf_ref[SUBLANES - 2:SUBLANES - 2 + tm, :]
    conv = cw_ref[0:1, :] * z2 + cw_ref[1:2, :] * z1 + cw_ref[2:3, :] * z
    y = proj(c0 + cw, cw) * conv
    co_ref[...] = _rms_norm(y, gc_ref[...]).astype(BF16)
    zbuf_ref[0:SUBLANES, :] = zbuf_ref[tm:tm + SUBLANES, :]


def _attn_body(sinks_ref, q_ref, kc_ref, kp_ref, vc_ref, vp_ref, co_ref, x_ref,
               ga_ref, wo_ref, g1_ref, b1_ref, h_ref,
               kf_ref, vf_ref, at_ref, mx_ref, *, n_q_heads, alpha, tiles_per_seq):
    tm = q_ref.shape[0]
    aw = q_ref.shape[1]
    group = n_q_heads // N_KV_HEADS
    i = pl.program_id(0)
    seq_start = i % tiles_per_seq == 0

    kf_ref[0:WINDOW, :] = kp_ref[...]
    kf_ref[WINDOW:WINDOW + tm, :] = kc_ref[...]
    vf_ref[0:WINDOW, :] = vp_ref[...]
    vf_ref[WINDOW:WINDOW + tm, :] = vc_ref[...]

    qi = lax.broadcasted_iota(jnp.int32, (WINDOW, 2 * WINDOW), 0)
    kj = lax.broadcasted_iota(jnp.int32, (WINDOW, 2 * WINDOW), 1)
    rel = qi + WINDOW - kj
    band = (rel >= 0) & (rel < WINDOW)

    def window(w, carry):
        r0 = pl.multiple_of(w * WINDOW, WINDOW)
        first_key = jnp.where(jnp.logical_and(seq_start, w == 0), WINDOW, 0)
        valid = band & (kj >= first_key)
        kk = kf_ref[pl.ds(r0, 2 * WINDOW), :]
        vv = vf_ref[pl.ds(r0, 2 * WINDOW), :]
        qw = q_ref[pl.ds(r0, WINDOW), :]
        outs = []
        for h in range(n_q_heads):
            g = h // group
            qh = qw[:, h * HEAD_DIM:(h + 1) * HEAD_DIM]
            kg = kk[:, g * HEAD_DIM:(g + 1) * HEAD_DIM]
            vg = vv[:, g * HEAD_DIM:(g + 1) * HEAD_DIM]
            s = lax.dot_general(qh, kg, (((1,), (1,)), ((), ())),
                                preferred_element_type=F32)
            s = jnp.where(valid, s, -jnp.inf)
            sink = sinks_ref[h]
            m = jnp.maximum(jnp.max(s, axis=-1, keepdims=True), sink)
            e = jnp.exp(s - m)
            denom = jnp.sum(e, axis=-1, keepdims=True) + jnp.exp(sink - m)
            o = jnp.dot(e.astype(BF16), vg, preferred_element_type=F32)
            outs.append(o / denom)
        at_ref[pl.ds(r0, WINDOW), :] = jnp.concatenate(outs, axis=1)
        return carry

    lax.fori_loop(0, tm // WINDOW, window, 0)

    mx_ref[:, 0:aw] = _rms_norm(at_ref[...], ga_ref[...]).astype(BF16)
    mx_ref[:, aw:] = co_ref[...]
    mix = jnp.dot(mx_ref[...], wo_ref[...], preferred_element_type=F32)
    h_ref[...] = _layer_norm(alpha * x_ref[...] + mix, g1_ref[...], b1_ref[...])


def _ffn_body(h_ref, wg_ref, wu_ref, wd_ref, g2_ref, b2_ref, o_ref, hb_ref, acc_ref, *, alpha):
    j = pl.program_id(1)

    @pl.when(j == 0)
    def _():
        hb_ref[...] = h_ref[...].astype(BF16)
        acc_ref[...] = jnp.zeros(acc_ref.shape, F32)

    hb = hb_ref[...]
    gate = jnp.dot(hb, wg_ref[...], preferred_element_type=F32)
    up = jnp.dot(hb, wu_ref[...], preferred_element_type=F32)
    act = (gate * (1.0 / (1.0 + jnp.exp(-gate))) * up).astype(BF16)
    acc_ref[...] += jnp.dot(act, wd_ref[...], preferred_element_type=F32)

    @pl.when(j == pl.num_programs(1) - 1)
    def _():
        o_ref[...] = _layer_norm(alpha * h_ref[...] + acc_ref[...], g2_ref[...], b2_ref[...])


def _rope_lane_tables():
    inv_freq = ROPE_THETA ** (-jnp.arange(0, ROT_DIM, 2, dtype=F32) / ROT_DIM)
    lane = np.arange(LANES) % HEAD_DIM
    half = ROT_DIM // 2
    rotated = lane < ROT_DIM
    invf = jnp.where(jnp.asarray(rotated), inv_freq[jnp.asarray(lane % half)], 0.0)
    m_lo = np.where(lane < half, -1.0, 0.0).astype(np.float32)
    m_hi = np.where(rotated & (lane >= half), 1.0, 0.0).astype(np.float32)
    return invf.reshape(1, LANES).astype(F32), jnp.asarray(m_lo).reshape(1, LANES), jnp.asarray(m_hi).reshape(1, LANES)


def _layer(h, pos_col, w_in, conv_w, sinks, g_attn, g_conv, w_out, ln1_g, ln1_b,
           w_gate, w_up, w_down, ln2_g, ln2_b, *, seq, alpha):
    m, d = h.shape
    aw = g_attn.shape[0]
    cw = g_conv.shape[0]
    kvw = N_KV_HEADS * HEAD_DIM
    n_q_heads = aw // HEAD_DIM
    d_ff = w_gate.shape[1]
    in_width = w_in.shape[1]
    assert in_width == aw + 2 * kvw + 3 * cw and d == aw + cw
    assert seq % TM_PROJ == 0 and seq % TM_ATTN == 0 and TM_ATTN % WINDOW == 0
    assert m % TM_FFN == 0 and d_ff % TF_FFN == 0

    invf, m_lo, m_hi = _rope_lane_tables()
    row = lambda a: a.reshape(1, -1).astype(F32)
    cparams = functools.partial(pltpu.CompilerParams, vmem_limit_bytes=VMEM_LIMIT_BYTES)

    tm = TM_PROJ
    const = lambda i: (0, 0)
    q, k, v, co = pl.pallas_call(
        functools.partial(_proj_body, dims=(aw, kvw, cw, HEAD_DIM ** -0.5), tiles_per_seq=seq // tm),
        grid=(m // tm,),
        in_specs=[
            pl.BlockSpec((tm, d), lambda i: (i, 0)),
            pl.BlockSpec((tm, 1), lambda i: (i, 0)),
            _resident((1, LANES), const),
            _resident((1, LANES), const),
            _resident((1, LANES), const),
            _resident((d, in_width), const),
            _resident((CONV_WIDTH, cw), const),
            _resident((1, cw), const),
        ],
        out_specs=[
            pl.BlockSpec((tm, aw), lambda i: (i, 0)),
            pl.BlockSpec((tm, kvw), lambda i: (i, 0)),
            pl.BlockSpec((tm, kvw), lambda i: (i, 0)),
            pl.BlockSpec((tm, cw), lambda i: (i, 0)),
        ],
        out_shape=[
            jax.ShapeDtypeStruct((m, aw), BF16),
            jax.ShapeDtypeStruct((m, kvw), BF16),
            jax.ShapeDtypeStruct((m, kvw), BF16),
            jax.ShapeDtypeStruct((m, cw), BF16),
        ],
        scratch_shapes=[pltpu.VMEM((tm + 2 * SUBLANES, cw), F32)],
        compiler_params=cparams(dimension_semantics=("arbitrary",)),
        name="proj_rope_conv",
    )(h, pos_col, invf, m_lo, m_hi, w_in.astype(BF16), conv_w.astype(F32), row(g_conv))

    tm = TM_ATTN
    wpt = tm // WINDOW
    cur = lambda i: (i, 0)
    prev = lambda i: (jnp.maximum(i * wpt - 1, 0), 0)
    h1 = pl.pallas_call(
        functools.partial(_attn_body, n_q_heads=n_q_heads, alpha=alpha, tiles_per_seq=seq // tm),
        grid=(m // tm,),
        in_specs=[
            pl.BlockSpec(memory_space=pltpu.SMEM),
            pl.BlockSpec((tm, aw), cur),
            pl.BlockSpec((tm, kvw), cur),
            pl.BlockSpec((WINDOW, kvw), prev),
            pl.BlockSpec((tm, kvw), cur),
            pl.BlockSpec((WINDOW, kvw), prev),
            pl.BlockSpec((tm, cw), cur),
            pl.BlockSpec((tm, d), cur),
            _resident((1, aw), const),
            _resident((d, d), const),
            _resident((1, d), const),
            _resident((1, d), const),
        ],
        out_specs=pl.BlockSpec((tm, d), cur),
        out_shape=jax.ShapeDtypeStruct((m, d), F32),
        scratch_shapes=[
            pltpu.VMEM((tm + WINDOW, kvw), BF16),
            pltpu.VMEM((tm + WINDOW, kvw), BF16),
            pltpu.VMEM((tm, aw), F32),
            pltpu.VMEM((tm, d), BF16),
        ],
        compiler_params=cparams(dimension_semantics=("arbitrary",)),
        name="swa_outproj_ln",
    )(sinks.astype(F32), q, k, k, v, v, co, h, row(g_attn), w_out.astype(BF16), row(ln1_g), row(ln1_b))

    tm, tf = TM_FFN, TF_FFN
    out = pl.pallas_call(
        functools.partial(_ffn_body, alpha=alpha),
        grid=(m // tm, d_ff // tf),
        in_specs=[
            pl.BlockSpec((tm, d), lambda i, j: (i, 0)),
            pl.BlockSpec((d, tf), lambda i, j: (0, j)),
            pl.BlockSpec((d, tf), lambda i, j: (0, j)),
            pl.BlockSpec((tf, d), lambda i, j: (j, 0)),
            _resident((1, d), lambda i, j: (0, 0)),
            _resident((1, d), lambda i, j: (0, 0)),
        ],
        out_specs=pl.BlockSpec((tm, d), lambda i, j: (i, 0)),
        out_shape=jax.ShapeDtypeStruct((m, d), F32),
        scratch_shapes=[pltpu.VMEM((tm, d), BF16), pltpu.VMEM((tm, d), F32)],
        compiler_params=cparams(dimension_semantics=("arbitrary", "arbitrary")),
        name="swiglu_ffn_ln",
    )(h1, w_gate.astype(BF16), w_up.astype(BF16), w_down.astype(BF16), row(ln2_g), row(ln2_b))
    return out


def kernel(x, positions, w_in, conv_w, sinks, g_attn, g_conv, w_out, ln1_g, ln1_b, w_gate, w_up, w_down, ln2_g, ln2_b):
    b, s, d = x.shape
    depth = w_in.shape[0]
    alpha = (2 * depth) ** 0.25
    h = x.reshape(b * s, d)
    pos_col = positions.reshape(b * s, 1)
    for l in range(depth):
        h = _layer(h, pos_col, w_in[l], conv_w[l], sinks[l], g_attn[l], g_conv[l], w_out[l],
                   ln1_g[l], ln1_b[l], w_gate[l], w_up[l], w_down[l], ln2_g[l], ln2_b[l],
                   seq=s, alpha=alpha)
    return h.reshape(b, s, d)
```

```python
import functools

import numpy as np
import jax
import jax.numpy as jnp
from jax import lax
from jax.experimental import pallas as pl
from jax.experimental.pallas import tpu as pltpu

F32 = jnp.float32
BF16 = jnp.bfloat16

HEAD_DIM = 64
N_KV_HEADS = 4
WINDOW = 128
ROT_DIM = HEAD_DIM // 4
ROPE_THETA = 500000.0
CONV_WIDTH = 3
LN_EPS = 1e-5
RMS_EPS = 1e-6

LANES = 128
SUBLANES = 8
BF16_ROWS = 16
VMEM_LIMIT_BYTES = 60000 * 1024

TM_PROJ = 256
TM_ATTN = 256
TM_FFN = 512
TF_FFN = 512

_NT = (((1,), (1,)), ((), ()))
_TN = (((0,), (0,)), ((), ()))


def _resident(block_shape, index_map):
    return pl.BlockSpec(block_shape, index_map, pipeline_mode=pl.Buffered(1))


def _layer_norm(x, g, b):
    mu = jnp.mean(x, axis=-1, keepdims=True)
    xc = x - mu
    var = jnp.mean(xc * xc, axis=-1, keepdims=True)
    return xc * lax.rsqrt(var + LN_EPS) * g + b


def _rms_norm(x, g):
    ms = jnp.mean(x * x, axis=-1, keepdims=True)
    return x * lax.rsqrt(ms + RMS_EPS) * g


def _proj_body(x_ref, pos_ref, invf_ref, m1_ref, m2_ref, w_ref, wvt_ref, cw_ref, gc_ref,
               q_ref, ka_ref, kb_ref, vt_ref, co_ref, zbuf_ref, *, dims, tiles_per_seq):
    aw, kvw, cw, attn_scale = dims
    tm = x_ref.shape[0]
    i = pl.program_id(0)
    xb = x_ref[...].astype(BF16)

    ang = pos_ref[...].astype(F32) * invf_ref[...]
    cosv = jnp.cos(ang)
    sinv = jnp.sin(ang)
    s_lo = sinv * m1_ref[...]
    s_hi = sinv * m2_ref[...]

    def rope(tc):
        up = pltpu.roll(tc, LANES - ROT_DIM // 2, 1)
        dn = pltpu.roll(tc, ROT_DIM // 2, 1)
        return tc * cosv + up * s_lo + dn * s_hi

    def proj(lo, width):
        return jnp.dot(xb, w_ref[:, lo:lo + width], preferred_element_type=F32)

    q = proj(0, aw)
    for c in range(aw // LANES):
        blk = slice(c * LANES, (c + 1) * LANES)
        q_ref[:, blk] = (rope(q[:, blk]) * attn_scale).astype(BF16)

    k = proj(aw, kvw)
    low = lax.broadcasted_iota(jnp.int32, (tm, LANES), 1) < HEAD_DIM
    for c in range(kvw // LANES):
        kr = rope(k[:, c * LANES:(c + 1) * LANES])
        sw = pltpu.roll(kr, HEAD_DIM, 1)
        even = slice(2 * c * LANES, (2 * c + 1) * LANES)
        odd = slice((2 * c + 1) * LANES, (2 * c + 2) * LANES)
        ka_ref[:, even] = jnp.where(low, kr, 0.0).astype(BF16)
        ka_ref[:, odd] = jnp.where(low, sw, 0.0).astype(BF16)
        kb_ref[:, even] = jnp.where(low, 0.0, sw).astype(BF16)
        kb_ref[:, odd] = jnp.where(low, 0.0, kr).astype(BF16)

    vt_ref[...] = lax.dot_general(wvt_ref[...], xb, _NT, preferred_element_type=F32).astype(BF16)

    c0 = aw + 2 * kvw
    z = proj(c0, cw) * proj(c0 + 2 * cw, cw)

    @pl.when(i % tiles_per_seq == 0)
    def _():
        zbuf_ref[0:SUBLANES, :] = jnp.zeros((SUBLANES, cw), F32)

    zbuf_ref[SUBLANES:SUBLANES + tm, :] = z
    z1 = zbuf_ref[SUBLANES - 1:SUBLANES - 1 + tm, :]
    z2 = zbuf_ref[SUBLANES - 2:SUBLANES - 2 + tm, :]
    conv = cw_ref[0:1, :] * z2 + cw_ref[1:2, :] * z1 + cw_ref[2:3, :] * z
    y = proj(c0 + cw, cw) * conv
    co_ref[...] = _rms_norm(y, gc_ref[...]).astype(BF16)
    zbuf_ref[0:SUBLANES, :] = zbuf_ref[tm:tm + SUBLANES, :]


def _attn_body(sinks_ref, q_ref, kac_ref, kap_ref, kbc_ref, kbp_ref, vtc_ref, vtp_ref, co_ref, x_ref,
               gat_ref, wo_ref, g1_ref, b1_ref, h_ref, at_ref, *, n_q_heads, alpha, tiles_per_seq):
    tm, aw = q_ref.shape
    W = WINDOW
    group = n_q_heads // N_KV_HEADS
    assert group == 4 and HEAD_DIM * 2 == LANES
    i = pl.program_id(0)
    prev_bias = jnp.where(i % tiles_per_seq == 0, -jnp.inf, 0.0)

    kj = lax.broadcasted_iota(jnp.int32, (W, 2 * W), 0)
    qi = lax.broadcasted_iota(jnp.int32, (W, 2 * W), 1) % W
    use_cur = kj <= qi
    cur_mask = jnp.where(use_cur, 1.0, 0.0).astype(BF16)
    prev_mask = jnp.where(use_cur, 0.0, 1.0).astype(BF16)
    ones_rows = jnp.ones((BF16_ROWS, 2 * W), BF16)

    halves = ((kac_ref, kap_ref), (kbc_ref, kbp_ref))
    for w in range(tm // W):
        rows = slice(w * W, (w + 1) * W)
        scores = []
        for g in range(N_KV_HEADS):
            qg = jnp.concatenate([q_ref[rows, (2 * g) * LANES:(2 * g + 1) * LANES],
                                  q_ref[rows, (2 * g + 1) * LANES:(2 * g + 2) * LANES]], axis=0)
            blk = slice(g * LANES, (g + 1) * LANES)
            for kc_ref, kp_ref in halves:
                if w == 0:
                    kk = jnp.concatenate([kp_ref[:, blk], kc_ref[0:W, blk]], axis=0)
                else:
                    kk = kc_ref[(w - 1) * W:(w + 1) * W, blk]
                scores.append(lax.dot_general(kk, qg, _NT, preferred_element_type=F32))
        probs = []
        for n, s in enumerate(scores):
            g, half = divmod(n, 2)
            s_prev = s[0:W, :]
            if w == 0:
                s_prev = s_prev + prev_bias
            merged = jnp.where(use_cur, s[W:2 * W, :], s_prev)
            heads = (group * g + half, group * g + 2 + half)
            sink = jnp.concatenate([jnp.full((1, W), sinks_ref[heads[0]], F32),
                                    jnp.full((1, W), sinks_ref[heads[1]], F32)], axis=1)
            m = jnp.maximum(jnp.max(merged, axis=0, keepdims=True), sink)
            e = jnp.exp(merged - m).astype(BF16)
            ek = jnp.concatenate([e * prev_mask, e * cur_mask], axis=0)
            probs.append((heads, ek, jnp.exp(sink - m)))
        for n, (heads, ek, sink_e) in enumerate(probs):
            g = n // 2
            hrows = slice(g * HEAD_DIM, (g + 1) * HEAD_DIM)
            if w == 0:
                vt = jnp.concatenate([vtp_ref[hrows, :], vtc_ref[hrows, 0:W]], axis=1)
            else:
                vt = vtc_ref[hrows, (w - 1) * W:(w + 1) * W]
            vt_aug = jnp.concatenate([vt, ones_rows], axis=0)
            o = jnp.dot(vt_aug, ek, preferred_element_type=F32)
            denom = o[HEAD_DIM:HEAD_DIM + 1, :] + sink_e
            on = o[0:HEAD_DIM, :] * (1.0 / denom)
            for t in range(2):
                hd = heads[t]
                at_ref[hd * HEAD_DIM:(hd + 1) * HEAD_DIM, rows] = on[:, t * W:(t + 1) * W]

    at = at_ref[...]
    ms = jnp.mean(at * at, axis=0, keepdims=True)
    gat = jnp.concatenate([gat_ref[...]] * (tm // LANES), axis=1)
    atn = (at * lax.rsqrt(ms + RMS_EPS) * gat).astype(BF16)
    mix = lax.dot_general(atn, wo_ref[0:aw, :], _TN, preferred_element_type=F32)
    mix = mix + jnp.dot(co_ref[...], wo_ref[aw:, :], preferred_element_type=F32)
    h_ref[...] = _layer_norm(alpha * x_ref[...] + mix, g1_ref[...], b1_ref[...])


def _ffn_body(h_ref, wg_ref, wu_ref, wd_ref, g2_ref, b2_ref, o_ref, hb_ref, acc_ref, *, alpha):
    j = pl.program_id(1)

    @pl.when(j == 0)
    def _():
        hb_ref[...] = h_ref[...].astype(BF16)
        acc_ref[...] = jnp.zeros(acc_ref.shape, F32)

    hb = hb_ref[...]
    gate = jnp.dot(hb, wg_ref[...], preferred_element_type=F32)
    up = jnp.dot(hb, wu_ref[...], preferred_element_type=F32)
    act = (gate * (1.0 / (1.0 + jnp.exp(-gate))) * up).astype(BF16)
    acc_ref[...] += jnp.dot(act, wd_ref[...], preferred_element_type=F32)

    @pl.when(j == pl.num_programs(1) - 1)
    def _():
        o_ref[...] = _layer_norm(alpha * h_ref[...] + acc_ref[...], g2_ref[...], b2_ref[...])


def _rope_lane_tables():
    inv_freq = ROPE_THETA ** (-jnp.arange(0, ROT_DIM, 2, dtype=F32) / ROT_DIM)
    lane = np.arange(LANES) % HEAD_DIM
    half = ROT_DIM // 2
    rotated = lane < ROT_DIM
    invf = jnp.where(jnp.asarray(rotated), inv_freq[jnp.asarray(lane % half)], 0.0)
    m_lo = np.where(lane < half, -1.0, 0.0).astype(np.float32)
    m_hi = np.where(rotated & (lane >= half), 1.0, 0.0).astype(np.float32)
    return invf.reshape(1, LANES).astype(F32), jnp.asarray(m_lo).reshape(1, LANES), jnp.asarray(m_hi).reshape(1, LANES)


def _layer(h, pos_col, w_in, conv_w, sinks, g_attn, g_conv, w_out, ln1_g, ln1_b,
           w_gate, w_up, w_down, ln2_g, ln2_b, *, seq, alpha):
    m, d = h.shape
    aw = g_attn.shape[0]
    cw = g_conv.shape[0]
    kvw = N_KV_HEADS * HEAD_DIM
    n_q_heads = aw // HEAD_DIM
    d_ff = w_gate.shape[1]
    in_width = w_in.shape[1]
    assert in_width == aw + 2 * kvw + 3 * cw and d == aw + cw
    assert seq % TM_PROJ == 0 and seq % TM_ATTN == 0 and TM_ATTN % WINDOW == 0
    assert m % TM_FFN == 0 and d_ff % TF_FFN == 0

    invf, m_lo, m_hi = _rope_lane_tables()
    row = lambda a: a.reshape(1, -1).astype(F32)
    cparams = functools.partial(pltpu.CompilerParams, vmem_limit_bytes=VMEM_LIMIT_BYTES)
    w_in_b = w_in.astype(BF16)
    w_v_t = w_in_b[:, aw + kvw:aw + 2 * kvw].T

    tm = TM_PROJ
    const = lambda i: (0, 0)
    q, ka, kb, vt, co = pl.pallas_call(
        functools.partial(_proj_body, dims=(aw, kvw, cw, HEAD_DIM ** -0.5), tiles_per_seq=seq // tm),
        grid=(m // tm,),
        in_specs=[
            pl.BlockSpec((tm, d), lambda i: (i, 0)),
            pl.BlockSpec((tm, 1), lambda i: (i, 0)),
            _resident((1, LANES), const),
            _resident((1, LANES), const),
            _resident((1, LANES), const),
            _resident((d, in_width), const),
            _resident((kvw, d), const),
            _resident((CONV_WIDTH, cw), const),
            _resident((1, cw), const),
        ],
        out_specs=[
            pl.BlockSpec((tm, aw), lambda i: (i, 0)),
            pl.BlockSpec((tm, 2 * kvw), lambda i: (i, 0)),
            pl.BlockSpec((tm, 2 * kvw), lambda i: (i, 0)),
            pl.BlockSpec((kvw, tm), lambda i: (0, i)),
            pl.BlockSpec((tm, cw), lambda i: (i, 0)),
        ],
        out_shape=[
            jax.ShapeDtypeStruct((m, aw), BF16),
            jax.ShapeDtypeStruct((m, 2 * kvw), BF16),
            jax.ShapeDtypeStruct((m, 2 * kvw), BF16),
            jax.ShapeDtypeStruct((kvw, m), BF16),
            jax.ShapeDtypeStruct((m, cw), BF16),
        ],
        scratch_shapes=[pltpu.VMEM((tm + 2 * SUBLANES, cw), F32)],
        compiler_params=cparams(dimension_semantics=("arbitrary",)),
        name="proj_rope_conv",
    )(h, pos_col, invf, m_lo, m_hi, w_in_b, w_v_t, conv_w.astype(F32), row(g_conv))

    tm = TM_ATTN
    wpt = tm // WINDOW
    cur = lambda i: (i, 0)
    prev = lambda i: (jnp.maximum(i * wpt - 1, 0), 0)
    cur_t = lambda i: (0, i)
    prev_t = lambda i: (0, jnp.maximum(i * wpt - 1, 0))
    g_attn_cols = jnp.broadcast_to(g_attn.astype(F32)[:, None], (aw, LANES))
    h1 = pl.pallas_call(
        functools.partial(_attn_body, n_q_heads=n_q_heads, alpha=alpha, tiles_per_seq=seq // tm),
        grid=(m // tm,),
        in_specs=[
            pl.BlockSpec(memory_space=pltpu.SMEM),
            pl.BlockSpec((tm, aw), cur),
            pl.BlockSpec((tm, 2 * kvw), cur),
            pl.BlockSpec((WINDOW, 2 * kvw), prev),
            pl.BlockSpec((tm, 2 * kvw), cur),
            pl.BlockSpec((WINDOW, 2 * kvw), prev),
            pl.BlockSpec((kvw, tm), cur_t),
            pl.BlockSpec((kvw, WINDOW), prev_t),
            pl.BlockSpec((tm, cw), cur),
            pl.BlockSpec((tm, d), cur),
            _resident((aw, LANES), const),
            _resident((d, d), const),
            _resident((1, d), const),
            _resident((1, d), const),
        ],
        out_specs=pl.BlockSpec((tm, d), cur),
        out_shape=jax.ShapeDtypeStruct((m, d), F32),
        scratch_shapes=[pltpu.VMEM((aw, tm), F32)],
        compiler_params=cparams(dimension_semantics=("arbitrary",)),
        name="swa_outproj_ln",
    )(sinks.astype(F32), q, ka, ka, kb, kb, vt, vt, co, h, g_attn_cols, w_out.astype(BF16),
      row(ln1_g), row(ln1_b))

    tm, tf = TM_FFN, TF_FFN
    out = pl.pallas_call(
        functools.partial(_ffn_body, alpha=alpha),
        grid=(m // tm, d_ff // tf),
        in_specs=[
            pl.BlockSpec((tm, d), lambda i, j: (i, 0)),
            pl.BlockSpec((d, tf), lambda i, j: (0, j)),
            pl.BlockSpec((d, tf), lambda i, j: (0, j)),
            pl.BlockSpec((tf, d), lambda i, j: (j, 0)),
            _resident((1, d), lambda i, j: (0, 0)),
            _resident((1, d), lambda i, j: (0, 0)),
        ],
        out_specs=pl.BlockSpec((tm, d), lambda i, j: (i, 0)),
        out_shape=jax.ShapeDtypeStruct((m, d), F32),
        scratch_shapes=[pltpu.VMEM((tm, d), BF16), pltpu.VMEM((tm, d), F32)],
        compiler_params=cparams(dimension_semantics=("arbitrary", "arbitrary")),
        name="swiglu_ffn_ln",
    )(h1, w_gate.astype(BF16), w_up.astype(BF16), w_down.astype(BF16), row(ln2_g), row(ln2_b))
    return out


def kernel(x, positions, w_in, conv_w, sinks, g_attn, g_conv, w_out, ln1_g, ln1_b, w_gate, w_up, w_down, ln2_g, ln2_b):
    b, s, d = x.shape
    depth = w_in.shape[0]
    alpha = (2 * depth) ** 0.25
    h = x.reshape(b * s, d)
    pos_col = positions.reshape(b * s, 1)
    for l in range(depth):
        h = _layer(h, pos_col, w_in[l], conv_w[l], sinks[l], g_attn[l], g_conv[l], w_out[l],
                   ln1_g[l], ln1_b[l], w_gate[l], w_up[l], w_down[l], ln2_g[l], ln2_b[l],
                   seq=s, alpha=alpha)
    return h.reshape(b, s, d)
```

```python
import functools

import numpy as np
import jax
import jax.numpy as jnp
from jax import lax
from jax.experimental import pallas as pl
from jax.experimental.pallas import tpu as pltpu

F32 = jnp.float32
BF16 = jnp.bfloat16

HEAD_DIM = 64
N_KV_HEADS = 4
WINDOW = 128
ROT_DIM = HEAD_DIM // 4
ROPE_THETA = 500000.0
CONV_WIDTH = 3
LN_EPS = 1e-5
RMS_EPS = 1e-6

LANES = 128
SUBLANES = 8
BF16_ROWS = 16
VMEM_LIMIT_BYTES = 60000 * 1024

TM_PROJ = 512
TM_ATTN = 512
SUB_ATTN = 256
TM_FFN = 512
TF_FFN = 512

_NT = (((1,), (1,)), ((), ()))
_TN = (((0,), (0,)), ((), ()))


def _resident(block_shape, index_map):
    return pl.BlockSpec(block_shape, index_map, pipeline_mode=pl.Buffered(1))


def _layer_norm(x, g, b):
    mu = jnp.mean(x, axis=-1, keepdims=True)
    xc = x - mu
    var = jnp.mean(xc * xc, axis=-1, keepdims=True)
    return xc * lax.rsqrt(var + LN_EPS) * g + b


def _rms_norm(x, g):
    ms = jnp.mean(x * x, axis=-1, keepdims=True)
    return x * lax.rsqrt(ms + RMS_EPS) * g


def _proj_body(x_ref, pos_ref, invf_ref, m1_ref, m2_ref, w_ref, wvt_ref, cw_ref, gc_ref,
               q_ref, ka_ref, kb_ref, vt_ref, co_ref, zbuf_ref, *, dims, tiles_per_seq):
    aw, kvw, cw, attn_scale = dims
    tm = x_ref.shape[0]
    i = pl.program_id(0)

    @pl.when(i % tiles_per_seq == 0)
    def _():
        zbuf_ref[0:SUBLANES, :] = jnp.zeros((SUBLANES, cw), F32)

    xb = x_ref[...].astype(BF16)

    ang = pos_ref[...].astype(F32) * invf_ref[...]
    cosv = jnp.cos(ang)
    sinv = jnp.sin(ang)
    s_lo = sinv * m1_ref[...]
    s_hi = sinv * m2_ref[...]

    def proj(lo, width):
        return jnp.dot(xb, w_ref[:, lo:lo + width], preferred_element_type=F32)

    c0 = aw + 2 * kvw
    z = proj(c0, cw) * proj(c0 + 2 * cw, cw)
    zbuf_ref[SUBLANES:SUBLANES + tm, :] = z
    z1 = zbuf_ref[SUBLANES - 1:SUBLANES - 1 + tm, :]
    z2 = zbuf_ref[SUBLANES - 2:SUBLANES - 2 + tm, :]
    conv = cw_ref[0:1, :] * z2 + cw_ref[1:2, :] * z1 + cw_ref[2:3, :] * z
    y = proj(c0 + cw, cw) * conv
    co_ref[...] = _rms_norm(y, gc_ref[...]).astype(BF16)
    zbuf_ref[0:SUBLANES, :] = zbuf_ref[tm:tm + SUBLANES, :]

    def rope(tc):
        up = pltpu.roll(tc, LANES - ROT_DIM // 2, 1)
        dn = pltpu.roll(tc, ROT_DIM // 2, 1)
        return tc * cosv + up * s_lo + dn * s_hi

    q = proj(0, aw)
    for c in range(aw // LANES):
        blk = slice(c * LANES, (c + 1) * LANES)
        q_ref[:, blk] = (rope(q[:, blk]) * attn_scale).astype(BF16)

    k = proj(aw, kvw)
    low = lax.broadcasted_iota(jnp.int32, (tm, LANES), 1) < HEAD_DIM
    for c in range(kvw // LANES):
        kr = rope(k[:, c * LANES:(c + 1) * LANES])
        sw = pltpu.roll(kr, HEAD_DIM, 1)
        even = slice(2 * c * LANES, (2 * c + 1) * LANES)
        odd = slice((2 * c + 1) * LANES, (2 * c + 2) * LANES)
        ka_ref[:, even] = jnp.where(low, kr, 0.0).astype(BF16)
        ka_ref[:, odd] = jnp.where(low, sw, 0.0).astype(BF16)
        kb_ref[:, even] = jnp.where(low, 0.0, sw).astype(BF16)
        kb_ref[:, odd] = jnp.where(low, 0.0, kr).astype(BF16)

    vt_ref[...] = lax.dot_general(wvt_ref[...], xb, _NT, preferred_element_type=F32).astype(BF16)


def _attn_body(sinks_ref, q_ref, kac_ref, kap_ref, kbc_ref, kbp_ref, vtc_ref, vtp_ref, co_ref, x_ref,
               gat_ref, wo_ref, g1_ref, b1_ref, h_ref, hb_ref, at_ref, *, n_q_heads, alpha, tiles_per_seq):
    tm, aw = q_ref.shape
    W = WINDOW
    group = n_q_heads // N_KV_HEADS
    assert group == 4 and HEAD_DIM * 2 == LANES
    i = pl.program_id(0)
    prev_bias = jnp.where(i % tiles_per_seq == 0, -jnp.inf, 0.0)

    kj = lax.broadcasted_iota(jnp.int32, (W, 2 * W), 0)
    qi = lax.broadcasted_iota(jnp.int32, (W, 2 * W), 1) % W
    use_cur = kj <= qi
    cur_mask = jnp.where(use_cur, 1.0, 0.0).astype(BF16)
    prev_mask = jnp.where(use_cur, 0.0, 1.0).astype(BF16)
    ones_rows = jnp.ones((BF16_ROWS, 2 * W), BF16)

    halves = ((kac_ref, kap_ref), (kbc_ref, kbp_ref))

    def window_scores(w):
        rows = slice(w * W, (w + 1) * W)
        scores = []
        for g in range(N_KV_HEADS):
            qg = jnp.concatenate([q_ref[rows, (2 * g) * LANES:(2 * g + 1) * LANES],
                                  q_ref[rows, (2 * g + 1) * LANES:(2 * g + 2) * LANES]], axis=0)
            blk = slice(g * LANES, (g + 1) * LANES)
            for kc_ref, kp_ref in halves:
                if w == 0:
                    kk = jnp.concatenate([kp_ref[:, blk], kc_ref[0:W, blk]], axis=0)
                else:
                    kk = kc_ref[(w - 1) * W:(w + 1) * W, blk]
                scores.append(lax.dot_general(kk, qg, _NT, preferred_element_type=F32))
        return scores

    def window_values(w, scores):
        rows = slice(w * W, (w + 1) * W)
        probs = []
        for n, s in enumerate(scores):
            g, half = divmod(n, 2)
            s_prev = s[0:W, :]
            if w == 0:
                s_prev = s_prev + prev_bias
            merged = jnp.where(use_cur, s[W:2 * W, :], s_prev)
            heads = (group * g + half, group * g + 2 + half)
            sink = jnp.concatenate([jnp.full((1, W), sinks_ref[heads[0]], F32),
                                    jnp.full((1, W), sinks_ref[heads[1]], F32)], axis=1)
            m = jnp.maximum(jnp.max(merged, axis=0, keepdims=True), sink)
            e = jnp.exp(merged - m).astype(BF16)
            ek = jnp.concatenate([e * prev_mask, e * cur_mask], axis=0)
            probs.append((heads, ek, jnp.exp(sink - m)))
        for n, (heads, ek, sink_e) in enumerate(probs):
            g = n // 2
            hrows = slice(g * HEAD_DIM, (g + 1) * HEAD_DIM)
            if w == 0:
                vt = jnp.concatenate([vtp_ref[hrows, :], vtc_ref[hrows, 0:W]], axis=1)
            else:
                vt = vtc_ref[hrows, (w - 1) * W:(w + 1) * W]
            vt_aug = jnp.concatenate([vt, ones_rows], axis=0)
            o = jnp.dot(vt_aug, ek, preferred_element_type=F32)
            denom = o[HEAD_DIM:HEAD_DIM + 1, :] + sink_e
            on = o[0:HEAD_DIM, :] * (1.0 / denom)
            for t in range(2):
                hd = heads[t]
                at_ref[hd * HEAD_DIM:(hd + 1) * HEAD_DIM, rows] = on[:, t * W:(t + 1) * W]

    def out_proj(rows):
        at = at_ref[:, rows]
        ms = jnp.mean(at * at, axis=0, keepdims=True)
        gat = jnp.concatenate([gat_ref[...]] * ((rows.stop - rows.start) // LANES), axis=1)
        atn = (at * lax.rsqrt(ms + RMS_EPS) * gat).astype(BF16)
        mix = lax.dot_general(atn, wo_ref[0:aw, :], _TN, preferred_element_type=F32)
        return mix + jnp.dot(co_ref[rows, :], wo_ref[aw:, :], preferred_element_type=F32)

    def finish(rows, mix):
        h = _layer_norm(alpha * x_ref[rows, :] + mix, g1_ref[...], b1_ref[...])
        h_ref[rows, :] = h
        hb_ref[rows, :] = h.astype(BF16)

    wpc = SUB_ATTN // W
    n_chunks = tm // SUB_ATTN
    for w in range(wpc):
        window_values(w, window_scores(w))
    for c in range(n_chunks):
        rows = slice(c * SUB_ATTN, (c + 1) * SUB_ATTN)
        nxt = [(w, window_scores(w)) for w in range((c + 1) * wpc, (c + 2) * wpc)] if c + 1 < n_chunks else []
        mix = out_proj(rows)
        for w, scores in nxt:
            window_values(w, scores)
        finish(rows, mix)


def _ffn_body(hf_ref, hb_ref, wgu_ref, wd_ref, g2_ref, b2_ref, o_ref, acc_ref, *, alpha, n_row_tiles):
    i = pl.program_id(0)
    j = pl.program_id(1)
    tf = wd_ref.shape[0]

    def accumulate(first):
        gu = jnp.dot(hb_ref[...], wgu_ref[...], preferred_element_type=F32)
        gate = gu[:, 0:tf]
        act = (gate * (1.0 / (1.0 + jnp.exp(-gate))) * gu[:, tf:2 * tf]).astype(BF16)
        down = jnp.dot(act, wd_ref[...], preferred_element_type=F32)
        base = alpha * hf_ref[...] if first else acc_ref[...]
        acc_ref[...] = base + down

    @pl.when(jnp.logical_and(i == 0, j == 0))
    def _():
        acc_ref[...] = jnp.zeros(acc_ref.shape, F32)

    @pl.when(j == 0)
    def _():
        o_ref[...] = _layer_norm(acc_ref[...], g2_ref[...], b2_ref[...])
        accumulate(True)

    @pl.when(jnp.logical_and(j > 0, i < n_row_tiles))
    def _():
        accumulate(False)


def _rope_lane_tables():
    inv_freq = ROPE_THETA ** (-jnp.arange(0, ROT_DIM, 2, dtype=F32) / ROT_DIM)
    lane = np.arange(LANES) % HEAD_DIM
    half = ROT_DIM // 2
    rotated = lane < ROT_DIM
    invf = jnp.where(jnp.asarray(rotated), inv_freq[jnp.asarray(lane % half)], 0.0)
    m_lo = np.where(lane < half, -1.0, 0.0).astype(np.float32)
    m_hi = np.where(rotated & (lane >= half), 1.0, 0.0).astype(np.float32)
    return invf.reshape(1, LANES).astype(F32), jnp.asarray(m_lo).reshape(1, LANES), jnp.asarray(m_hi).reshape(1, LANES)


def _layer(h, pos_col, w_in, conv_w, sinks, g_attn, g_conv, w_out, ln1_g, ln1_b,
           w_gate, w_up, w_down, ln2_g, ln2_b, *, seq, alpha):
    m, d = h.shape
    aw = g_attn.shape[0]
    cw = g_conv.shape[0]
    kvw = N_KV_HEADS * HEAD_DIM
    n_q_heads = aw // HEAD_DIM
    d_ff = w_gate.shape[1]
    in_width = w_in.shape[1]
    assert in_width == aw + 2 * kvw + 3 * cw and d == aw + cw
    assert seq % TM_PROJ == 0 and seq % TM_ATTN == 0 and TM_ATTN % SUB_ATTN == 0 and SUB_ATTN % WINDOW == 0
    assert m % TM_FFN == 0 and d_ff % TF_FFN == 0

    invf, m_lo, m_hi = _rope_lane_tables()
    row = lambda a: a.reshape(1, -1).astype(F32)
    cparams = functools.partial(pltpu.CompilerParams, vmem_limit_bytes=VMEM_LIMIT_BYTES)
    w_in_b = w_in.astype(BF16)
    w_v_t = w_in_b[:, aw + kvw:aw + 2 * kvw].T

    tm = TM_PROJ
    const = lambda i: (0, 0)
    q, ka, kb, vt, co = pl.pallas_call(
        functools.partial(_proj_body, dims=(aw, kvw, cw, HEAD_DIM ** -0.5), tiles_per_seq=seq // tm),
        grid=(m // tm,),
        in_specs=[
            pl.BlockSpec((tm, d), lambda i: (i, 0)),
            pl.BlockSpec((tm, 1), lambda i: (i, 0)),
            _resident((1, LANES), const),
            _resident((1, LANES), const),
            _resident((1, LANES), const),
            _resident((d, in_width), const),
            _resident((kvw, d), const),
            _resident((CONV_WIDTH, cw), const),
            _resident((1, cw), const),
        ],
        out_specs=[
            pl.BlockSpec((tm, aw), lambda i: (i, 0)),
            pl.BlockSpec((tm, 2 * kvw), lambda i: (i, 0)),
            pl.BlockSpec((tm, 2 * kvw), lambda i: (i, 0)),
            pl.BlockSpec((kvw, tm), lambda i: (0, i)),
            pl.BlockSpec((tm, cw), lambda i: (i, 0)),
        ],
        out_shape=[
            jax.ShapeDtypeStruct((m, aw), BF16),
            jax.ShapeDtypeStruct((m, 2 * kvw), BF16),
            jax.ShapeDtypeStruct((m, 2 * kvw), BF16),
            jax.ShapeDtypeStruct((kvw, m), BF16),
            jax.ShapeDtypeStruct((m, cw), BF16),
        ],
        scratch_shapes=[pltpu.VMEM((tm + 2 * SUBLANES, cw), F32)],
        compiler_params=cparams(dimension_semantics=("arbitrary",)),
        name="proj_rope_conv",
    )(h, pos_col, invf, m_lo, m_hi, w_in_b, w_v_t, conv_w.astype(F32), row(g_conv))

    tm = TM_ATTN
    wpt = tm // WINDOW
    cur = lambda i: (i, 0)
    prev = lambda i: (jnp.maximum(i * wpt - 1, 0), 0)
    cur_t = lambda i: (0, i)
    prev_t = lambda i: (0, jnp.maximum(i * wpt - 1, 0))
    g_attn_cols = jnp.broadcast_to(g_attn.astype(F32)[:, None], (aw, LANES))
    h1, h1b = pl.pallas_call(
        functools.partial(_attn_body, n_q_heads=n_q_heads, alpha=alpha, tiles_per_seq=seq // tm),
        grid=(m // tm,),
        in_specs=[
            pl.BlockSpec(memory_space=pltpu.SMEM),
            pl.BlockSpec((tm, aw), cur),
            pl.BlockSpec((tm, 2 * kvw), cur),
            pl.BlockSpec((WINDOW, 2 * kvw), prev),
            pl.BlockSpec((tm, 2 * kvw), cur),
            pl.BlockSpec((WINDOW, 2 * kvw), prev),
            pl.BlockSpec((kvw, tm), cur_t),
            pl.BlockSpec((kvw, WINDOW), prev_t),
            pl.BlockSpec((tm, cw), cur),
            pl.BlockSpec((tm, d), cur),
            _resident((aw, LANES), const),
            _resident((d, d), const),
            _resident((1, d), const),
            _resident((1, d), const),
        ],
        out_specs=[pl.BlockSpec((tm, d), cur), pl.BlockSpec((tm, d), cur)],
        out_shape=[jax.ShapeDtypeStruct((m, d), F32), jax.ShapeDtypeStruct((m, d), BF16)],
        scratch_shapes=[pltpu.VMEM((aw, tm), F32)],
        compiler_params=cparams(dimension_semantics=("arbitrary",)),
        name="swa_outproj_ln",
    )(sinks.astype(F32), q, ka, ka, kb, kb, vt, vt, co, h, g_attn_cols, w_out.astype(BF16),
      row(ln1_g), row(ln1_b))

    tm, tf = TM_FFN, TF_FFN
    n_i, n_j = m // tm, d_ff // tf
    w_gu = jnp.concatenate([w_gate.reshape(d, n_j, tf), w_up.reshape(d, n_j, tf)], axis=2)
    w_gu = w_gu.reshape(d, 2 * d_ff).astype(BF16)
    row_i = lambda i, j: (jnp.minimum(i, n_i - 1), 0)
    col_j = lambda i, j: jnp.where(i < n_i, j, n_j - 1)
    out_i = lambda i, j: (jnp.clip(jnp.where(j == 0, i - 1, i), 0, n_i - 1), 0)
    out = pl.pallas_call(
        functools.partial(_ffn_body, alpha=alpha, n_row_tiles=n_i),
        grid=(n_i + 1, n_j),
        in_specs=[
            pl.BlockSpec((tm, d), row_i),
            pl.BlockSpec((tm, d), row_i),
            pl.BlockSpec((d, 2 * tf), lambda i, j: (0, col_j(i, j))),
            pl.BlockSpec((tf, d), lambda i, j: (col_j(i, j), 0)),
            _resident((1, d), lambda i, j: (0, 0)),
            _resident((1, d), lambda i, j: (0, 0)),
        ],
        out_specs=pl.BlockSpec((tm, d), out_i),
        out_shape=jax.ShapeDtypeStruct((m, d), F32),
        scratch_shapes=[pltpu.VMEM((tm, d), F32)],
        compiler_params=cparams(dimension_semantics=("arbitrary", "arbitrary")),
        name="swiglu_ffn_ln",
    )(h1, h1b, w_gu, w_down.astype(BF16), row(ln2_g), row(ln2_b))
    return out


def kernel(x, positions, w_in, conv_w, sinks, g_attn, g_conv, w_out, ln1_g, ln1_b, w_gate, w_up, w_down, ln2_g, ln2_b):
    b, s, d = x.shape
    depth = w_in.shape[0]
    alpha = (2 * depth) ** 0.25
    h = x.reshape(b * s, d)
    pos_col = positions.reshape(b * s, 1)
    for l in range(depth):
        h = _layer(h, pos_col, w_in[l], conv_w[l], sinks[l], g_attn[l], g_conv[l], w_out[l],
                   ln1_g[l], ln1_b[l], w_gate[l], w_up[l], w_down[l], ln2_g[l], ln2_b[l],
                   seq=s, alpha=alpha)
    return h.reshape(b, s, d)
```

```python
import functools

import numpy as np
import jax
import jax.numpy as jnp
from jax import lax
from jax.experimental import pallas as pl
from jax.experimental.pallas import tpu as pltpu

F32 = jnp.float32
BF16 = jnp.bfloat16

HEAD_DIM = 64
N_KV_HEADS = 4
WINDOW = 128
ROT_DIM = HEAD_DIM // 4
ROPE_THETA = 500000.0
CONV_WIDTH = 3
LN_EPS = 1e-5
RMS_EPS = 1e-6

LANES = 128
SUBLANES = 8
BF16_ROWS = 16
VMEM_LIMIT_BYTES = 60000 * 1024

TM_PROJ = 512
TM_ATTN = 512
SUB_ATTN = 256
TM_FFN = 512
TF_FFN = 512

_NT = (((1,), (1,)), ((), ()))
_TN = (((0,), (0,)), ((), ()))


def _resident(block_shape, index_map):
    return pl.BlockSpec(block_shape, index_map, pipeline_mode=pl.Buffered(1))


def _layer_norm(x, g, b):
    mu = jnp.mean(x, axis=-1, keepdims=True)
    xc = x - mu
    var = jnp.mean(xc * xc, axis=-1, keepdims=True)
    return xc * lax.rsqrt(var + LN_EPS) * g + b


def _rms_norm(x, g):
    ms = jnp.mean(x * x, axis=-1, keepdims=True)
    return x * lax.rsqrt(ms + RMS_EPS) * g


def _proj_body(x_ref, pos_ref, invf_ref, m1_ref, m2_ref, w_ref, wvt_ref, cw_ref, gc_ref,
               wg_ref, wu_ref, wo_ref,
               q_ref, ka_ref, kb_ref, vt_ref, co_ref, wgu_ref, wob_ref, zbuf_ref, *, dims, tiles_per_seq, tf):
    aw, kvw, cw, attn_scale = dims
    tm = x_ref.shape[0]
    i = pl.program_id(0)

    @pl.when(i % tiles_per_seq == 0)
    def _():
        zbuf_ref[0:SUBLANES, :] = jnp.zeros((SUBLANES, cw), F32)

    for j in range(wg_ref.shape[1] // tf):
        src = slice(j * tf, (j + 1) * tf)
        wgu_ref[:, 2 * j * tf:(2 * j + 1) * tf] = wg_ref[:, src].astype(BF16)
        wgu_ref[:, (2 * j + 1) * tf:(2 * j + 2) * tf] = wu_ref[:, src].astype(BF16)
    wob_ref[...] = wo_ref[...].astype(BF16)

    xb = x_ref[...].astype(BF16)

    ang = pos_ref[...].astype(F32) * invf_ref[...]
    cosv = jnp.cos(ang)
    sinv = jnp.sin(ang)
    s_lo = sinv * m1_ref[...]
    s_hi = sinv * m2_ref[...]

    def proj(lo, width):
        return jnp.dot(xb, w_ref[:, lo:lo + width], preferred_element_type=F32)

    c0 = aw + 2 * kvw
    z = proj(c0, cw) * proj(c0 + 2 * cw, cw)
    zbuf_ref[SUBLANES:SUBLANES + tm, :] = z
    z1 = zbuf_ref[SUBLANES - 1:SUBLANES - 1 + tm, :]
    z2 = zbuf_ref[SUBLANES - 2:SUBLANES - 2 + tm, :]
    conv = cw_ref[0:1, :] * z2 + cw_ref[1:2, :] * z1 + cw_ref[2:3, :] * z
    y = proj(c0 + cw, cw) * conv
    co_ref[...] = _rms_norm(y, gc_ref[...]).astype(BF16)
    zbuf_ref[0:SUBLANES, :] = zbuf_ref[tm:tm + SUBLANES, :]

    def rope(tc):
        up = pltpu.roll(tc, LANES - ROT_DIM // 2, 1)
        dn = pltpu.roll(tc, ROT_DIM // 2, 1)
        return tc * cosv + up * s_lo + dn * s_hi

    q = proj(0, aw)
    for c in range(aw // LANES):
        blk = slice(c * LANES, (c + 1) * LANES)
        q_ref[:, blk] = (rope(q[:, blk]) * attn_scale).astype(BF16)

    k = proj(aw, kvw)
    low = lax.broadcasted_iota(jnp.int32, (tm, LANES), 1) < HEAD_DIM
    for c in range(kvw // LANES):
        kr = rope(k[:, c * LANES:(c + 1) * LANES])
        sw = pltpu.roll(kr, HEAD_DIM, 1)
        even = slice(2 * c * LANES, (2 * c + 1) * LANES)
        odd = slice((2 * c + 1) * LANES, (2 * c + 2) * LANES)
        ka_ref[:, even] = jnp.where(low, kr, 0.0).astype(BF16)
        ka_ref[:, odd] = jnp.where(low, sw, 0.0).astype(BF16)
        kb_ref[:, even] = jnp.where(low, 0.0, sw).astype(BF16)
        kb_ref[:, odd] = jnp.where(low, 0.0, kr).astype(BF16)

    vt_ref[...] = lax.dot_general(wvt_ref[...], xb, _NT, preferred_element_type=F32).astype(BF16)


def _attn_body(sinks_ref, q_ref, kac_ref, kap_ref, kbc_ref, kbp_ref, vtc_ref, vtp_ref, co_ref, x_ref,
               gat_ref, wo_ref, g1_ref, b1_ref, wd_ref, h_ref, hb_ref, wdb_ref, at_ref,
               *, n_q_heads, alpha, tiles_per_seq):
    tm, aw = q_ref.shape
    wdb_ref[...] = wd_ref[...].astype(BF16)
    W = WINDOW
    group = n_q_heads // N_KV_HEADS
    assert group == 4 and HEAD_DIM * 2 == LANES
    i = pl.program_id(0)
    prev_bias = jnp.where(i % tiles_per_seq == 0, -jnp.inf, 0.0)

    kj = lax.broadcasted_iota(jnp.int32, (W, 2 * W), 0)
    qi = lax.broadcasted_iota(jnp.int32, (W, 2 * W), 1) % W
    use_cur = kj <= qi
    cur_mask = jnp.where(use_cur, 1.0, 0.0).astype(BF16)
    prev_mask = jnp.where(use_cur, 0.0, 1.0).astype(BF16)
    ones_rows = jnp.ones((BF16_ROWS, 2 * W), BF16)

    halves = ((kac_ref, kap_ref), (kbc_ref, kbp_ref))

    def window_scores(w):
        rows = slice(w * W, (w + 1) * W)
        scores = []
        for g in range(N_KV_HEADS):
            qg = jnp.concatenate([q_ref[rows, (2 * g) * LANES:(2 * g + 1) * LANES],
                                  q_ref[rows, (2 * g + 1) * LANES:(2 * g + 2) * LANES]], axis=0)
            blk = slice(g * LANES, (g + 1) * LANES)
            for kc_ref, kp_ref in halves:
                if w == 0:
                    kk = jnp.concatenate([kp_ref[:, blk], kc_ref[0:W, blk]], axis=0)
                else:
                    kk = kc_ref[(w - 1) * W:(w + 1) * W, blk]
                scores.append(lax.dot_general(kk, qg, _NT, preferred_element_type=F32))
        return scores

    def window_values(w, scores):
        rows = slice(w * W, (w + 1) * W)
        probs = []
        for n, s in enumerate(scores):
            g, half = divmod(n, 2)
            s_prev = s[0:W, :]
            if w == 0:
                s_prev = s_prev + prev_bias
            merged = jnp.where(use_cur, s[W:2 * W, :], s_prev)
            heads = (group * g + half, group * g + 2 + half)
            sink = jnp.concatenate([jnp.full((1, W), sinks_ref[heads[0]], F32),
                                    jnp.full((1, W), sinks_ref[heads[1]], F32)], axis=1)
            m = jnp.maximum(jnp.max(merged, axis=0, keepdims=True), sink)
            e = jnp.exp(merged - m).astype(BF16)
            ek = jnp.concatenate([e * prev_mask, e * cur_mask], axis=0)
            probs.append((heads, ek, jnp.exp(sink - m)))
        for n, (heads, ek, sink_e) in enumerate(probs):
            g = n // 2
            hrows = slice(g * HEAD_DIM, (g + 1) * HEAD_DIM)
            if w == 0:
                vt = jnp.concatenate([vtp_ref[hrows, :], vtc_ref[hrows, 0:W]], axis=1)
            else:
                vt = vtc_ref[hrows, (w - 1) * W:(w + 1) * W]
            vt_aug = jnp.concatenate([vt, ones_rows], axis=0)
            o = jnp.dot(vt_aug, ek, preferred_element_type=F32)
            denom = o[HEAD_DIM:HEAD_DIM + 1, :] + sink_e
            on = o[0:HEAD_DIM, :] * (1.0 / denom)
            for t in range(2):
                hd = heads[t]
                at_ref[hd * HEAD_DIM:(hd + 1) * HEAD_DIM, rows] = on[:, t * W:(t + 1) * W]

    def out_proj(rows):
        at = at_ref[:, rows]
        ms = jnp.mean(at * at, axis=0, keepdims=True)
        gat = jnp.concatenate([gat_ref[...]] * ((rows.stop - rows.start) // LANES), axis=1)
        atn = (at * lax.rsqrt(ms + RMS_EPS) * gat).astype(BF16)
        mix = lax.dot_general(atn, wo_ref[0:aw, :], _TN, preferred_element_type=F32)
        return mix + jnp.dot(co_ref[rows, :], wo_ref[aw:, :], preferred_element_type=F32)

    def finish(rows, mix):
        h = _layer_norm(alpha * x_ref[rows, :] + mix, g1_ref[...], b1_ref[...])
        h_ref[rows, :] = h
        hb_ref[rows, :] = h.astype(BF16)

    wpc = SUB_ATTN // W
    n_chunks = tm // SUB_ATTN
    for w in range(wpc):
        window_values(w, window_scores(w))
    for c in range(n_chunks):
        rows = slice(c * SUB_ATTN, (c + 1) * SUB_ATTN)
        nxt = [(w, window_scores(w)) for w in range((c + 1) * wpc, (c + 2) * wpc)] if c + 1 < n_chunks else []
        mix = out_proj(rows)
        for w, scores in nxt:
            window_values(w, scores)
        finish(rows, mix)


def _ffn_body(hf_ref, hb_ref, wgu_ref, wd_ref, g2_ref, b2_ref, o_ref, acc_ref, *, alpha, n_row_tiles):
    i = pl.program_id(0)
    j = pl.program_id(1)
    tf = wd_ref.shape[0]

    def accumulate(first):
        gu = jnp.dot(hb_ref[...], wgu_ref[...], preferred_element_type=F32)
        gate = gu[:, 0:tf]
        act = (gate * (1.0 / (1.0 + jnp.exp(-gate))) * gu[:, tf:2 * tf]).astype(BF16)
        down = jnp.dot(act, wd_ref[...], preferred_element_type=F32)
        base = alpha * hf_ref[...] if first else acc_ref[...]
        acc_ref[...] = base + down

    @pl.when(jnp.logical_and(i == 0, j == 0))
    def _():
        acc_ref[...] = jnp.zeros(acc_ref.shape, F32)

    @pl.when(j == 0)
    def _():
        o_ref[...] = _layer_norm(acc_ref[...], g2_ref[...], b2_ref[...])
        accumulate(True)

    @pl.when(jnp.logical_and(j > 0, i < n_row_tiles))
    def _():
        accumulate(False)


def _rope_lane_tables():
    inv_freq = ROPE_THETA ** (-jnp.arange(0, ROT_DIM, 2, dtype=F32) / ROT_DIM)
    lane = np.arange(LANES) % HEAD_DIM
    half = ROT_DIM // 2
    rotated = lane < ROT_DIM
    invf = jnp.where(jnp.asarray(rotated), inv_freq[jnp.asarray(lane % half)], 0.0)
    m_lo = np.where(lane < half, -1.0, 0.0).astype(np.float32)
    m_hi = np.where(rotated & (lane >= half), 1.0, 0.0).astype(np.float32)
    return invf.reshape(1, LANES).astype(F32), jnp.asarray(m_lo).reshape(1, LANES), jnp.asarray(m_hi).reshape(1, LANES)


def _layer(h, pos_col, w_in, conv_w, sinks, g_attn, g_conv, w_out, ln1_g, ln1_b,
           w_gate, w_up, w_down, ln2_g, ln2_b, *, seq, alpha):
    m, d = h.shape
    aw = g_attn.shape[0]
    cw = g_conv.shape[0]
    kvw = N_KV_HEADS * HEAD_DIM
    n_q_heads = aw // HEAD_DIM
    d_ff = w_gate.shape[1]
    in_width = w_in.shape[1]
    assert in_width == aw + 2 * kvw + 3 * cw and d == aw + cw
    assert seq % TM_PROJ == 0 and seq % TM_ATTN == 0 and TM_ATTN % SUB_ATTN == 0 and SUB_ATTN % WINDOW == 0
    assert m % TM_FFN == 0 and d_ff % TF_FFN == 0

    invf, m_lo, m_hi = _rope_lane_tables()
    row = lambda a: a.reshape(1, -1).astype(F32)
    cparams = functools.partial(pltpu.CompilerParams, vmem_limit_bytes=VMEM_LIMIT_BYTES)
    w_in_b = w_in.astype(BF16)
    w_v_t = w_in[:, aw + kvw:aw + 2 * kvw].T.astype(BF16)

    tm = TM_PROJ
    tf = TF_FFN
    steps = m // tm
    assert d % (steps * BF16_ROWS) == 0
    slab = d // steps
    const = lambda i: (0, 0)
    by_row = lambda i: (i, 0)
    q, ka, kb, vt, co, w_gu, w_out_b = pl.pallas_call(
        functools.partial(_proj_body, dims=(aw, kvw, cw, HEAD_DIM ** -0.5), tiles_per_seq=seq // tm, tf=tf),
        grid=(steps,),
        in_specs=[
            pl.BlockSpec((tm, d), by_row),
            pl.BlockSpec((tm, 1), by_row),
            _resident((1, LANES), const),
            _resident((1, LANES), const),
            _resident((1, LANES), const),
            _resident((d, in_width), const),
            _resident((kvw, d), const),
            _resident((CONV_WIDTH, cw), const),
            _resident((1, cw), const),
            pl.BlockSpec((slab, d_ff), by_row),
            pl.BlockSpec((slab, d_ff), by_row),
            pl.BlockSpec((slab, d), by_row),
        ],
        out_specs=[
            pl.BlockSpec((tm, aw), by_row),
            pl.BlockSpec((tm, 2 * kvw), by_row),
            pl.BlockSpec((tm, 2 * kvw), by_row),
            pl.BlockSpec((kvw, tm), lambda i: (0, i)),
            pl.BlockSpec((tm, cw), by_row),
            pl.BlockSpec((slab, 2 * d_ff), by_row),
            pl.BlockSpec((slab, d), by_row),
        ],
        out_shape=[
            jax.ShapeDtypeStruct((m, aw), BF16),
            jax.ShapeDtypeStruct((m, 2 * kvw), BF16),
            jax.ShapeDtypeStruct((m, 2 * kvw), BF16),
            jax.ShapeDtypeStruct((kvw, m), BF16),
            jax.ShapeDtypeStruct((m, cw), BF16),
            jax.ShapeDtypeStruct((d, 2 * d_ff), BF16),
            jax.ShapeDtypeStruct((d, d), BF16),
        ],
        scratch_shapes=[pltpu.VMEM((tm + 2 * SUBLANES, cw), F32)],
        compiler_params=cparams(dimension_semantics=("arbitrary",)),
        name="proj_rope_conv",
    )(h, pos_col, invf, m_lo, m_hi, w_in_b, w_v_t, conv_w.astype(F32), row(g_conv), w_gate, w_up, w_out)

    tm = TM_ATTN
    wpt = tm // WINDOW
    cur = lambda i: (i, 0)
    prev = lambda i: (jnp.maximum(i * wpt - 1, 0), 0)
    cur_t = lambda i: (0, i)
    prev_t = lambda i: (0, jnp.maximum(i * wpt - 1, 0))
    g_attn_cols = jnp.broadcast_to(g_attn.astype(F32)[:, None], (aw, LANES))
    steps = m // tm
    assert d_ff % (steps * BF16_ROWS) == 0
    slab = d_ff // steps
    h1, h1b, w_down_b = pl.pallas_call(
        functools.partial(_attn_body, n_q_heads=n_q_heads, alpha=alpha, tiles_per_seq=seq // tm),
        grid=(steps,),
        in_specs=[
            pl.BlockSpec(memory_space=pltpu.SMEM),
            pl.BlockSpec((tm, aw), cur),
            pl.BlockSpec((tm, 2 * kvw), cur),
            pl.BlockSpec((WINDOW, 2 * kvw), prev),
            pl.BlockSpec((tm, 2 * kvw), cur),
            pl.BlockSpec((WINDOW, 2 * kvw), prev),
            pl.BlockSpec((kvw, tm), cur_t),
            pl.BlockSpec((kvw, WINDOW), prev_t),
            pl.BlockSpec((tm, cw), cur),
            pl.BlockSpec((tm, d), cur),
            _resident((aw, LANES), const),
            _resident((d, d), const),
            _resident((1, d), const),
            _resident((1, d), const),
            pl.BlockSpec((slab, d), cur),
        ],
        out_specs=[pl.BlockSpec((tm, d), cur), pl.BlockSpec((tm, d), cur), pl.BlockSpec((slab, d), cur)],
        out_shape=[jax.ShapeDtypeStruct((m, d), F32), jax.ShapeDtypeStruct((m, d), BF16),
                   jax.ShapeDtypeStruct((d_ff, d), BF16)],
        scratch_shapes=[pltpu.VMEM((aw, tm), F32)],
        compiler_params=cparams(dimension_semantics=("arbitrary",)),
        name="swa_outproj_ln",
    )(sinks.astype(F32), q, ka, ka, kb, kb, vt, vt, co, h, g_attn_cols, w_out_b,
      row(ln1_g), row(ln1_b), w_down)

    tm, tf = TM_FFN, TF_FFN
    n_i, n_j = m // tm, d_ff // tf
    row_i = lambda i, j: (jnp.minimum(i, n_i - 1), 0)
    last_col = n_j - 1 if (n_i - 1) % 2 == 0 else 0

    def col_j(i, j):
        return jnp.where(i < n_i, jnp.where(i % 2 == 0, j, n_j - 1 - j), last_col)

    out_i = lambda i, j: (jnp.clip(jnp.where(j == 0, i - 1, i), 0, n_i - 1), 0)
    out = pl.pallas_call(
        functools.partial(_ffn_body, alpha=alpha, n_row_tiles=n_i),
        grid=(n_i + 1, n_j),
        in_specs=[
            pl.BlockSpec((tm, d), row_i),
            pl.BlockSpec((tm, d), row_i),
            pl.BlockSpec((d, 2 * tf), lambda i, j: (0, col_j(i, j))),
            pl.BlockSpec((tf, d), lambda i, j: (col_j(i, j), 0)),
            _resident((1, d), lambda i, j: (0, 0)),
            _resident((1, d), lambda i, j: (0, 0)),
        ],
        out_specs=pl.BlockSpec((tm, d), out_i),
        out_shape=jax.ShapeDtypeStruct((m, d), F32),
        scratch_shapes=[pltpu.VMEM((tm, d), F32)],
        compiler_params=cparams(dimension_semantics=("arbitrary", "arbitrary")),
        name="swiglu_ffn_ln",
    )(h1, h1b, w_gu, w_down_b, row(ln2_g), row(ln2_b))
    return out


def kernel(x, positions, w_in, conv_w, sinks, g_attn, g_conv, w_out, ln1_g, ln1_b, w_gate, w_up, w_down, ln2_g, ln2_b):
    b, s, d = x.shape
    depth = w_in.shape[0]
    alpha = (2 * depth) ** 0.25
    h = x.reshape(b * s, d)
    pos_col = positions.reshape(b * s, 1)
    for l in range(depth):
        h = _layer(h, pos_col, w_in[l], conv_w[l], sinks[l], g_attn[l], g_conv[l], w_out[l],
                   ln1_g[l], ln1_b[l], w_gate[l], w_up[l], w_down[l], ln2_g[l], ln2_b[l],
                   seq=s, alpha=alpha)
    return h.reshape(b, s, d)
```

```python
import functools

import numpy as np
import jax
import jax.numpy as jnp
from jax import lax
from jax.experimental import pallas as pl
from jax.experimental.pallas import tpu as pltpu

F32 = jnp.float32
BF16 = jnp.bfloat16

HEAD_DIM = 64
N_KV_HEADS = 4
WINDOW = 128
ROT_DIM = HEAD_DIM // 4
ROPE_THETA = 500000.0
CONV_WIDTH = 3
LN_EPS = 1e-5
RMS_EPS = 1e-6

LANES = 128
SUBLANES = 8
BF16_ROWS = 16
VMEM_LIMIT_BYTES = 60000 * 1024

TM_PROJ = 512
TM_ATTN = 512
SUB_ATTN = 256
TM_FFN = 512
TF_FFN = 512

_NT = (((1,), (1,)), ((), ()))
_TN = (((0,), (0,)), ((), ()))


def _resident(block_shape, index_map):
    return pl.BlockSpec(block_shape, index_map, pipeline_mode=pl.Buffered(1))


def _layer_norm(x, g, b):
    mu = jnp.mean(x, axis=-1, keepdims=True)
    xc = x - mu
    var = jnp.mean(xc * xc, axis=-1, keepdims=True)
    return xc * lax.rsqrt(var + LN_EPS) * g + b


def _rms_norm(x, g):
    ms = jnp.mean(x * x, axis=-1, keepdims=True)
    return x * lax.rsqrt(ms + RMS_EPS) * g


def _proj_body(x_ref, pos_ref, invf_ref, m1_ref, m2_ref, w_ref, cw_ref, gc_ref,
               wg_ref, wu_ref, wo_ref,
               q_ref, ka_ref, kb_ref, vt_ref, co_ref, wgu_ref, wob_ref, zbuf_ref, *, dims, tiles_per_seq, tf):
    aw, kvw, cw, attn_scale = dims
    tm = x_ref.shape[0]
    i = pl.program_id(0)

    @pl.when(i % tiles_per_seq == 0)
    def _():
        zbuf_ref[0:SUBLANES, :] = jnp.zeros((SUBLANES, cw), F32)

    for j in range(wg_ref.shape[1] // tf):
        src = slice(j * tf, (j + 1) * tf)
        wgu_ref[:, 2 * j * tf:(2 * j + 1) * tf] = wg_ref[:, src].astype(BF16)
        wgu_ref[:, (2 * j + 1) * tf:(2 * j + 2) * tf] = wu_ref[:, src].astype(BF16)
    wob_ref[...] = wo_ref[...].astype(BF16)

    xb = x_ref[...].astype(BF16)

    ang = pos_ref[...].astype(F32) * invf_ref[...]
    cosv = jnp.cos(ang)
    sinv = jnp.sin(ang)
    s_lo = sinv * m1_ref[...]
    s_hi = sinv * m2_ref[...]

    def proj(lo, width):
        return jnp.dot(xb, w_ref[:, lo:lo + width], preferred_element_type=F32)

    c0 = aw + 2 * kvw
    z = proj(c0, cw) * proj(c0 + 2 * cw, cw)
    zbuf_ref[SUBLANES:SUBLANES + tm, :] = z
    z1 = zbuf_ref[SUBLANES - 1:SUBLANES - 1 + tm, :]
    z2 = zbuf_ref[SUBLANES - 2:SUBLANES - 2 + tm, :]
    conv = cw_ref[0:1, :] * z2 + cw_ref[1:2, :] * z1 + cw_ref[2:3, :] * z
    y = proj(c0 + cw, cw) * conv
    co_ref[...] = _rms_norm(y, gc_ref[...]).astype(BF16)
    zbuf_ref[0:SUBLANES, :] = zbuf_ref[tm:tm + SUBLANES, :]

    def rope(tc):
        up = pltpu.roll(tc, LANES - ROT_DIM // 2, 1)
        dn = pltpu.roll(tc, ROT_DIM // 2, 1)
        return tc * cosv + up * s_lo + dn * s_hi

    q = proj(0, aw)
    for c in range(aw // LANES):
        blk = slice(c * LANES, (c + 1) * LANES)
        q_ref[:, blk] = (rope(q[:, blk]) * attn_scale).astype(BF16)

    k = proj(aw, kvw)
    low = lax.broadcasted_iota(jnp.int32, (tm, LANES), 1) < HEAD_DIM
    for c in range(kvw // LANES):
        kr = rope(k[:, c * LANES:(c + 1) * LANES])
        sw = pltpu.roll(kr, HEAD_DIM, 1)
        even = slice(2 * c * LANES, (2 * c + 1) * LANES)
        odd = slice((2 * c + 1) * LANES, (2 * c + 2) * LANES)
        ka_ref[:, even] = jnp.where(low, kr, 0.0).astype(BF16)
        ka_ref[:, odd] = jnp.where(low, sw, 0.0).astype(BF16)
        kb_ref[:, even] = jnp.where(low, 0.0, sw).astype(BF16)
        kb_ref[:, odd] = jnp.where(low, 0.0, kr).astype(BF16)

    vt_ref[...] = proj(aw + kvw, kvw).T.astype(BF16)


def _attn_body(sinks_ref, q_ref, kac_ref, kap_ref, kbc_ref, kbp_ref, vtc_ref, vtp_ref, co_ref, x_ref,
               gat_ref, wo_ref, g1_ref, b1_ref, wd_ref, h_ref, wdb_ref, at_ref,
               *, n_q_heads, alpha, tiles_per_seq):
    tm, aw = q_ref.shape
    wdb_ref[...] = wd_ref[...].astype(BF16)
    W = WINDOW
    group = n_q_heads // N_KV_HEADS
    assert group == 4 and HEAD_DIM * 2 == LANES
    i = pl.program_id(0)
    prev_bias = jnp.where(i % tiles_per_seq == 0, -jnp.inf, 0.0)

    kj = lax.broadcasted_iota(jnp.int32, (W, 2 * W), 0)
    qi = lax.broadcasted_iota(jnp.int32, (W, 2 * W), 1) % W
    use_cur = kj <= qi
    cur_mask = jnp.where(use_cur, 1.0, 0.0).astype(BF16)
    prev_mask = jnp.where(use_cur, 0.0, 1.0).astype(BF16)
    ones_rows = jnp.ones((BF16_ROWS, 2 * W), BF16)

    halves = ((kac_ref, kap_ref), (kbc_ref, kbp_ref))

    def window_scores(w):
        rows = slice(w * W, (w + 1) * W)
        scores = []
        for g in range(N_KV_HEADS):
            qg = jnp.concatenate([q_ref[rows, (2 * g) * LANES:(2 * g + 1) * LANES],
                                  q_ref[rows, (2 * g + 1) * LANES:(2 * g + 2) * LANES]], axis=0)
            blk = slice(g * LANES, (g + 1) * LANES)
            for kc_ref, kp_ref in halves:
                if w == 0:
                    kk = jnp.concatenate([kp_ref[:, blk], kc_ref[0:W, blk]], axis=0)
                else:
                    kk = kc_ref[(w - 1) * W:(w + 1) * W, blk]
                scores.append(lax.dot_general(kk, qg, _NT, preferred_element_type=F32))
        return scores

    def window_values(w, scores):
        rows = slice(w * W, (w + 1) * W)
        probs = []
        for n, s in enumerate(scores):
            g, half = divmod(n, 2)
            s_prev = s[0:W, :]
            if w == 0:
                s_prev = s_prev + prev_bias
            merged = jnp.where(use_cur, s[W:2 * W, :], s_prev)
            heads = (group * g + half, group * g + 2 + half)
            sink = jnp.concatenate([jnp.full((1, W), sinks_ref[heads[0]], F32),
                                    jnp.full((1, W), sinks_ref[heads[1]], F32)], axis=1)
            m = jnp.maximum(jnp.max(merged, axis=0, keepdims=True), sink)
            e = jnp.exp(merged - m).astype(BF16)
            ek = jnp.concatenate([e * prev_mask, e * cur_mask], axis=0)
            probs.append((heads, ek, jnp.exp(sink - m)))
        for n, (heads, ek, sink_e) in enumerate(probs):
            g = n // 2
            hrows = slice(g * HEAD_DIM, (g + 1) * HEAD_DIM)
            if w == 0:
                vt = jnp.concatenate([vtp_ref[hrows, :], vtc_ref[hrows, 0:W]], axis=1)
            else:
                vt = vtc_ref[hrows, (w - 1) * W:(w + 1) * W]
            vt_aug = jnp.concatenate([vt, ones_rows], axis=0)
            o = jnp.dot(vt_aug, ek, preferred_element_type=F32)
            denom = o[HEAD_DIM:HEAD_DIM + 1, :] + sink_e
            on = o[0:HEAD_DIM, :] * (1.0 / denom)
            for t in range(2):
                hd = heads[t]
                at_ref[hd * HEAD_DIM:(hd + 1) * HEAD_DIM, rows] = on[:, t * W:(t + 1) * W]

    def out_proj(rows):
        at = at_ref[:, rows]
        ms = jnp.mean(at * at, axis=0, keepdims=True)
        gat = jnp.concatenate([gat_ref[...]] * ((rows.stop - rows.start) // LANES), axis=1)
        atn = (at * lax.rsqrt(ms + RMS_EPS) * gat).astype(BF16)
        mix = lax.dot_general(atn, wo_ref[0:aw, :], _TN, preferred_element_type=F32)
        return mix + jnp.dot(co_ref[rows, :], wo_ref[aw:, :], preferred_element_type=F32)

    def finish(rows, mix):
        h_ref[rows, :] = _layer_norm(alpha * x_ref[rows, :] + mix, g1_ref[...], b1_ref[...])

    wpc = SUB_ATTN // W
    n_chunks = tm // SUB_ATTN
    for w in range(wpc):
        window_values(w, window_scores(w))
    pending = None
    for c in range(n_chunks):
        rows = slice(c * SUB_ATTN, (c + 1) * SUB_ATTN)
        nxt = [(w, window_scores(w)) for w in range((c + 1) * wpc, (c + 2) * wpc)] if c + 1 < n_chunks else []
        mix = out_proj(rows)
        if pending is not None:
            finish(*pending)
        for w, scores in nxt:
            window_values(w, scores)
        pending = (rows, mix)
    finish(*pending)


def _ffn_body(h_ref, wgu_ref, wd_ref, g2_ref, b2_ref, o_ref, hb_ref, acc_ref, *, alpha):
    j = pl.program_id(1)
    tf = wd_ref.shape[0]

    @pl.when(j == 0)
    def _():
        h = h_ref[...]
        hb_ref[...] = h.astype(BF16)
        acc_ref[...] = alpha * h

    gu = jnp.dot(hb_ref[...], wgu_ref[...], preferred_element_type=F32)
    gate = gu[:, 0:tf]
    act = (gate * (1.0 / (1.0 + jnp.exp(-gate))) * gu[:, tf:2 * tf]).astype(BF16)
    acc_ref[...] += jnp.dot(act, wd_ref[...], preferred_element_type=F32)

    @pl.when(j == pl.num_programs(1) - 1)
    def _():
        o_ref[...] = _layer_norm(acc_ref[...], g2_ref[...], b2_ref[...])


def _rope_lane_tables():
    inv_freq = ROPE_THETA ** (-jnp.arange(0, ROT_DIM, 2, dtype=F32) / ROT_DIM)
    lane = np.arange(LANES) % HEAD_DIM
    half = ROT_DIM // 2
    rotated = lane < ROT_DIM
    invf = jnp.where(jnp.asarray(rotated), inv_freq[jnp.asarray(lane % half)], 0.0)
    m_lo = np.where(lane < half, -1.0, 0.0).astype(np.float32)
    m_hi = np.where(rotated & (lane >= half), 1.0, 0.0).astype(np.float32)
    return invf.reshape(1, LANES).astype(F32), jnp.asarray(m_lo).reshape(1, LANES), jnp.asarray(m_hi).reshape(1, LANES)


def _layer(h, pos_col, w_in, conv_w, sinks, g_attn, g_conv, w_out, ln1_g, ln1_b,
           w_gate, w_up, w_down, ln2_g, ln2_b, *, seq, alpha):
    m, d = h.shape
    aw = g_attn.shape[0]
    cw = g_conv.shape[0]
    kvw = N_KV_HEADS * HEAD_DIM
    n_q_heads = aw // HEAD_DIM
    d_ff = w_gate.shape[1]
    in_width = w_in.shape[1]
    assert in_width == aw + 2 * kvw + 3 * cw and d == aw + cw
    assert seq % TM_PROJ == 0 and seq % TM_ATTN == 0 and TM_ATTN % SUB_ATTN == 0 and SUB_ATTN % WINDOW == 0
    assert m % TM_FFN == 0 and d_ff % TF_FFN == 0

    invf, m_lo, m_hi = _rope_lane_tables()
    row = lambda a: a.reshape(1, -1).astype(F32)
    cparams = functools.partial(pltpu.CompilerParams, vmem_limit_bytes=VMEM_LIMIT_BYTES)
    w_in_b = w_in.astype(BF16)

    tm = TM_PROJ
    tf = TF_FFN
    steps = m // tm
    assert d % (steps * BF16_ROWS) == 0
    slab = d // steps
    const = lambda i: (0, 0)
    by_row = lambda i: (i, 0)
    q, ka, kb, vt, co, w_gu, w_out_b = pl.pallas_call(
        functools.partial(_proj_body, dims=(aw, kvw, cw, HEAD_DIM ** -0.5), tiles_per_seq=seq // tm, tf=tf),
        grid=(steps,),
        in_specs=[
            pl.BlockSpec((tm, d), by_row),
            pl.BlockSpec((tm, 1), by_row),
            _resident((1, LANES), const),
            _resident((1, LANES), const),
            _resident((1, LANES), const),
            _resident((d, in_width), const),
            _resident((CONV_WIDTH, cw), const),
            _resident((1, cw), const),
            pl.BlockSpec((slab, d_ff), by_row),
            pl.BlockSpec((slab, d_ff), by_row),
            pl.BlockSpec((slab, d), by_row),
        ],
        out_specs=[
            pl.BlockSpec((tm, aw), by_row),
            pl.BlockSpec((tm, 2 * kvw), by_row),
            pl.BlockSpec((tm, 2 * kvw), by_row),
            pl.BlockSpec((kvw, tm), lambda i: (0, i)),
            pl.BlockSpec((tm, cw), by_row),
            pl.BlockSpec((slab, 2 * d_ff), by_row),
            pl.BlockSpec((slab, d), by_row),
        ],
        out_shape=[
            jax.ShapeDtypeStruct((m, aw), BF16),
            jax.ShapeDtypeStruct((m, 2 * kvw), BF16),
            jax.ShapeDtypeStruct((m, 2 * kvw), BF16),
            jax.ShapeDtypeStruct((kvw, m), BF16),
            jax.ShapeDtypeStruct((m, cw), BF16),
            jax.ShapeDtypeStruct((d, 2 * d_ff), BF16),
            jax.ShapeDtypeStruct((d, d), BF16),
        ],
        scratch_shapes=[pltpu.VMEM((tm + 2 * SUBLANES, cw), F32)],
        compiler_params=cparams(dimension_semantics=("arbitrary",)),
        name="proj_rope_conv",
    )(h, pos_col, invf, m_lo, m_hi, w_in_b, conv_w.astype(F32), row(g_conv), w_gate, w_up, w_out)

    tm = TM_ATTN
    wpt = tm // WINDOW
    cur = lambda i: (i, 0)
    prev = lambda i: (jnp.maximum(i * wpt - 1, 0), 0)
    cur_t = lambda i: (0, i)
    prev_t = lambda i: (0, jnp.maximum(i * wpt - 1, 0))
    g_attn_cols = jnp.broadcast_to(g_attn.astype(F32)[:, None], (aw, LANES))
    steps = m // tm
    assert d_ff % (steps * BF16_ROWS) == 0
    slab = d_ff // steps
    h1, w_down_b = pl.pallas_call(
        functools.partial(_attn_body, n_q_heads=n_q_heads, alpha=alpha, tiles_per_seq=seq // tm),
        grid=(steps,),
        in_specs=[
            pl.BlockSpec(memory_space=pltpu.SMEM),
            pl.BlockSpec((tm, aw), cur),
            pl.BlockSpec((tm, 2 * kvw), cur),
            pl.BlockSpec((WINDOW, 2 * kvw), prev),
            pl.BlockSpec((tm, 2 * kvw), cur),
            pl.BlockSpec((WINDOW, 2 * kvw), prev),
            pl.BlockSpec((kvw, tm), cur_t),
            pl.BlockSpec((kvw, WINDOW), prev_t),
            pl.BlockSpec((tm, cw), cur),
            pl.BlockSpec((tm, d), cur),
            _resident((aw, LANES), const),
            _resident((d, d), const),
            _resident((1, d), const),
            _resident((1, d), const),
            pl.BlockSpec((slab, d), cur),
        ],
        out_specs=[pl.BlockSpec((tm, d), cur), pl.BlockSpec((slab, d), cur)],
        out_shape=[jax.ShapeDtypeStruct((m, d), F32), jax.ShapeDtypeStruct((d_ff, d), BF16)],
        scratch_shapes=[pltpu.VMEM((aw, tm), F32)],
        compiler_params=cparams(dimension_semantics=("arbitrary",)),
        name="swa_outproj_ln",
    )(sinks.astype(F32), q, ka, ka, kb, kb, vt, vt, co, h, g_attn_cols, w_out_b,
      row(ln1_g), row(ln1_b), w_down)

    tm, tf = TM_FFN, TF_FFN
    n_i, n_j = m // tm, d_ff // tf
    col_j = lambda i, j: jnp.where(i % 2 == 0, j, n_j - 1 - j)
    out = pl.pallas_call(
        functools.partial(_ffn_body, alpha=alpha),
        grid=(n_i, n_j),
        in_specs=[
            pl.BlockSpec((tm, d), lambda i, j: (i, 0)),
            pl.BlockSpec((d, 2 * tf), lambda i, j: (0, col_j(i, j))),
            pl.BlockSpec((tf, d), lambda i, j: (col_j(i, j), 0)),
            _resident((1, d), lambda i, j: (0, 0)),
            _resident((1, d), lambda i, j: (0, 0)),
        ],
        out_specs=pl.BlockSpec((tm, d), lambda i, j: (i, 0)),
        out_shape=jax.ShapeDtypeStruct((m, d), F32),
        scratch_shapes=[pltpu.VMEM((tm, d), BF16), pltpu.VMEM((tm, d), F32)],
        compiler_params=cparams(dimension_semantics=("arbitrary", "arbitrary")),
        name="swiglu_ffn_ln",
    )(h1, w_gu, w_down_b, row(ln2_g), row(ln2_b))
    return out


def kernel(x, positions, w_in, conv_w, sinks, g_attn, g_conv, w_out, ln1_g, ln1_b, w_gate, w_up, w_down, ln2_g, ln2_b):
    b, s, d = x.shape
    depth = w_in.shape[0]
    alpha = (2 * depth) ** 0.25
    h = x.reshape(b * s, d)
    pos_col = positions.reshape(b * s, 1)
    for l in range(depth):
        h = _layer(h, pos_col, w_in[l], conv_w[l], sinks[l], g_attn[l], g_conv[l], w_out[l],
                   ln1_g[l], ln1_b[l], w_gate[l], w_up[l], w_down[l], ln2_g[l], ln2_b[l],
                   seq=s, alpha=alpha)
    return h.reshape(b, s, d)
```

```python
import functools

import numpy as np
import jax
import jax.numpy as jnp
from jax import lax
from jax.experimental import pallas as pl
from jax.experimental.pallas import tpu as pltpu

F32 = jnp.float32
BF16 = jnp.bfloat16

HEAD_DIM = 64
N_KV_HEADS = 4
WINDOW = 128
ROT_DIM = HEAD_DIM // 4
ROPE_THETA = 500000.0
CONV_WIDTH = 3
LN_EPS = 1e-5
RMS_EPS = 1e-6

LANES = 128
SUBLANES = 8
BF16_ROWS = 16
VMEM_LIMIT_BYTES = 60000 * 1024

TM_PROJ = 512
TM_ATTN = 512
SUB_ATTN = 256
TM_FFN = 512
TF_FFN = 512

_NT = (((1,), (1,)), ((), ()))
_TN = (((0,), (0,)), ((), ()))


def _resident(block_shape, index_map):
    return pl.BlockSpec(block_shape, index_map, pipeline_mode=pl.Buffered(1))


def _layer_norm(x, g, b):
    mu = jnp.mean(x, axis=-1, keepdims=True)
    xc = x - mu
    var = jnp.mean(xc * xc, axis=-1, keepdims=True)
    return xc * lax.rsqrt(var + LN_EPS) * g + b


def _rms_norm(x, g):
    ms = jnp.mean(x * x, axis=-1, keepdims=True)
    return x * lax.rsqrt(ms + RMS_EPS) * g


def _proj_body(x_ref, pos_ref, invf_ref, m1_ref, m2_ref, w_ref, cw_ref, gc_ref,
               wg_ref, wu_ref, wo_ref,
               q_ref, ka_ref, kb_ref, vt_ref, co_ref, wgu_ref, wob_ref, zbuf_ref, *, dims, tiles_per_seq, tf):
    aw, kvw, cw, attn_scale = dims
    tm = x_ref.shape[0]
    i = pl.program_id(0)

    @pl.when(i % tiles_per_seq == 0)
    def _():
        zbuf_ref[0:SUBLANES, :] = jnp.zeros((SUBLANES, cw), F32)

    for j in range(wg_ref.shape[1] // tf):
        src = slice(j * tf, (j + 1) * tf)
        wgu_ref[:, 2 * j * tf:(2 * j + 1) * tf] = wg_ref[:, src].astype(BF16)
        wgu_ref[:, (2 * j + 1) * tf:(2 * j + 2) * tf] = wu_ref[:, src].astype(BF16)
    wob_ref[...] = wo_ref[...].astype(BF16)

    xb = x_ref[...].astype(BF16)

    ang = pos_ref[...].astype(F32) * invf_ref[...]
    cosv = jnp.cos(ang)
    sinv = jnp.sin(ang)
    s_lo = sinv * m1_ref[...]
    s_hi = sinv * m2_ref[...]

    def proj(lo, width):
        return jnp.dot(xb, w_ref[:, lo:lo + width], preferred_element_type=F32)

    c0 = aw + 2 * kvw
    z = proj(c0, cw) * proj(c0 + 2 * cw, cw)
    zbuf_ref[SUBLANES:SUBLANES + tm, :] = z
    z1 = zbuf_ref[SUBLANES - 1:SUBLANES - 1 + tm, :]
    z2 = zbuf_ref[SUBLANES - 2:SUBLANES - 2 + tm, :]
    conv = cw_ref[0:1, :] * z2 + cw_ref[1:2, :] * z1 + cw_ref[2:3, :] * z
    y = proj(c0 + cw, cw) * conv
    co_ref[...] = _rms_norm(y, gc_ref[...]).astype(BF16)
    zbuf_ref[0:SUBLANES, :] = zbuf_ref[tm:tm + SUBLANES, :]

    def rope(tc):
        up = pltpu.roll(tc, LANES - ROT_DIM // 2, 1)
        dn = pltpu.roll(tc, ROT_DIM // 2, 1)
        return tc * cosv + up * s_lo + dn * s_hi

    q = proj(0, aw)
    for c in range(aw // LANES):
        blk = slice(c * LANES, (c + 1) * LANES)
        q_ref[:, blk] = (rope(q[:, blk]) * attn_scale).astype(BF16)

    k = proj(aw, kvw)
    low = lax.broadcasted_iota(jnp.int32, (tm, LANES), 1) < HEAD_DIM
    for c in range(kvw // LANES):
        kr = rope(k[:, c * LANES:(c + 1) * LANES])
        sw = pltpu.roll(kr, HEAD_DIM, 1)
        even = slice(2 * c * LANES, (2 * c + 1) * LANES)
        odd = slice((2 * c + 1) * LANES, (2 * c + 2) * LANES)
        ka_ref[:, even] = jnp.where(low, kr, 0.0).astype(BF16)
        ka_ref[:, odd] = jnp.where(low, sw, 0.0).astype(BF16)
        kb_ref[:, even] = jnp.where(low, 0.0, sw).astype(BF16)
        kb_ref[:, odd] = jnp.where(low, 0.0, kr).astype(BF16)

    vt_ref[...] = proj(aw + kvw, kvw).T.astype(BF16)


def _attn_body(sinks_ref, q_ref, kac_ref, kap_ref, kbc_ref, kbp_ref, vtc_ref, vtp_ref, co_ref, x_ref,
               gat_ref, wo_ref, g1_ref, b1_ref, wd_ref, h_ref, wdb_ref, at_ref,
               *, n_q_heads, alpha, tiles_per_seq):
    tm, aw = q_ref.shape
    wdb_ref[...] = wd_ref[...].astype(BF16)
    W = WINDOW
    group = n_q_heads // N_KV_HEADS
    assert group == 4 and HEAD_DIM * 2 == LANES
    i = pl.program_id(0)
    prev_bias = jnp.where(i % tiles_per_seq == 0, -jnp.inf, 0.0)

    kj = lax.broadcasted_iota(jnp.int32, (W, 2 * W), 0)
    qi = lax.broadcasted_iota(jnp.int32, (W, 2 * W), 1) % W
    use_cur = kj <= qi
    cur_mask = jnp.where(use_cur, 1.0, 0.0).astype(BF16)
    prev_mask = jnp.where(use_cur, 0.0, 1.0).astype(BF16)
    ones_rows = jnp.ones((BF16_ROWS, 2 * W), BF16)

    halves = ((kac_ref, kap_ref), (kbc_ref, kbp_ref))

    def window_scores(w):
        rows = slice(w * W, (w + 1) * W)
        scores = []
        for g in range(N_KV_HEADS):
            qg = jnp.concatenate([q_ref[rows, (2 * g) * LANES:(2 * g + 1) * LANES],
                                  q_ref[rows, (2 * g + 1) * LANES:(2 * g + 2) * LANES]], axis=0)
            blk = slice(g * LANES, (g + 1) * LANES)
            for kc_ref, kp_ref in halves:
                if w == 0:
                    kk = jnp.concatenate([kp_ref[:, blk], kc_ref[0:W, blk]], axis=0)
                else:
                    kk = kc_ref[(w - 1) * W:(w + 1) * W, blk]
                scores.append(lax.dot_general(kk, qg, _NT, preferred_element_type=F32))
        return scores

    def window_values(w, scores):
        rows = slice(w * W, (w + 1) * W)
        probs = []
        for n, s in enumerate(scores):
            g, half = divmod(n, 2)
            s_prev = s[0:W, :]
            if w == 0:
                s_prev = s_prev + prev_bias
            merged = jnp.where(use_cur, s[W:2 * W, :], s_prev)
            heads = (group * g + half, group * g + 2 + half)
            sink = jnp.concatenate([jnp.full((1, W), sinks_ref[heads[0]], F32),
                                    jnp.full((1, W), sinks_ref[heads[1]], F32)], axis=1)
            m = jnp.maximum(jnp.max(merged, axis=0, keepdims=True), sink)
            e = jnp.exp(merged - m).astype(BF16)
            ek = jnp.concatenate([e * prev_mask, e * cur_mask], axis=0)
            probs.append((heads, ek, jnp.exp(sink - m)))
        for n, (heads, ek, sink_e) in enumerate(probs):
            g = n // 2
            hrows = slice(g * HEAD_DIM, (g + 1) * HEAD_DIM)
            if w == 0:
                vt = jnp.concatenate([vtp_ref[hrows, :], vtc_ref[hrows, 0:W]], axis=1)
            else:
                vt = vtc_ref[hrows, (w - 1) * W:(w + 1) * W]
            vt_aug = jnp.concatenate([vt, ones_rows], axis=0)
            o = jnp.dot(vt_aug, ek, preferred_element_type=F32)
            denom = o[HEAD_DIM:HEAD_DIM + 1, :] + sink_e
            on = o[0:HEAD_DIM, :] * (1.0 / denom)
            for t in range(2):
                hd = heads[t]
                at_ref[hd * HEAD_DIM:(hd + 1) * HEAD_DIM, rows] = on[:, t * W:(t + 1) * W]

    def out_proj(rows):
        at = at_ref[:, rows]
        ms = jnp.mean(at * at, axis=0, keepdims=True)
        gat = jnp.concatenate([gat_ref[...]] * ((rows.stop - rows.start) // LANES), axis=1)
        atn = (at * lax.rsqrt(ms + RMS_EPS) * gat).astype(BF16)
        mix = lax.dot_general(atn, wo_ref[0:aw, :], _TN, preferred_element_type=F32)
        return mix + jnp.dot(co_ref[rows, :], wo_ref[aw:, :], preferred_element_type=F32)

    def finish(rows, mix):
        h_ref[rows, :] = _layer_norm(alpha * x_ref[rows, :] + mix, g1_ref[...], b1_ref[...])

    wpc = SUB_ATTN // W
    n_chunks = tm // SUB_ATTN
    for w in range(wpc):
        window_values(w, window_scores(w))
    pending = None
    for c in range(n_chunks):
        rows = slice(c * SUB_ATTN, (c + 1) * SUB_ATTN)
        nxt = [(w, window_scores(w)) for w in range((c + 1) * wpc, (c + 2) * wpc)] if c + 1 < n_chunks else []
        mix = out_proj(rows)
        if pending is not None:
            finish(*pending)
        for w, scores in nxt:
            window_values(w, scores)
        pending = (rows, mix)
    finish(*pending)


def _ffn_body(h_ref, wgu_a_ref, wgu_b_ref, wd_a_ref, wd_b_ref, g2_ref, b2_ref, o_ref, hb_ref, acc_ref,
              *, alpha, n_pairs):
    i = pl.program_id(0)
    s = pl.program_id(1)
    n_steps = pl.num_programs(1)
    s_eff = jnp.where(i % 2 == 0, s, n_steps - 1 - s)
    tf = wd_a_ref.shape[0]

    @pl.when(s == 0)
    def _():
        h = h_ref[...]
        hb_ref[...] = h.astype(BF16)
        acc_ref[...] = alpha * h

    def gate_up(w_ref):
        return jnp.dot(hb_ref[...], w_ref[...], preferred_element_type=F32)

    def activation(gu):
        gate = gu[:, 0:tf]
        return (gate * (1.0 / (1.0 + jnp.exp(-gate))) * gu[:, tf:2 * tf]).astype(BF16)

    @pl.when(s_eff < n_pairs)
    def _():
        gu_a = gate_up(wgu_a_ref)
        gu_b = gate_up(wgu_b_ref)
        acc_ref[...] += jnp.dot(activation(gu_a), wd_a_ref[...], preferred_element_type=F32)
        acc_ref[...] += jnp.dot(activation(gu_b), wd_b_ref[...], preferred_element_type=F32)

    @pl.when(s_eff >= n_pairs)
    def _():
        acc_ref[...] += jnp.dot(activation(gate_up(wgu_a_ref)), wd_a_ref[...], preferred_element_type=F32)

    @pl.when(s == n_steps - 1)
    def _():
        o_ref[...] = _layer_norm(acc_ref[...], g2_ref[...], b2_ref[...])


def _rope_lane_tables():
    inv_freq = ROPE_THETA ** (-jnp.arange(0, ROT_DIM, 2, dtype=F32) / ROT_DIM)
    lane = np.arange(LANES) % HEAD_DIM
    half = ROT_DIM // 2
    rotated = lane < ROT_DIM
    invf = jnp.where(jnp.asarray(rotated), inv_freq[jnp.asarray(lane % half)], 0.0)
    m_lo = np.where(lane < half, -1.0, 0.0).astype(np.float32)
    m_hi = np.where(rotated & (lane >= half), 1.0, 0.0).astype(np.float32)
    return invf.reshape(1, LANES).astype(F32), jnp.asarray(m_lo).reshape(1, LANES), jnp.asarray(m_hi).reshape(1, LANES)


def _layer(h, pos_col, w_in, conv_w, sinks, g_attn, g_conv, w_out, ln1_g, ln1_b,
           w_gate, w_up, w_down, ln2_g, ln2_b, *, seq, alpha):
    m, d = h.shape
    aw = g_attn.shape[0]
    cw = g_conv.shape[0]
    kvw = N_KV_HEADS * HEAD_DIM
    n_q_heads = aw // HEAD_DIM
    d_ff = w_gate.shape[1]
    in_width = w_in.shape[1]
    assert in_width == aw + 2 * kvw + 3 * cw and d == aw + cw
    assert seq % TM_PROJ == 0 and seq % TM_ATTN == 0 and TM_ATTN % SUB_ATTN == 0 and SUB_ATTN % WINDOW == 0
    assert m % TM_FFN == 0 and d_ff % TF_FFN == 0

    invf, m_lo, m_hi = _rope_lane_tables()
    row = lambda a: a.reshape(1, -1).astype(F32)
    cparams = functools.partial(pltpu.CompilerParams, vmem_limit_bytes=VMEM_LIMIT_BYTES)
    w_in_b = w_in.astype(BF16)

    tm = TM_PROJ
    tf = TF_FFN
    steps = m // tm
    assert d % (steps * BF16_ROWS) == 0
    slab = d // steps
    const = lambda i: (0, 0)
    by_row = lambda i: (i, 0)
    q, ka, kb, vt, co, w_gu, w_out_b = pl.pallas_call(
        functools.partial(_proj_body, dims=(aw, kvw, cw, HEAD_DIM ** -0.5), tiles_per_seq=seq // tm, tf=tf),
        grid=(steps,),
        in_specs=[
            pl.BlockSpec((tm, d), by_row),
            pl.BlockSpec((tm, 1), by_row),
            _resident((1, LANES), const),
            _resident((1, LANES), const),
            _resident((1, LANES), const),
            _resident((d, in_width), const),
            _resident((CONV_WIDTH, cw), const),
            _resident((1, cw), const),
            pl.BlockSpec((slab, d_ff), by_row),
            pl.BlockSpec((slab, d_ff), by_row),
            pl.BlockSpec((slab, d), by_row),
        ],
        out_specs=[
            pl.BlockSpec((tm, aw), by_row),
            pl.BlockSpec((tm, 2 * kvw), by_row),
            pl.BlockSpec((tm, 2 * kvw), by_row),
            pl.BlockSpec((kvw, tm), lambda i: (0, i)),
            pl.BlockSpec((tm, cw), by_row),
            pl.BlockSpec((slab, 2 * d_ff), by_row),
            pl.BlockSpec((slab, d), by_row),
        ],
        out_shape=[
            jax.ShapeDtypeStruct((m, aw), BF16),
            jax.ShapeDtypeStruct((m, 2 * kvw), BF16),
            jax.ShapeDtypeStruct((m, 2 * kvw), BF16),
            jax.ShapeDtypeStruct((kvw, m), BF16),
            jax.ShapeDtypeStruct((m, cw), BF16),
            jax.ShapeDtypeStruct((d, 2 * d_ff), BF16),
            jax.ShapeDtypeStruct((d, d), BF16),
        ],
        scratch_shapes=[pltpu.VMEM((tm + 2 * SUBLANES, cw), F32)],
        compiler_params=cparams(dimension_semantics=("arbitrary",)),
        name="proj_rope_conv",
    )(h, pos_col, invf, m_lo, m_hi, w_in_b, conv_w.astype(F32), row(g_conv), w_gate, w_up, w_out)

    tm = TM_ATTN
    wpt = tm // WINDOW
    cur = lambda i: (i, 0)
    prev = lambda i: (jnp.maximum(i * wpt - 1, 0), 0)
    cur_t = lambda i: (0, i)
    prev_t = lambda i: (0, jnp.maximum(i * wpt - 1, 0))
    g_attn_cols = jnp.broadcast_to(g_attn.astype(F32)[:, None], (aw, LANES))
    steps = m // tm
    assert d_ff % (steps * BF16_ROWS) == 0
    slab = d_ff // steps
    h1, w_down_b = pl.pallas_call(
        functools.partial(_attn_body, n_q_heads=n_q_heads, alpha=alpha, tiles_per_seq=seq // tm),
        grid=(steps,),
        in_specs=[
            pl.BlockSpec(memory_space=pltpu.SMEM),
            pl.BlockSpec((tm, aw), cur),
            pl.BlockSpec((tm, 2 * kvw), cur),
            pl.BlockSpec((WINDOW, 2 * kvw), prev),
            pl.BlockSpec((tm, 2 * kvw), cur),
            pl.BlockSpec((WINDOW, 2 * kvw), prev),
            pl.BlockSpec((kvw, tm), cur_t),
            pl.BlockSpec((kvw, WINDOW), prev_t),
            pl.BlockSpec((tm, cw), cur),
            pl.BlockSpec((tm, d), cur),
            _resident((aw, LANES), const),
            _resident((d, d), const),
            _resident((1, d), const),
            _resident((1, d), const),
            pl.BlockSpec((slab, d), cur),
        ],
        out_specs=[pl.BlockSpec((tm, d), cur), pl.BlockSpec((slab, d), cur)],
        out_shape=[jax.ShapeDtypeStruct((m, d), F32), jax.ShapeDtypeStruct((d_ff, d), BF16)],
        scratch_shapes=[pltpu.VMEM((aw, tm), F32)],
        compiler_params=cparams(dimension_semantics=("arbitrary",)),
        name="swa_outproj_ln",
    )(sinks.astype(F32), q, ka, ka, kb, kb, vt, vt, co, h, g_attn_cols, w_out_b,
      row(ln1_g), row(ln1_b), w_down)

    tm, tf = TM_FFN, TF_FFN
    n_i, n_j = m // tm, d_ff // tf
    n_pairs = n_j // 2
    n_steps = n_pairs + n_j % 2
    step = lambda i, s: jnp.where(i % 2 == 0, s, n_steps - 1 - s)
    tile_a = lambda i, s: 2 * step(i, s)
    tile_b = lambda i, s: jnp.minimum(2 * step(i, s) + 1, 2 * n_pairs - 1)
    out = pl.pallas_call(
        functools.partial(_ffn_body, alpha=alpha, n_pairs=n_pairs),
        grid=(n_i, n_steps),
        in_specs=[
            pl.BlockSpec((tm, d), lambda i, s: (i, 0)),
            pl.BlockSpec((d, 2 * tf), lambda i, s: (0, tile_a(i, s))),
            pl.BlockSpec((d, 2 * tf), lambda i, s: (0, tile_b(i, s))),
            pl.BlockSpec((tf, d), lambda i, s: (tile_a(i, s), 0)),
            pl.BlockSpec((tf, d), lambda i, s: (tile_b(i, s), 0)),
            _resident((1, d), lambda i, s: (0, 0)),
            _resident((1, d), lambda i, s: (0, 0)),
        ],
        out_specs=pl.BlockSpec((tm, d), lambda i, s: (i, 0)),
        out_shape=jax.ShapeDtypeStruct((m, d), F32),
        scratch_shapes=[pltpu.VMEM((tm, d), BF16), pltpu.VMEM((tm, d), F32)],
        compiler_params=cparams(dimension_semantics=("arbitrary", "arbitrary")),
        name="swiglu_ffn_ln",
    )(h1, w_gu, w_gu, w_down_b, w_down_b, row(ln2_g), row(ln2_b))
    return out


def kernel(x, positions, w_in, conv_w, sinks, g_attn, g_conv, w_out, ln1_g, ln1_b, w_gate, w_up, w_down, ln2_g, ln2_b):
    b, s, d = x.shape
    depth = w_in.shape[0]
    alpha = (2 * depth) ** 0.25
    h = x.reshape(b * s, d)
    pos_col = positions.reshape(b * s, 1)
    for l in range(depth):
        h = _layer(h, pos_col, w_in[l], conv_w[l], sinks[l], g_attn[l], g_conv[l], w_out[l],
                   ln1_g[l], ln1_b[l], w_gate[l], w_up[l], w_down[l], ln2_g[l], ln2_b[l],
                   seq=s, alpha=alpha)
    return h.reshape(b, s, d)
```

```python
import functools

import jax
import jax.numpy as jnp
from jax import lax
from jax.experimental import pallas as pl
from jax.experimental.pallas import tpu as pltpu

F32 = jnp.float32
BF16 = jnp.bfloat16

HEAD_DIM = 64
N_KV_HEADS = 4
WINDOW = 128
ROT_DIM = HEAD_DIM // 4
ROPE_THETA = 500000.0
CONV_WIDTH = 3
LN_EPS = 1e-5
RMS_EPS = 1e-6

LANES = 128
SUBLANES = 8
BF16_ROWS = 16
VMEM_LIMIT_BYTES = 60000 * 1024

TM_PROJ = 512
TM_ATTN = 512
SUB_ATTN = 256
TM_FFN = 512
TF_FFN = 512

_TN = (((0,), (0,)), ((), ()))


def _resident(block_shape, index_map):
    return pl.BlockSpec(block_shape, index_map, pipeline_mode=pl.Buffered(1))


def _layer_norm(x, g, b):
    mu = jnp.mean(x, axis=-1, keepdims=True)
    xc = x - mu
    var = jnp.mean(xc * xc, axis=-1, keepdims=True)
    return xc * lax.rsqrt(var + LN_EPS) * g + b


def _rms_norm(x, g):
    ms = jnp.mean(x * x, axis=-1, keepdims=True)
    return x * lax.rsqrt(ms + RMS_EPS) * g


def _proj_body(x_ref, pos_ref, invf_ref, w_ref, cw_ref, gc_ref,
               wg_ref, wu_ref, wo_ref,
               q_ref, ka_ref, kb_ref, vt_ref, co_ref, wgu_ref, wob_ref, zbuf_ref, *, dims, tiles_per_seq, tf):
    aw, kvw, cw, attn_scale = dims
    tm = x_ref.shape[0]
    i = pl.program_id(0)

    @pl.when(i % tiles_per_seq == 0)
    def _():
        zbuf_ref[0:SUBLANES, :] = jnp.zeros((SUBLANES, cw), F32)

    for j in range(wg_ref.shape[1] // tf):
        src = slice(j * tf, (j + 1) * tf)
        wgu_ref[:, 2 * j * tf:(2 * j + 1) * tf] = wg_ref[:, src].astype(BF16)
        wgu_ref[:, (2 * j + 1) * tf:(2 * j + 2) * tf] = wu_ref[:, src].astype(BF16)
    wob_ref[...] = wo_ref[...].astype(BF16)

    xb = x_ref[...].astype(BF16)

    ang = invf_ref[...] * pos_ref[...].astype(F32)
    cos_t = jnp.cos(ang)
    sin_t = jnp.sin(ang)

    def proj(lo, width):
        return jnp.dot(xb, w_ref[:, lo:lo + width], preferred_element_type=F32)

    c0 = aw + 2 * kvw
    z = proj(c0, cw) * proj(c0 + 2 * cw, cw)
    zbuf_ref[SUBLANES:SUBLANES + tm, :] = z
    z1 = zbuf_ref[SUBLANES - 1:SUBLANES - 1 + tm, :]
    z2 = zbuf_ref[SUBLANES - 2:SUBLANES - 2 + tm, :]
    conv = cw_ref[0:1, :] * z2 + cw_ref[1:2, :] * z1 + cw_ref[2:3, :] * z
    y = proj(c0 + cw, cw) * conv
    co_ref[...] = _rms_norm(y, gc_ref[...]).astype(BF16)
    zbuf_ref[0:SUBLANES, :] = zbuf_ref[tm:tm + SUBLANES, :]

    half = ROT_DIM // 2

    def rope_t(tt, head):
        b0 = head * HEAD_DIM
        t1 = tt[b0:b0 + half, :]
        t2 = tt[b0 + half:b0 + ROT_DIM, :]
        return jnp.concatenate([t1 * cos_t - t2 * sin_t, t2 * cos_t + t1 * sin_t,
                                tt[b0 + ROT_DIM:b0 + HEAD_DIM, :]], axis=0)

    kt = proj(aw, kvw).T
    k = jnp.concatenate([rope_t(kt, hd) for hd in range(kvw // HEAD_DIM)], axis=0).T
    low = lax.broadcasted_iota(jnp.int32, (tm, LANES), 1) < HEAD_DIM
    for c in range(kvw // LANES):
        kr = k[:, c * LANES:(c + 1) * LANES]
        sw = pltpu.roll(kr, HEAD_DIM, 1)
        even = slice(2 * c * LANES, (2 * c + 1) * LANES)
        odd = slice((2 * c + 1) * LANES, (2 * c + 2) * LANES)
        ka_ref[:, even] = jnp.where(low, kr, 0.0).astype(BF16)
        ka_ref[:, odd] = jnp.where(low, sw, 0.0).astype(BF16)
        kb_ref[:, even] = jnp.where(low, 0.0, sw).astype(BF16)
        kb_ref[:, odd] = jnp.where(low, 0.0, kr).astype(BF16)

    vt_ref[...] = proj(aw + kvw, kvw).T.astype(BF16)

    qt = proj(0, aw).T
    for hd in range(aw // HEAD_DIM):
        q_ref[hd * HEAD_DIM:(hd + 1) * HEAD_DIM, :] = (rope_t(qt, hd) * attn_scale).astype(BF16)


def _attn_body(sinks_ref, q_ref, kac_ref, kap_ref, kbc_ref, kbp_ref, vtc_ref, vtp_ref, co_ref, x_ref,
               gat_ref, wo_ref, g1_ref, b1_ref, wd_ref, h_ref, wdb_ref, at_ref,
               *, n_q_heads, alpha, tiles_per_seq):
    aw, tm = q_ref.shape
    wdb_ref[...] = wd_ref[...].astype(BF16)
    W = WINDOW
    group = n_q_heads // N_KV_HEADS
    assert group == 4 and HEAD_DIM * 2 == LANES
    i = pl.program_id(0)
    prev_bias = jnp.where(i % tiles_per_seq == 0, -jnp.inf, 0.0)

    kj = lax.broadcasted_iota(jnp.int32, (W, 2 * W), 0)
    qi = lax.broadcasted_iota(jnp.int32, (W, 2 * W), 1) % W
    use_cur = kj <= qi
    cur_mask = jnp.where(use_cur, 1.0, 0.0).astype(BF16)
    prev_mask = jnp.where(use_cur, 0.0, 1.0).astype(BF16)
    ones_rows = jnp.ones((BF16_ROWS, 2 * W), BF16)

    halves = ((kac_ref, kap_ref), (kbc_ref, kbp_ref))

    def window_scores(w):
        rows = slice(w * W, (w + 1) * W)
        scores = []
        for g in range(N_KV_HEADS):
            qg = jnp.concatenate([q_ref[(2 * g) * LANES:(2 * g + 1) * LANES, rows],
                                  q_ref[(2 * g + 1) * LANES:(2 * g + 2) * LANES, rows]], axis=1)
            blk = slice(g * LANES, (g + 1) * LANES)
            for kc_ref, kp_ref in halves:
                if w == 0:
                    kk = jnp.concatenate([kp_ref[:, blk], kc_ref[0:W, blk]], axis=0)
                else:
                    kk = kc_ref[(w - 1) * W:(w + 1) * W, blk]
                scores.append(jnp.dot(kk, qg, preferred_element_type=F32))
        return scores

    def window_values(w, scores):
        rows = slice(w * W, (w + 1) * W)
        probs = []
        for n, s in enumerate(scores):
            g, half = divmod(n, 2)
            s_prev = s[0:W, :]
            if w == 0:
                s_prev = s_prev + prev_bias
            merged = jnp.where(use_cur, s[W:2 * W, :], s_prev)
            heads = (group * g + half, group * g + 2 + half)
            sink = jnp.concatenate([jnp.full((1, W), sinks_ref[heads[0]], F32),
                                    jnp.full((1, W), sinks_ref[heads[1]], F32)], axis=1)
            m = jnp.maximum(jnp.max(merged, axis=0, keepdims=True), sink)
            e = jnp.exp(merged - m).astype(BF16)
            ek = jnp.concatenate([e * prev_mask, e * cur_mask], axis=0)
            probs.append((heads, ek, jnp.exp(sink - m)))
        for n, (heads, ek, sink_e) in enumerate(probs):
            g = n // 2
            hrows = slice(g * HEAD_DIM, (g + 1) * HEAD_DIM)
            if w == 0:
                vt = jnp.concatenate([vtp_ref[hrows, :], vtc_ref[hrows, 0:W]], axis=1)
            else:
                vt = vtc_ref[hrows, (w - 1) * W:(w + 1) * W]
            vt_aug = jnp.concatenate([vt, ones_rows], axis=0)
            o = jnp.dot(vt_aug, ek, preferred_element_type=F32)
            denom = o[HEAD_DIM:HEAD_DIM + 1, :] + sink_e
            on = o[0:HEAD_DIM, :] * (1.0 / denom)
            for t in range(2):
                hd = heads[t]
                at_ref[hd * HEAD_DIM:(hd + 1) * HEAD_DIM, rows] = on[:, t * W:(t + 1) * W]

    def out_proj(rows):
        at = at_ref[:, rows]
        ms = jnp.mean(at * at, axis=0, keepdims=True)
        gat = jnp.concatenate([gat_ref[...]] * ((rows.stop - rows.start) // LANES), axis=1)
        atn = (at * lax.rsqrt(ms + RMS_EPS) * gat).astype(BF16)
        mix = lax.dot_general(atn, wo_ref[0:aw, :], _TN, preferred_element_type=F32)
        return mix + jnp.dot(co_ref[rows, :], wo_ref[aw:, :], preferred_element_type=F32)

    def finish(rows, mix):
        h_ref[rows, :] = _layer_norm(alpha * x_ref[rows, :] + mix, g1_ref[...], b1_ref[...])

    wpc = SUB_ATTN // W
    n_chunks = tm // SUB_ATTN
    for w in range(wpc):
        window_values(w, window_scores(w))
    pending = None
    for c in range(n_chunks):
        rows = slice(c * SUB_ATTN, (c + 1) * SUB_ATTN)
        nxt = [(w, window_scores(w)) for w in range((c + 1) * wpc, (c + 2) * wpc)] if c + 1 < n_chunks else []
        mix = out_proj(rows)
        if pending is not None:
            finish(*pending)
        for w, scores in nxt:
            window_values(w, scores)
        pending = (rows, mix)
    finish(*pending)


def _ffn_body(h_ref, wgu_a_ref, wgu_b_ref, wd_a_ref, wd_b_ref, g2_ref, b2_ref, o_ref, hb_ref, acc_ref,
              *, alpha, n_pairs):
    i = pl.program_id(0)
    s = pl.program_id(1)
    n_steps = pl.num_programs(1)
    s_eff = jnp.where(i % 2 == 0, s, n_steps - 1 - s)
    tf = wd_a_ref.shape[0]

    @pl.when(s == 0)
    def _():
        h = h_ref[...]
        hb_ref[...] = h.astype(BF16)
        acc_ref[...] = alpha * h

    def gate_up(w_ref):
        return jnp.dot(hb_ref[...], w_ref[...], preferred_element_type=F32)

    def activation(gu):
        gate = gu[:, 0:tf]
        return (gate * (1.0 / (1.0 + jnp.exp(-gate))) * gu[:, tf:2 * tf]).astype(BF16)

    @pl.when(s_eff < n_pairs)
    def _():
        gu_a = gate_up(wgu_a_ref)
        gu_b = gate_up(wgu_b_ref)
        acc_ref[...] += jnp.dot(activation(gu_a), wd_a_ref[...], preferred_element_type=F32)
        acc_ref[...] += jnp.dot(activation(gu_b), wd_b_ref[...], preferred_element_type=F32)

    @pl.when(s_eff >= n_pairs)
    def _():
        acc_ref[...] += jnp.dot(activation(gate_up(wgu_a_ref)), wd_a_ref[...], preferred_element_type=F32)

    @pl.when(s == n_steps - 1)
    def _():
        o_ref[...] = _layer_norm(acc_ref[...], g2_ref[...], b2_ref[...])


def _rope_inv_freq():
    inv_freq = ROPE_THETA ** (-jnp.arange(0, ROT_DIM, 2, dtype=F32) / ROT_DIM)
    return inv_freq.reshape(ROT_DIM // 2, 1)


def _layer(h, pos_row, w_in, conv_w, sinks, g_attn, g_conv, w_out, ln1_g, ln1_b,
           w_gate, w_up, w_down, ln2_g, ln2_b, *, seq, alpha):
    m, d = h.shape
    aw = g_attn.shape[0]
    cw = g_conv.shape[0]
    kvw = N_KV_HEADS * HEAD_DIM
    n_q_heads = aw // HEAD_DIM
    d_ff = w_gate.shape[1]
    in_width = w_in.shape[1]
    assert in_width == aw + 2 * kvw + 3 * cw and d == aw + cw
    assert seq % TM_PROJ == 0 and seq % TM_ATTN == 0 and TM_ATTN % SUB_ATTN == 0 and SUB_ATTN % WINDOW == 0
    assert m % TM_FFN == 0 and d_ff % TF_FFN == 0

    invf = _rope_inv_freq()
    row = lambda a: a.reshape(1, -1).astype(F32)
    cparams = functools.partial(pltpu.CompilerParams, vmem_limit_bytes=VMEM_LIMIT_BYTES)
    w_in_b = w_in.astype(BF16)

    tm = TM_PROJ
    tf = TF_FFN
    steps = m // tm
    assert d % (steps * BF16_ROWS) == 0
    slab = d // steps
    const = lambda i: (0, 0)
    by_row = lambda i: (i, 0)
    q, ka, kb, vt, co, w_gu, w_out_b = pl.pallas_call(
        functools.partial(_proj_body, dims=(aw, kvw, cw, HEAD_DIM ** -0.5), tiles_per_seq=seq // tm, tf=tf),
        grid=(steps,),
        in_specs=[
            pl.BlockSpec((tm, d), by_row),
            pl.BlockSpec((1, tm), lambda i: (0, i)),
            _resident((ROT_DIM // 2, 1), const),
            _resident((d, in_width), const),
            _resident((CONV_WIDTH, cw), const),
            _resident((1, cw), const),
            pl.BlockSpec((slab, d_ff), by_row),
            pl.BlockSpec((slab, d_ff), by_row),
            pl.BlockSpec((slab, d), by_row),
        ],
        out_specs=[
            pl.BlockSpec((aw, tm), lambda i: (0, i)),
            pl.BlockSpec((tm, 2 * kvw), by_row),
            pl.BlockSpec((tm, 2 * kvw), by_row),
            pl.BlockSpec((kvw, tm), lambda i: (0, i)),
            pl.BlockSpec((tm, cw), by_row),
            pl.BlockSpec((slab, 2 * d_ff), by_row),
            pl.BlockSpec((slab, d), by_row),
        ],
        out_shape=[
            jax.ShapeDtypeStruct((aw, m), BF16),
            jax.ShapeDtypeStruct((m, 2 * kvw), BF16),
            jax.ShapeDtypeStruct((m, 2 * kvw), BF16),
            jax.ShapeDtypeStruct((kvw, m), BF16),
            jax.ShapeDtypeStruct((m, cw), BF16),
            jax.ShapeDtypeStruct((d, 2 * d_ff), BF16),
            jax.ShapeDtypeStruct((d, d), BF16),
        ],
        scratch_shapes=[pltpu.VMEM((tm + 2 * SUBLANES, cw), F32)],
        compiler_params=cparams(dimension_semantics=("arbitrary",)),
        name="proj_rope_conv",
    )(h, pos_row, invf, w_in_b, conv_w.astype(F32), row(g_conv), w_gate, w_up, w_out)

    tm = TM_ATTN
    wpt = tm // WINDOW
    cur = lambda i: (i, 0)
    prev = lambda i: (jnp.maximum(i * wpt - 1, 0), 0)
    cur_t = lambda i: (0, i)
    prev_t = lambda i: (0, jnp.maximum(i * wpt - 1, 0))
    g_attn_cols = jnp.broadcast_to(g_attn.astype(F32)[:, None], (aw, LANES))
    steps = m // tm
    assert d_ff % (steps * BF16_ROWS) == 0
    slab = d_ff // steps
    h1, w_down_b = pl.pallas_call(
        functools.partial(_attn_body, n_q_heads=n_q_heads, alpha=alpha, tiles_per_seq=seq // tm),
        grid=(steps,),
        in_specs=[
            pl.BlockSpec(memory_space=pltpu.SMEM),
            pl.BlockSpec((aw, tm), cur_t),
            pl.BlockSpec((tm, 2 * kvw), cur),
            pl.BlockSpec((WINDOW, 2 * kvw), prev),
            pl.BlockSpec((tm, 2 * kvw), cur),
            pl.BlockSpec((WINDOW, 2 * kvw), prev),
            pl.BlockSpec((kvw, tm), cur_t),
            pl.BlockSpec((kvw, WINDOW), prev_t),
            pl.BlockSpec((tm, cw), cur),
            pl.BlockSpec((tm, d), cur),
            _resident((aw, LANES), const),
            _resident((d, d), const),
            _resident((1, d), const),
            _resident((1, d), const),
            pl.BlockSpec((slab, d), cur),
        ],
        out_specs=[pl.BlockSpec((tm, d), cur), pl.BlockSpec((slab, d), cur)],
        out_shape=[jax.ShapeDtypeStruct((m, d), F32), jax.ShapeDtypeStruct((d_ff, d), BF16)],
        scratch_shapes=[pltpu.VMEM((aw, tm), F32)],
        compiler_params=cparams(dimension_semantics=("arbitrary",)),
        name="swa_outproj_ln",
    )(sinks.astype(F32), q, ka, ka, kb, kb, vt, vt, co, h, g_attn_cols, w_out_b,
      row(ln1_g), row(ln1_b), w_down)

    tm, tf = TM_FFN, TF_FFN
    n_i, n_j = m // tm, d_ff // tf
    n_pairs = n_j // 2
    n_steps = n_pairs + n_j % 2
    step = lambda i, s: jnp.where(i % 2 == 0, s, n_steps - 1 - s)
    tile_a = lambda i, s: 2 * step(i, s)
    tile_b = lambda i, s: jnp.minimum(2 * step(i, s) + 1, 2 * n_pairs - 1)
    out = pl.pallas_call(
        functools.partial(_ffn_body, alpha=alpha, n_pairs=n_pairs),
        grid=(n_i, n_steps),
        in_specs=[
            pl.BlockSpec((tm, d), lambda i, s: (i, 0)),
            pl.BlockSpec((d, 2 * tf), lambda i, s: (0, tile_a(i, s))),
            pl.BlockSpec((d, 2 * tf), lambda i, s: (0, tile_b(i, s))),
            pl.BlockSpec((tf, d), lambda i, s: (tile_a(i, s), 0)),
            pl.BlockSpec((tf, d), lambda i, s: (tile_b(i, s), 0)),
            _resident((1, d), lambda i, s: (0, 0)),
            _resident((1, d), lambda i, s: (0, 0)),
        ],
        out_specs=pl.BlockSpec((tm, d), lambda i, s: (i, 0)),
        out_shape=jax.ShapeDtypeStruct((m, d), F32),
        scratch_shapes=[pltpu.VMEM((tm, d), BF16), pltpu.VMEM((tm, d), F32)],
        compiler_params=cparams(dimension_semantics=("arbitrary", "arbitrary")),
        name="swiglu_ffn_ln",
    )(h1, w_gu, w_gu, w_down_b, w_down_b, row(ln2_g), row(ln2_b))
    return out


def kernel(x, positions, w_in, conv_w, sinks, g_attn, g_conv, w_out, ln1_g, ln1_b, w_gate, w_up, w_down, ln2_g, ln2_b):
    b, s, d = x.shape
    depth = w_in.shape[0]
    alpha = (2 * depth) ** 0.25
    h = x.reshape(b * s, d)
    pos_row = positions.reshape(1, b * s)
    for l in range(depth):
        h = _layer(h, pos_row, w_in[l], conv_w[l], sinks[l], g_attn[l], g_conv[l], w_out[l],
                   ln1_g[l], ln1_b[l], w_gate[l], w_up[l], w_down[l], ln2_g[l], ln2_b[l],
                   seq=s, alpha=alpha)
    return h.reshape(b, s, d)
```

```python
import functools

import jax
import jax.numpy as jnp
from jax import lax
from jax.experimental import pallas as pl
from jax.experimental.pallas import tpu as pltpu

F32 = jnp.float32
BF16 = jnp.bfloat16

HEAD_DIM = 64
N_KV_HEADS = 4
WINDOW = 128
ROT_DIM = HEAD_DIM // 4
ROPE_THETA = 500000.0
CONV_WIDTH = 3
LN_EPS = 1e-5
RMS_EPS = 1e-6

LANES = 128
SUBLANES = 8
BF16_ROWS = 16
VMEM_LIMIT_BYTES = 60000 * 1024

TM_PROJ = 512
TM_ATTN = 512
SUB_ATTN = 256
TM_FFN = 512
TF_FFN = 512

_TN = (((0,), (0,)), ((), ()))


def _resident(block_shape, index_map):
    return pl.BlockSpec(block_shape, index_map, pipeline_mode=pl.Buffered(1))


def _layer_norm(x, g, b):
    mu = jnp.mean(x, axis=-1, keepdims=True)
    xc = x - mu
    var = jnp.mean(xc * xc, axis=-1, keepdims=True)
    return xc * lax.rsqrt(var + LN_EPS) * g + b


def _rms_norm(x, g):
    ms = jnp.mean(x * x, axis=-1, keepdims=True)
    return x * lax.rsqrt(ms + RMS_EPS) * g


def _proj_body(x_ref, pos_ref, invf_ref, w_ref, cw_ref, gc_ref,
               wg_ref, wu_ref, wo_ref,
               q_ref, k2_ref, vt_ref, co_ref, wgu_ref, wob_ref, zbuf_ref, *, dims, tiles_per_seq, tf):
    aw, kvw, cw, attn_scale = dims
    tm = x_ref.shape[0]
    i = pl.program_id(0)

    @pl.when(i % tiles_per_seq == 0)
    def _():
        zbuf_ref[0:SUBLANES, :] = jnp.zeros((SUBLANES, cw), F32)

    for j in range(wg_ref.shape[1] // tf):
        src = slice(j * tf, (j + 1) * tf)
        wgu_ref[:, 2 * j * tf:(2 * j + 1) * tf] = wg_ref[:, src].astype(BF16)
        wgu_ref[:, (2 * j + 1) * tf:(2 * j + 2) * tf] = wu_ref[:, src].astype(BF16)
    wob_ref[...] = wo_ref[...].astype(BF16)

    xb = x_ref[...].astype(BF16)

    ang = invf_ref[...] * pos_ref[...].astype(F32)
    cos_t = jnp.cos(ang)
    sin_t = jnp.sin(ang)

    def proj(lo, width):
        return jnp.dot(xb, w_ref[:, lo:lo + width], preferred_element_type=F32)

    c0 = aw + 2 * kvw
    z = proj(c0, cw) * proj(c0 + 2 * cw, cw)
    zbuf_ref[SUBLANES:SUBLANES + tm, :] = z
    z1 = zbuf_ref[SUBLANES - 1:SUBLANES - 1 + tm, :]
    z2 = zbuf_ref[SUBLANES - 2:SUBLANES - 2 + tm, :]
    conv = cw_ref[0:1, :] * z2 + cw_ref[1:2, :] * z1 + cw_ref[2:3, :] * z
    y = proj(c0 + cw, cw) * conv
    co_ref[...] = _rms_norm(y, gc_ref[...]).astype(BF16)
    zbuf_ref[0:SUBLANES, :] = zbuf_ref[tm:tm + SUBLANES, :]

    half = ROT_DIM // 2

    def rope_t(tt, head):
        b0 = head * HEAD_DIM
        t1 = tt[b0:b0 + half, :]
        t2 = tt[b0 + half:b0 + ROT_DIM, :]
        return jnp.concatenate([t1 * cos_t - t2 * sin_t, t2 * cos_t + t1 * sin_t,
                                tt[b0 + ROT_DIM:b0 + HEAD_DIM, :]], axis=0)

    kt = proj(aw, kvw).T
    k = jnp.concatenate([rope_t(kt, hd) for hd in range(kvw // HEAD_DIM)], axis=0).T
    low = lax.broadcasted_iota(jnp.int32, (tm, LANES), 1) < HEAD_DIM
    for c in range(kvw // LANES):
        kr = k[:, c * LANES:(c + 1) * LANES]
        sw = pltpu.roll(kr, HEAD_DIM, 1)
        even = slice(2 * c * LANES, (2 * c + 1) * LANES)
        odd = slice((2 * c + 1) * LANES, (2 * c + 2) * LANES)
        second = lambda blk: slice(2 * kvw + blk.start, 2 * kvw + blk.stop)
        k2_ref[:, even] = jnp.where(low, kr, 0.0).astype(BF16)
        k2_ref[:, odd] = jnp.where(low, sw, 0.0).astype(BF16)
        k2_ref[:, second(even)] = jnp.where(low, 0.0, sw).astype(BF16)
        k2_ref[:, second(odd)] = jnp.where(low, 0.0, kr).astype(BF16)

    qt = proj(0, aw).T
    for hd in range(aw // HEAD_DIM):
        q_ref[hd * HEAD_DIM:(hd + 1) * HEAD_DIM, :] = (rope_t(qt, hd) * attn_scale).astype(BF16)

    vt_ref[...] = proj(aw + kvw, kvw).T.astype(BF16)


def _attn_body(sinks_ref, q_ref, kc_ref, kp_ref, vtc_ref, vtp_ref, co_ref, x_ref,
               gat_ref, wo_ref, g1_ref, b1_ref, wd_ref, h_ref, wdb_ref, at_ref,
               *, n_q_heads, alpha, tiles_per_seq):
    aw, tm = q_ref.shape
    wdb_ref[...] = wd_ref[...].astype(BF16)
    W = WINDOW
    group = n_q_heads // N_KV_HEADS
    assert group == 4 and HEAD_DIM * 2 == LANES
    i = pl.program_id(0)
    prev_bias = jnp.where(i % tiles_per_seq == 0, -jnp.inf, 0.0)

    kj = lax.broadcasted_iota(jnp.int32, (W, 2 * W), 0)
    qi = lax.broadcasted_iota(jnp.int32, (W, 2 * W), 1) % W
    use_cur = kj <= qi
    cur_mask = jnp.where(use_cur, 1.0, 0.0).astype(BF16)
    prev_mask = jnp.where(use_cur, 0.0, 1.0).astype(BF16)
    ones_rows = jnp.ones((BF16_ROWS, 2 * W), BF16)

    copy_lanes = kc_ref.shape[1] // 2

    def window_scores(w):
        rows = slice(w * W, (w + 1) * W)
        scores = []
        for g in range(N_KV_HEADS):
            qg = jnp.concatenate([q_ref[(2 * g) * LANES:(2 * g + 1) * LANES, rows],
                                  q_ref[(2 * g + 1) * LANES:(2 * g + 2) * LANES, rows]], axis=1)
            for off in (0, copy_lanes):
                blk = slice(off + g * LANES, off + (g + 1) * LANES)
                if w == 0:
                    kk = jnp.concatenate([kp_ref[:, blk], kc_ref[0:W, blk]], axis=0)
                else:
                    kk = kc_ref[(w - 1) * W:(w + 1) * W, blk]
                scores.append(jnp.dot(kk, qg, preferred_element_type=F32))
        return scores

    def window_values(w, scores):
        rows = slice(w * W, (w + 1) * W)
        probs = []
        for n, s in enumerate(scores):
            g, half = divmod(n, 2)
            s_prev = s[0:W, :]
            if w == 0:
                s_prev = s_prev + prev_bias
            merged = jnp.where(use_cur, s[W:2 * W, :], s_prev)
            heads = (group * g + half, group * g + 2 + half)
            sink = jnp.concatenate([jnp.full((1, W), sinks_ref[heads[0]], F32),
                                    jnp.full((1, W), sinks_ref[heads[1]], F32)], axis=1)
            m = jnp.maximum(jnp.max(merged, axis=0, keepdims=True), sink)
            e = jnp.exp(merged - m).astype(BF16)
            ek = jnp.concatenate([e * prev_mask, e * cur_mask], axis=0)
            probs.append((heads, ek, jnp.exp(sink - m)))
        for n, (heads, ek, sink_e) in enumerate(probs):
            g = n // 2
            hrows = slice(g * HEAD_DIM, (g + 1) * HEAD_DIM)
            if w == 0:
                vt = jnp.concatenate([vtp_ref[hrows, :], vtc_ref[hrows, 0:W]], axis=1)
            else:
                vt = vtc_ref[hrows, (w - 1) * W:(w + 1) * W]
            vt_aug = jnp.concatenate([vt, ones_rows], axis=0)
            o = jnp.dot(vt_aug, ek, preferred_element_type=F32)
            denom = o[HEAD_DIM:HEAD_DIM + 1, :] + sink_e
            on = o[0:HEAD_DIM, :] * (1.0 / denom)
            for t in range(2):
                hd = heads[t]
                at_ref[hd * HEAD_DIM:(hd + 1) * HEAD_DIM, rows] = on[:, t * W:(t + 1) * W]

    def out_proj(rows):
        at = at_ref[:, rows]
        ms = jnp.mean(at * at, axis=0, keepdims=True)
        gat = jnp.concatenate([gat_ref[...]] * ((rows.stop - rows.start) // LANES), axis=1)
        atn = (at * lax.rsqrt(ms + RMS_EPS) * gat).astype(BF16)
        mix = lax.dot_general(atn, wo_ref[0:aw, :], _TN, preferred_element_type=F32)
        return mix + jnp.dot(co_ref[rows, :], wo_ref[aw:, :], preferred_element_type=F32)

    def finish(rows, mix):
        h_ref[rows, :] = _layer_norm(alpha * x_ref[rows, :] + mix, g1_ref[...], b1_ref[...])

    wpc = SUB_ATTN // W
    n_chunks = tm // SUB_ATTN
    for w in range(wpc):
        window_values(w, window_scores(w))
    pending = None
    for c in range(n_chunks):
        rows = slice(c * SUB_ATTN, (c + 1) * SUB_ATTN)
        nxt = [(w, window_scores(w)) for w in range((c + 1) * wpc, (c + 2) * wpc)] if c + 1 < n_chunks else []
        mix = out_proj(rows)
        if pending is not None:
            finish(*pending)
        for w, scores in nxt:
            window_values(w, scores)
        pending = (rows, mix)
    finish(*pending)


def _ffn_body(h_ref, wgu_a_ref, wgu_b_ref, wd_a_ref, wd_b_ref, g2_ref, b2_ref, o_ref, hb_ref, acc_ref,
              *, alpha, n_pairs):
    i = pl.program_id(0)
    s = pl.program_id(1)
    n_steps = pl.num_programs(1)
    s_eff = jnp.where(i % 2 == 0, s, n_steps - 1 - s)
    tf = wd_a_ref.shape[0]

    @pl.when(s == 0)
    def _():
        h = h_ref[...]
        hb_ref[...] = h.astype(BF16)
        acc_ref[...] = alpha * h

    def gate_up(w_ref):
        return jnp.dot(hb_ref[...], w_ref[...], preferred_element_type=F32)

    def activation(gu):
        gate = gu[:, 0:tf]
        return (gate * (1.0 / (1.0 + jnp.exp(-gate))) * gu[:, tf:2 * tf]).astype(BF16)

    @pl.when(s_eff < n_pairs)
    def _():
        gu_a = gate_up(wgu_a_ref)
        gu_b = gate_up(wgu_b_ref)
        acc_ref[...] += jnp.dot(activation(gu_a), wd_a_ref[...], preferred_element_type=F32)
        acc_ref[...] += jnp.dot(activation(gu_b), wd_b_ref[...], preferred_element_type=F32)

    @pl.when(s_eff >= n_pairs)
    def _():
        acc_ref[...] += jnp.dot(activation(gate_up(wgu_a_ref)), wd_a_ref[...], preferred_element_type=F32)

    @pl.when(s == n_steps - 1)
    def _():
        o_ref[...] = _layer_norm(acc_ref[...], g2_ref[...], b2_ref[...])


def _rope_inv_freq():
    inv_freq = ROPE_THETA ** (-jnp.arange(0, ROT_DIM, 2, dtype=F32) / ROT_DIM)
    return inv_freq.reshape(ROT_DIM // 2, 1)


def _layer(h, pos_row, w_in, conv_w, sinks, g_attn, g_conv, w_out, ln1_g, ln1_b,
           w_gate, w_up, w_down, ln2_g, ln2_b, *, seq, alpha):
    m, d = h.shape
    aw = g_attn.shape[0]
    cw = g_conv.shape[0]
    kvw = N_KV_HEADS * HEAD_DIM
    n_q_heads = aw // HEAD_DIM
    d_ff = w_gate.shape[1]
    in_width = w_in.shape[1]
    assert in_width == aw + 2 * kvw + 3 * cw and d == aw + cw
    assert seq % TM_PROJ == 0 and seq % TM_ATTN == 0 and TM_ATTN % SUB_ATTN == 0 and SUB_ATTN % WINDOW == 0
    assert m % TM_FFN == 0 and d_ff % TF_FFN == 0

    invf = _rope_inv_freq()
    row = lambda a: a.reshape(1, -1).astype(F32)
    cparams = functools.partial(pltpu.CompilerParams, vmem_limit_bytes=VMEM_LIMIT_BYTES)
    w_in_b = w_in.astype(BF16)

    tm = TM_PROJ
    tf = TF_FFN
    steps = m // tm
    assert d % (steps * BF16_ROWS) == 0
    slab = d // steps
    const = lambda i: (0, 0)
    by_row = lambda i: (i, 0)
    q, k2, vt, co, w_gu, w_out_b = pl.pallas_call(
        functools.partial(_proj_body, dims=(aw, kvw, cw, HEAD_DIM ** -0.5), tiles_per_seq=seq // tm, tf=tf),
        grid=(steps,),
        in_specs=[
            pl.BlockSpec((tm, d), by_row),
            pl.BlockSpec((1, tm), lambda i: (0, i)),
            _resident((ROT_DIM // 2, 1), const),
            _resident((d, in_width), const),
            _resident((CONV_WIDTH, cw), const),
            _resident((1, cw), const),
            pl.BlockSpec((slab, d_ff), by_row),
            pl.BlockSpec((slab, d_ff), by_row),
            pl.BlockSpec((slab, d), by_row),
        ],
        out_specs=[
            pl.BlockSpec((aw, tm), lambda i: (0, i)),
            pl.BlockSpec((tm, 4 * kvw), by_row),
            pl.BlockSpec((kvw, tm), lambda i: (0, i)),
            pl.BlockSpec((tm, cw), by_row),
            pl.BlockSpec((slab, 2 * d_ff), by_row),
            pl.BlockSpec((slab, d), by_row),
        ],
        out_shape=[
            jax.ShapeDtypeStruct((aw, m), BF16),
            jax.ShapeDtypeStruct((m, 4 * kvw), BF16),
            jax.ShapeDtypeStruct((kvw, m), BF16),
            jax.ShapeDtypeStruct((m, cw), BF16),
            jax.ShapeDtypeStruct((d, 2 * d_ff), BF16),
            jax.ShapeDtypeStruct((d, d), BF16),
        ],
        scratch_shapes=[pltpu.VMEM((tm + 2 * SUBLANES, cw), F32)],
        compiler_params=cparams(dimension_semantics=("arbitrary",)),
        name="proj_rope_conv",
    )(h, pos_row, invf, w_in_b, conv_w.astype(F32), row(g_conv), w_gate, w_up, w_out)

    tm = TM_ATTN
    wpt = tm // WINDOW
    cur = lambda i: (i, 0)
    prev = lambda i: (jnp.maximum(i * wpt - 1, 0), 0)
    cur_t = lambda i: (0, i)
    prev_t = lambda i: (0, jnp.maximum(i * wpt - 1, 0))
    g_attn_cols = jnp.broadcast_to(g_attn.astype(F32)[:, None], (aw, LANES))
    steps = m // tm
    assert d_ff % (steps * BF16_ROWS) == 0
    slab = d_ff // steps
    h1, w_down_b = pl.pallas_call(
        functools.partial(_attn_body, n_q_heads=n_q_heads, alpha=alpha, tiles_per_seq=seq // tm),
        grid=(steps,),
        in_specs=[
            pl.BlockSpec(memory_space=pltpu.SMEM),
            pl.BlockSpec((aw, tm), cur_t),
            pl.BlockSpec((tm, 4 * kvw), cur),
            pl.BlockSpec((WINDOW, 4 * kvw), prev),
            pl.BlockSpec((kvw, tm), cur_t),
            pl.BlockSpec((kvw, WINDOW), prev_t),
            pl.BlockSpec((tm, cw), cur),
            pl.BlockSpec((tm, d), cur),
            _resident((aw, LANES), const),
            _resident((d, d), const),
            _resident((1, d), const),
            _resident((1, d), const),
            pl.BlockSpec((slab, d), cur),
        ],
        out_specs=[pl.BlockSpec((tm, d), cur), pl.BlockSpec((slab, d), cur)],
        out_shape=[jax.ShapeDtypeStruct((m, d), F32), jax.ShapeDtypeStruct((d_ff, d), BF16)],
        scratch_shapes=[pltpu.VMEM((aw, tm), F32)],
        compiler_params=cparams(dimension_semantics=("arbitrary",)),
        name="swa_outproj_ln",
    )(sinks.astype(F32), q, k2, k2, vt, vt, co, h, g_attn_cols, w_out_b,
      row(ln1_g), row(ln1_b), w_down)

    tm, tf = TM_FFN, TF_FFN
    n_i, n_j = m // tm, d_ff // tf
    n_pairs = n_j // 2
    n_steps = n_pairs + n_j % 2
    step = lambda i, s: jnp.where(i % 2 == 0, s, n_steps - 1 - s)
    tile_a = lambda i, s: 2 * step(i, s)
    tile_b = lambda i, s: jnp.minimum(2 * step(i, s) + 1, 2 * n_pairs - 1)
    out = pl.pallas_call(
        functools.partial(_ffn_body, alpha=alpha, n_pairs=n_pairs),
        grid=(n_i, n_steps),
        in_specs=[
            pl.BlockSpec((tm, d), lambda i, s: (i, 0)),
            pl.BlockSpec((d, 2 * tf), lambda i, s: (0, tile_a(i, s))),
            pl.BlockSpec((d, 2 * tf), lambda i, s: (0, tile_b(i, s))),
            pl.BlockSpec((tf, d), lambda i, s: (tile_a(i, s), 0)),
            pl.BlockSpec((tf, d), lambda i, s: (tile_b(i, s), 0)),
            _resident((1, d), lambda i, s: (0, 0)),
            _resident((1, d), lambda i, s: (0, 0)),
        ],
        out_specs=pl.BlockSpec((tm, d), lambda i, s: (i, 0)),
        out_shape=jax.ShapeDtypeStruct((m, d), F32),
        scratch_shapes=[pltpu.VMEM((tm, d), BF16), pltpu.VMEM((tm, d), F32)],
        compiler_params=cparams(dimension_semantics=("arbitrary", "arbitrary")),
        name="swiglu_ffn_ln",
    )(h1, w_gu, w_gu, w_down_b, w_down_b, row(ln2_g), row(ln2_b))
    return out


def kernel(x, positions, w_in, conv_w, sinks, g_attn, g_conv, w_out, ln1_g, ln1_b, w_gate, w_up, w_down, ln2_g, ln2_b):
    b, s, d = x.shape
    depth = w_in.shape[0]
    alpha = (2 * depth) ** 0.25
    h = x.reshape(b * s, d)
    pos_row = positions.reshape(1, b * s)
    for l in range(depth):
        h = _layer(h, pos_row, w_in[l], conv_w[l], sinks[l], g_attn[l], g_conv[l], w_out[l],
                   ln1_g[l], ln1_b[l], w_gate[l], w_up[l], w_down[l], ln2_g[l], ln2_b[l],
                   seq=s, alpha=alpha)
    return h.reshape(b, s, d)
```

```python
import functools

import jax
import jax.numpy as jnp
from jax import lax
from jax.experimental import pallas as pl
from jax.experimental.pallas import tpu as pltpu

F32 = jnp.float32
BF16 = jnp.bfloat16

HEAD_DIM = 64
N_KV_HEADS = 4
WINDOW = 128
ROT_DIM = HEAD_DIM // 4
ROPE_THETA = 500000.0
CONV_WIDTH = 3
LN_EPS = 1e-5
RMS_EPS = 1e-6

LANES = 128
SUBLANES = 8
BF16_ROWS = 16
VMEM_LIMIT_BYTES = 60000 * 1024

TM_PROJ = 512
TM_ATTN = 512
SUB_ATTN = 256
TM_FFN = 512
TF_FFN = 512

_TN = (((0,), (0,)), ((), ()))


def _resident(block_shape, index_map):
    return pl.BlockSpec(block_shape, index_map, pipeline_mode=pl.Buffered(1))


def _layer_norm(x, g, b):
    mu = jnp.mean(x, axis=-1, keepdims=True)
    xc = x - mu
    var = jnp.mean(xc * xc, axis=-1, keepdims=True)
    return xc * lax.rsqrt(var + LN_EPS) * g + b


def _rms_norm(x, g):
    ms = jnp.mean(x * x, axis=-1, keepdims=True)
    return x * lax.rsqrt(ms + RMS_EPS) * g


def _proj_body(x_ref, pos_ref, invf_ref, w_ref, cw_ref, gc_ref,
               wg_ref, wu_ref, wo_ref,
               q_ref, k2_ref, vt_ref, co_ref, wgu_ref, wob_ref, zbuf_ref, *, dims, tiles_per_seq, tf):
    aw, kvw, cw, attn_scale = dims
    tm = x_ref.shape[0]
    i = pl.program_id(0)

    @pl.when(i % tiles_per_seq == 0)
    def _():
        zbuf_ref[0:SUBLANES, :] = jnp.zeros((SUBLANES, cw), F32)

    for j in range(wg_ref.shape[1] // tf):
        src = slice(j * tf, (j + 1) * tf)
        wgu_ref[:, 2 * j * tf:(2 * j + 1) * tf] = wg_ref[:, src].astype(BF16)
        wgu_ref[:, (2 * j + 1) * tf:(2 * j + 2) * tf] = wu_ref[:, src].astype(BF16)
    wob_ref[...] = wo_ref[...].astype(BF16)

    xb = x_ref[...].astype(BF16)

    ang = invf_ref[...] * pos_ref[...].astype(F32)
    cos_t = jnp.cos(ang)
    sin_t = jnp.sin(ang)

    def proj(lo, width):
        return jnp.dot(xb, w_ref[:, lo:lo + width], preferred_element_type=F32)

    c0 = aw + 2 * kvw
    z = proj(c0, cw) * proj(c0 + 2 * cw, cw)
    zbuf_ref[SUBLANES:SUBLANES + tm, :] = z
    z1 = zbuf_ref[SUBLANES - 1:SUBLANES - 1 + tm, :]
    z2 = zbuf_ref[SUBLANES - 2:SUBLANES - 2 + tm, :]
    conv = cw_ref[0:1, :] * z2 + cw_ref[1:2, :] * z1 + cw_ref[2:3, :] * z
    y = proj(c0 + cw, cw) * conv
    co_ref[...] = _rms_norm(y, gc_ref[...]).astype(BF16)
    zbuf_ref[0:SUBLANES, :] = zbuf_ref[tm:tm + SUBLANES, :]

    half = ROT_DIM // 2

    def rope_t(tt, head):
        b0 = head * HEAD_DIM
        t1 = tt[b0:b0 + half, :]
        t2 = tt[b0 + half:b0 + ROT_DIM, :]
        return jnp.concatenate([t1 * cos_t - t2 * sin_t, t2 * cos_t + t1 * sin_t,
                                tt[b0 + ROT_DIM:b0 + HEAD_DIM, :]], axis=0)

    kt = proj(aw, kvw).T
    k = jnp.concatenate([rope_t(kt, hd) for hd in range(kvw // HEAD_DIM)], axis=0).T
    low = lax.broadcasted_iota(jnp.int32, (tm, LANES), 1) < HEAD_DIM
    for c in range(kvw // LANES):
        kr = k[:, c * LANES:(c + 1) * LANES]
        sw = pltpu.roll(kr, HEAD_DIM, 1)
        even = slice(2 * c * LANES, (2 * c + 1) * LANES)
        odd = slice((2 * c + 1) * LANES, (2 * c + 2) * LANES)
        second = lambda blk: slice(2 * kvw + blk.start, 2 * kvw + blk.stop)
        k2_ref[:, even] = jnp.where(low, kr, 0.0).astype(BF16)
        k2_ref[:, odd] = jnp.where(low, sw, 0.0).astype(BF16)
        k2_ref[:, second(even)] = jnp.where(low, 0.0, sw).astype(BF16)
        k2_ref[:, second(odd)] = jnp.where(low, 0.0, kr).astype(BF16)

    qt = proj(0, aw).T
    for hd in range(aw // HEAD_DIM):
        q_ref[hd * HEAD_DIM:(hd + 1) * HEAD_DIM, :] = (rope_t(qt, hd) * attn_scale).astype(BF16)

    vt_ref[...] = proj(aw + kvw, kvw).T.astype(BF16)


def _attn_body(sinks_ref, q_ref, kc_ref, kp_ref, vtc_ref, vtp_ref, co_ref, x_ref,
               gat_ref, wo_ref, g1_ref, b1_ref, wd_ref, h_ref, wdb_ref, at_ref,
               *, n_q_heads, alpha, tiles_per_seq):
    aw, tm = q_ref.shape
    wdb_ref[...] = wd_ref[...].astype(BF16)
    W = WINDOW
    group = n_q_heads // N_KV_HEADS
    assert group == 4 and HEAD_DIM * 2 == LANES
    i = pl.program_id(0)
    prev_bias = jnp.where(i % tiles_per_seq == 0, -jnp.inf, 0.0)

    kj = lax.broadcasted_iota(jnp.int32, (W, 2 * W), 0)
    qi = lax.broadcasted_iota(jnp.int32, (W, 2 * W), 1) % W
    use_cur = kj <= qi
    cur_mask = jnp.where(use_cur, 1.0, 0.0).astype(BF16)
    prev_mask = jnp.where(use_cur, 0.0, 1.0).astype(BF16)
    ones_rows = jnp.ones((BF16_ROWS, 2 * W), BF16)

    copy_lanes = kc_ref.shape[1] // 2

    def window_scores(w):
        rows = slice(w * W, (w + 1) * W)
        scores = []
        for g in range(N_KV_HEADS):
            qg = jnp.concatenate([q_ref[(2 * g) * LANES:(2 * g + 1) * LANES, rows],
                                  q_ref[(2 * g + 1) * LANES:(2 * g + 2) * LANES, rows]], axis=1)
            for off in (0, copy_lanes):
                blk = slice(off + g * LANES, off + (g + 1) * LANES)
                if w == 0:
                    kk = jnp.concatenate([kp_ref[:, blk], kc_ref[0:W, blk]], axis=0)
                else:
                    kk = kc_ref[(w - 1) * W:(w + 1) * W, blk]
                scores.append(jnp.dot(kk, qg, preferred_element_type=F32))
        return scores

    def window_values(w, scores):
        rows = slice(w * W, (w + 1) * W)
        probs = []
        for n, s in enumerate(scores):
            g, half = divmod(n, 2)
            s_prev = s[0:W, :]
            if w == 0:
                s_prev = s_prev + prev_bias
            merged = jnp.where(use_cur, s[W:2 * W, :], s_prev)
            heads = (group * g + half, group * g + 2 + half)
            sink = jnp.concatenate([jnp.full((1, W), sinks_ref[heads[0]], F32),
                                    jnp.full((1, W), sinks_ref[heads[1]], F32)], axis=1)
            m = jnp.maximum(jnp.max(merged, axis=0, keepdims=True), sink)
            e = jnp.exp(merged - m).astype(BF16)
            ek = jnp.concatenate([e * prev_mask, e * cur_mask], axis=0)
            probs.append((heads, ek, jnp.exp(sink - m)))
        for n, (heads, ek, sink_e) in enumerate(probs):
            g = n // 2
            hrows = slice(g * HEAD_DIM, (g + 1) * HEAD_DIM)
            if w == 0:
                vt = jnp.concatenate([vtp_ref[hrows, :], vtc_ref[hrows, 0:W]], axis=1)
            else:
                vt = vtc_ref[hrows, (w - 1) * W:(w + 1) * W]
            vt_aug = jnp.concatenate([vt, ones_rows], axis=0)
            o = jnp.dot(vt_aug, ek, preferred_element_type=F32)
            denom = o[HEAD_DIM:HEAD_DIM + 1, :] + sink_e
            on = o[0:HEAD_DIM, :] * (1.0 / denom)
            for t in range(2):
                hd = heads[t]
                at_ref[hd * HEAD_DIM:(hd + 1) * HEAD_DIM, rows] = on[:, t * W:(t + 1) * W]

    def out_proj(rows):
        at = at_ref[:, rows]
        ms = jnp.mean(at * at, axis=0, keepdims=True)
        gat = jnp.concatenate([gat_ref[...]] * ((rows.stop - rows.start) // LANES), axis=1)
        atn = (at * lax.rsqrt(ms + RMS_EPS) * gat).astype(BF16)
        mix = lax.dot_general(atn, wo_ref[0:aw, :], _TN, preferred_element_type=F32)
        return mix + jnp.dot(co_ref[rows, :], wo_ref[aw:, :], preferred_element_type=F32)

    def finish(rows, mix):
        h_ref[rows, :] = _layer_norm(alpha * x_ref[rows, :] + mix, g1_ref[...], b1_ref[...])

    wpc = SUB_ATTN // W
    n_chunks = tm // SUB_ATTN
    for w in range(wpc):
        window_values(w, window_scores(w))
    pending = None
    for c in range(n_chunks):
        rows = slice(c * SUB_ATTN, (c + 1) * SUB_ATTN)
        nxt = [(w, window_scores(w)) for w in range((c + 1) * wpc, (c + 2) * wpc)] if c + 1 < n_chunks else []
        mix = out_proj(rows)
        if pending is not None:
            finish(*pending)
        for w, scores in nxt:
            window_values(w, scores)
        pending = (rows, mix)
    finish(*pending)


def _zero_after(x):
    bits = pltpu.bitcast(jnp.max(x, axis=(0, 1), keepdims=True), jnp.uint32)
    return pltpu.bitcast((bits >> 16) >> 16, F32)


def _ffn_body(h_ref, wgu_a_ref, wgu_b_ref, wd_a_ref, wd_b_ref, g2_ref, b2_ref, o_ref, hb_ref, acc_ref,
              *, alpha, n_pairs, n_row_tiles):
    i = pl.program_id(0)
    s = pl.program_id(1)
    n_steps = pl.num_programs(1)
    s_eff = jnp.where(i % 2 == 0, s, n_steps - 1 - s)
    tf = wd_a_ref.shape[0]
    live = i < n_row_tiles
    first = s == 0
    pair = s_eff < n_pairs

    @pl.when(jnp.logical_and(first, i == 0))
    def _():
        acc_ref[...] = jnp.zeros(acc_ref.shape, F32)

    @pl.when(jnp.logical_and(first, live))
    def _():
        hb_ref[...] = h_ref[...].astype(BF16)

    def gate_up(w_ref):
        return jnp.dot(hb_ref[...], w_ref[...], preferred_element_type=F32)

    def activation(gu, anchor=None):
        gate = gu[:, 0:tf]
        if anchor is not None:
            gate = gate + anchor
        return (gate * (1.0 / (1.0 + jnp.exp(-gate))) * gu[:, tf:2 * tf]).astype(BF16)

    def finish_previous():
        y = _layer_norm(acc_ref[...], g2_ref[...], b2_ref[...])
        o_ref[...] = y
        return y

    def step_body(pair_step, first_step):
        gus = [gate_up(wgu_a_ref)] + ([gate_up(wgu_b_ref)] if pair_step else [])
        anchors = [None] * len(gus)
        if first_step:
            anchors[-1] = _zero_after(finish_previous())
        acts = [activation(gu, an) for gu, an in zip(gus, anchors)]
        down = jnp.dot(acts[0], wd_a_ref[...], preferred_element_type=F32)
        base = alpha * h_ref[...] if first_step else acc_ref[...]
        acc_ref[...] = base + down
        if pair_step:
            acc_ref[...] += jnp.dot(acts[1], wd_b_ref[...], preferred_element_type=F32)

    for pair_step in (True, False):
        for first_step in (True, False):
            cond = jnp.logical_and(live, jnp.logical_and(pair == pair_step, first == first_step))
            pl.when(cond)(functools.partial(step_body, pair_step, first_step))

    @pl.when(jnp.logical_and(first, jnp.logical_not(live)))
    def _():
        finish_previous()


def _rope_inv_freq():
    inv_freq = ROPE_THETA ** (-jnp.arange(0, ROT_DIM, 2, dtype=F32) / ROT_DIM)
    return inv_freq.reshape(ROT_DIM // 2, 1)


def _layer(h, pos_row, w_in, conv_w, sinks, g_attn, g_conv, w_out, ln1_g, ln1_b,
           w_gate, w_up, w_down, ln2_g, ln2_b, *, seq, alpha):
    m, d = h.shape
    aw = g_attn.shape[0]
    cw = g_conv.shape[0]
    kvw = N_KV_HEADS * HEAD_DIM
    n_q_heads = aw // HEAD_DIM
    d_ff = w_gate.shape[1]
    in_width = w_in.shape[1]
    assert in_width == aw + 2 * kvw + 3 * cw and d == aw + cw
    assert seq % TM_PROJ == 0 and seq % TM_ATTN == 0 and TM_ATTN % SUB_ATTN == 0 and SUB_ATTN % WINDOW == 0
    assert m % TM_FFN == 0 and d_ff % TF_FFN == 0

    invf = _rope_inv_freq()
    row = lambda a: a.reshape(1, -1).astype(F32)
    cparams = functools.partial(pltpu.CompilerParams, vmem_limit_bytes=VMEM_LIMIT_BYTES)
    w_in_b = w_in.astype(BF16)

    tm = TM_PROJ
    tf = TF_FFN
    steps = m // tm
    assert d % (steps * BF16_ROWS) == 0
    slab = d // steps
    const = lambda i: (0, 0)
    by_row = lambda i: (i, 0)
    q, k2, vt, co, w_gu, w_out_b = pl.pallas_call(
        functools.partial(_proj_body, dims=(aw, kvw, cw, HEAD_DIM ** -0.5), tiles_per_seq=seq // tm, tf=tf),
        grid=(steps,),
        in_specs=[
            pl.BlockSpec((tm, d), by_row),
            pl.BlockSpec((1, tm), lambda i: (0, i)),
            _resident((ROT_DIM // 2, 1), const),
            _resident((d, in_width), const),
            _resident((CONV_WIDTH, cw), const),
            _resident((1, cw), const),
            pl.BlockSpec((slab, d_ff), by_row),
            pl.BlockSpec((slab, d_ff), by_row),
            pl.BlockSpec((slab, d), by_row),
        ],
        out_specs=[
            pl.BlockSpec((aw, tm), lambda i: (0, i)),
            pl.BlockSpec((tm, 4 * kvw), by_row),
            pl.BlockSpec((kvw, tm), lambda i: (0, i)),
            pl.BlockSpec((tm, cw), by_row),
            pl.BlockSpec((slab, 2 * d_ff), by_row),
            pl.BlockSpec((slab, d), by_row),
        ],
        out_shape=[
            jax.ShapeDtypeStruct((aw, m), BF16),
            jax.ShapeDtypeStruct((m, 4 * kvw), BF16),
            jax.ShapeDtypeStruct((kvw, m), BF16),
            jax.ShapeDtypeStruct((m, cw), BF16),
            jax.ShapeDtypeStruct((d, 2 * d_ff), BF16),
            jax.ShapeDtypeStruct((d, d), BF16),
        ],
        scratch_shapes=[pltpu.VMEM((tm + 2 * SUBLANES, cw), F32)],
        compiler_params=cparams(dimension_semantics=("arbitrary",)),
        name="proj_rope_conv",
    )(h, pos_row, invf, w_in_b, conv_w.astype(F32), row(g_conv), w_gate, w_up, w_out)

    tm = TM_ATTN
    wpt = tm // WINDOW
    cur = lambda i: (i, 0)
    prev = lambda i: (jnp.maximum(i * wpt - 1, 0), 0)
    cur_t = lambda i: (0, i)
    prev_t = lambda i: (0, jnp.maximum(i * wpt - 1, 0))
    g_attn_cols = jnp.broadcast_to(g_attn.astype(F32)[:, None], (aw, LANES))
    steps = m // tm
    assert d_ff % (steps * BF16_ROWS) == 0
    slab = d_ff // steps
    h1, w_down_b = pl.pallas_call(
        functools.partial(_attn_body, n_q_heads=n_q_heads, alpha=alpha, tiles_per_seq=seq // tm),
        grid=(steps,),
        in_specs=[
            pl.BlockSpec(memory_space=pltpu.SMEM),
            pl.BlockSpec((aw, tm), cur_t),
            pl.BlockSpec((tm, 4 * kvw), cur),
            pl.BlockSpec((WINDOW, 4 * kvw), prev),
            pl.BlockSpec((kvw, tm), cur_t),
            pl.BlockSpec((kvw, WINDOW), prev_t),
            pl.BlockSpec((tm, cw), cur),
            pl.BlockSpec((tm, d), cur),
            _resident((aw, LANES), const),
            _resident((d, d), const),
            _resident((1, d), const),
            _resident((1, d), const),
            pl.BlockSpec((slab, d), cur),
        ],
        out_specs=[pl.BlockSpec((tm, d), cur), pl.BlockSpec((slab, d), cur)],
        out_shape=[jax.ShapeDtypeStruct((m, d), F32), jax.ShapeDtypeStruct((d_ff, d), BF16)],
        scratch_shapes=[pltpu.VMEM((aw, tm), F32)],
        compiler_params=cparams(dimension_semantics=("arbitrary",)),
        name="swa_outproj_ln",
    )(sinks.astype(F32), q, k2, k2, vt, vt, co, h, g_attn_cols, w_out_b,
      row(ln1_g), row(ln1_b), w_down)

    tm, tf = TM_FFN, TF_FFN
    n_i, n_j = m // tm, d_ff // tf
    n_pairs = n_j // 2
    n_steps = n_pairs + n_j % 2
    last_step = n_steps - 1 if (n_i - 1) % 2 == 0 else 0

    def step(i, s):
        return jnp.where(i < n_i, jnp.where(i % 2 == 0, s, n_steps - 1 - s), last_step)

    tile_a = lambda i, s: 2 * step(i, s)
    tile_b = lambda i, s: jnp.minimum(2 * step(i, s) + 1, 2 * n_pairs - 1)
    row_i = lambda i, s: (jnp.minimum(i, n_i - 1), 0)
    out_i = lambda i, s: (jnp.clip(jnp.where(s == 0, i - 1, i), 0, n_i - 1), 0)
    out = pl.pallas_call(
        functools.partial(_ffn_body, alpha=alpha, n_pairs=n_pairs, n_row_tiles=n_i),
        grid=(n_i + 1, n_steps),
        in_specs=[
            pl.BlockSpec((tm, d), row_i),
            pl.BlockSpec((d, 2 * tf), lambda i, s: (0, tile_a(i, s))),
            pl.BlockSpec((d, 2 * tf), lambda i, s: (0, tile_b(i, s))),
            pl.BlockSpec((tf, d), lambda i, s: (tile_a(i, s), 0)),
            pl.BlockSpec((tf, d), lambda i, s: (tile_b(i, s), 0)),
            _resident((1, d), lambda i, s: (0, 0)),
            _resident((1, d), lambda i, s: (0, 0)),
        ],
        out_specs=pl.BlockSpec((tm, d), out_i),
        out_shape=jax.ShapeDtypeStruct((m, d), F32),
        scratch_shapes=[pltpu.VMEM((tm, d), BF16), pltpu.VMEM((tm, d), F32)],
        compiler_params=cparams(dimension_semantics=("arbitrary", "arbitrary")),
        name="swiglu_ffn_ln",
    )(h1, w_gu, w_gu, w_down_b, w_down_b, row(ln2_g), row(ln2_b))
    return out


def kernel(x, positions, w_in, conv_w, sinks, g_attn, g_conv, w_out, ln1_g, ln1_b, w_gate, w_up, w_down, ln2_g, ln2_b):
    b, s, d = x.shape
    depth = w_in.shape[0]
    alpha = (2 * depth) ** 0.25
    h = x.reshape(b * s, d)
    pos_row = positions.reshape(1, b * s)
    for l in range(depth):
        h = _layer(h, pos_row, w_in[l], conv_w[l], sinks[l], g_attn[l], g_conv[l], w_out[l],
                   ln1_g[l], ln1_b[l], w_gate[l], w_up[l], w_down[l], ln2_g[l], ln2_b[l],
                   seq=s, alpha=alpha)
    return h.reshape(b, s, d)
```

```python
import functools

import jax
import jax.numpy as jnp
from jax import lax
from jax.experimental import pallas as pl
from jax.experimental.pallas import tpu as pltpu

F32 = jnp.float32
BF16 = jnp.bfloat16

HEAD_DIM = 64
N_KV_HEADS = 4
WINDOW = 128
ROT_DIM = HEAD_DIM // 4
ROPE_THETA = 500000.0
CONV_WIDTH = 3
LN_EPS = 1e-5
RMS_EPS = 1e-6

LANES = 128
SUBLANES = 8
BF16_ROWS = 16
VMEM_LIMIT_BYTES = 60000 * 1024

TM_PROJ = 512
TM_ATTN = 512
SUB_ATTN = 256
TM_FFN = 512
TF_FFN = 512

_TN = (((0,), (0,)), ((), ()))


def _resident(block_shape, index_map):
    return pl.BlockSpec(block_shape, index_map, pipeline_mode=pl.Buffered(1))


def _layer_norm(x, g, b):
    mu = jnp.mean(x, axis=-1, keepdims=True)
    xc = x - mu
    var = jnp.mean(xc * xc, axis=-1, keepdims=True)
    return xc * lax.rsqrt(var + LN_EPS) * g + b


def _rms_norm(x, g):
    ms = jnp.mean(x * x, axis=-1, keepdims=True)
    return x * lax.rsqrt(ms + RMS_EPS) * g


def _proj_body(x_ref, pos_ref, invf_ref, w_ref, cw_ref, gc_ref,
               wg_ref, wu_ref, wo_ref,
               q_ref, k2_ref, vt_ref, co_ref, wgu_ref, wob_ref, zbuf_ref, *, dims, tiles_per_seq, tf):
    aw, kvw, cw, attn_scale = dims
    tm = x_ref.shape[0]
    i = pl.program_id(0)

    @pl.when(i % tiles_per_seq == 0)
    def _():
        zbuf_ref[0:SUBLANES, :] = jnp.zeros((SUBLANES, cw), F32)

    for j in range(wg_ref.shape[1] // tf):
        src = slice(j * tf, (j + 1) * tf)
        wgu_ref[:, 2 * j * tf:(2 * j + 1) * tf] = wg_ref[:, src].astype(BF16)
        wgu_ref[:, (2 * j + 1) * tf:(2 * j + 2) * tf] = wu_ref[:, src].astype(BF16)
    wob_ref[...] = wo_ref[...].astype(BF16)

    xb = x_ref[...].astype(BF16)

    ang = invf_ref[...] * pos_ref[...].astype(F32)
    cos_t = jnp.cos(ang)
    sin_t = jnp.sin(ang)

    def proj(lo, width):
        return jnp.dot(xb, w_ref[:, lo:lo + width], preferred_element_type=F32)

    c0 = aw + 2 * kvw
    z = proj(c0, cw) * proj(c0 + 2 * cw, cw)
    zbuf_ref[SUBLANES:SUBLANES + tm, :] = z
    z1 = zbuf_ref[SUBLANES - 1:SUBLANES - 1 + tm, :]
    z2 = zbuf_ref[SUBLANES - 2:SUBLANES - 2 + tm, :]
    conv = cw_ref[0:1, :] * z2 + cw_ref[1:2, :] * z1 + cw_ref[2:3, :] * z
    y = proj(c0 + cw, cw) * conv
    co_ref[...] = _rms_norm(y, gc_ref[...]).astype(BF16)
    zbuf_ref[0:SUBLANES, :] = zbuf_ref[tm:tm + SUBLANES, :]

    half = ROT_DIM // 2

    def rope_t(tt, head):
        b0 = head * HEAD_DIM
        t1 = tt[b0:b0 + half, :]
        t2 = tt[b0 + half:b0 + ROT_DIM, :]
        return jnp.concatenate([t1 * cos_t - t2 * sin_t, t2 * cos_t + t1 * sin_t,
                                tt[b0 + ROT_DIM:b0 + HEAD_DIM, :]], axis=0)

    kt = proj(aw, kvw).T
    k = jnp.concatenate([rope_t(kt, hd) for hd in range(kvw // HEAD_DIM)], axis=0).T
    low = lax.broadcasted_iota(jnp.int32, (tm, LANES), 1) < HEAD_DIM
    for c in range(kvw // LANES):
        kr = k[:, c * LANES:(c + 1) * LANES]
        sw = pltpu.roll(kr, HEAD_DIM, 1)
        even = slice(2 * c * LANES, (2 * c + 1) * LANES)
        odd = slice((2 * c + 1) * LANES, (2 * c + 2) * LANES)
        second = lambda blk: slice(2 * kvw + blk.start, 2 * kvw + blk.stop)
        k2_ref[:, even] = jnp.where(low, kr, 0.0).astype(BF16)
        k2_ref[:, odd] = jnp.where(low, sw, 0.0).astype(BF16)
        k2_ref[:, second(even)] = jnp.where(low, 0.0, sw).astype(BF16)
        k2_ref[:, second(odd)] = jnp.where(low, 0.0, kr).astype(BF16)

    qt = proj(0, aw).T
    for hd in range(aw // HEAD_DIM):
        q_ref[hd * HEAD_DIM:(hd + 1) * HEAD_DIM, :] = (rope_t(qt, hd) * attn_scale).astype(BF16)

    vt_ref[...] = proj(aw + kvw, kvw).T.astype(BF16)


def _attn_body(sinks_ref, q_ref, kc_ref, kp_ref, vtc_ref, vtp_ref, co_ref, x_ref,
               gat_ref, wo_ref, g1_ref, b1_ref, wd_ref, h_ref, wdb_ref, at_ref,
               *, n_q_heads, alpha, tiles_per_seq):
    aw, tm = q_ref.shape
    wdb_ref[...] = wd_ref[...].astype(BF16)
    W = WINDOW
    group = n_q_heads // N_KV_HEADS
    assert group == 4 and HEAD_DIM * 2 == LANES
    i = pl.program_id(0)
    prev_bias = jnp.where(i % tiles_per_seq == 0, -jnp.inf, 0.0)

    kj = lax.broadcasted_iota(jnp.int32, (W, 2 * W), 0)
    qi = lax.broadcasted_iota(jnp.int32, (W, 2 * W), 1) % W
    use_cur = kj <= qi
    cur_mask = jnp.where(use_cur, 1.0, 0.0).astype(BF16)
    prev_mask = jnp.where(use_cur, 0.0, 1.0).astype(BF16)
    ones_rows = jnp.ones((BF16_ROWS, 2 * W), BF16)

    copy_lanes = kc_ref.shape[1] // 2

    def window_scores(w):
        rows = slice(w * W, (w + 1) * W)
        scores = []
        for g in range(N_KV_HEADS):
            qg = jnp.concatenate([q_ref[(2 * g) * LANES:(2 * g + 1) * LANES, rows],
                                  q_ref[(2 * g + 1) * LANES:(2 * g + 2) * LANES, rows]], axis=1)
            for off in (0, copy_lanes):
                blk = slice(off + g * LANES, off + (g + 1) * LANES)
                if w == 0:
                    kk = jnp.concatenate([kp_ref[:, blk], kc_ref[0:W, blk]], axis=0)
                else:
                    kk = kc_ref[(w - 1) * W:(w + 1) * W, blk]
                scores.append(jnp.dot(kk, qg, preferred_element_type=F32))
        return scores

    def window_values(w, scores):
        rows = slice(w * W, (w + 1) * W)
        probs = []
        for n, s in enumerate(scores):
            g, half = divmod(n, 2)
            s_prev = s[0:W, :]
            if w == 0:
                s_prev = s_prev + prev_bias
            merged = jnp.where(use_cur, s[W:2 * W, :], s_prev)
            heads = (group * g + half, group * g + 2 + half)
            sink = jnp.concatenate([jnp.full((1, W), sinks_ref[heads[0]], F32),
                                    jnp.full((1, W), sinks_ref[heads[1]], F32)], axis=1)
            m = jnp.maximum(jnp.max(merged, axis=0, keepdims=True), sink)
            e = jnp.exp(merged - m).astype(BF16)
            ek = jnp.concatenate([e * prev_mask, e * cur_mask], axis=0)
            probs.append((heads, ek, jnp.exp(sink - m)))
        for n, (heads, ek, sink_e) in enumerate(probs):
            g = n // 2
            hrows = slice(g * HEAD_DIM, (g + 1) * HEAD_DIM)
            if w == 0:
                vt = jnp.concatenate([vtp_ref[hrows, :], vtc_ref[hrows, 0:W]], axis=1)
            else:
                vt = vtc_ref[hrows, (w - 1) * W:(w + 1) * W]
            vt_aug = jnp.concatenate([vt, ones_rows], axis=0)
            o = jnp.dot(vt_aug, ek, preferred_element_type=F32)
            denom = o[HEAD_DIM:HEAD_DIM + 1, :] + sink_e
            on = o[0:HEAD_DIM, :] * (1.0 / denom)
            for t in range(2):
                hd = heads[t]
                at_ref[hd * HEAD_DIM:(hd + 1) * HEAD_DIM, rows] = on[:, t * W:(t + 1) * W]

    def out_proj(rows):
        at = at_ref[:, rows]
        ms = jnp.mean(at * at, axis=0, keepdims=True)
        gat = jnp.concatenate([gat_ref[...]] * ((rows.stop - rows.start) // LANES), axis=1)
        atn = (at * lax.rsqrt(ms + RMS_EPS) * gat).astype(BF16)
        mix = lax.dot_general(atn, wo_ref[0:aw, :], _TN, preferred_element_type=F32)
        return mix + jnp.dot(co_ref[rows, :], wo_ref[aw:, :], preferred_element_type=F32)

    def finish(rows, mix):
        h_ref[rows, :] = _layer_norm(alpha * x_ref[rows, :] + mix, g1_ref[...], b1_ref[...])

    wpc = SUB_ATTN // W
    n_chunks = tm // SUB_ATTN
    for w in range(wpc):
        window_values(w, window_scores(w))
    pending = None
    for c in range(n_chunks):
        rows = slice(c * SUB_ATTN, (c + 1) * SUB_ATTN)
        nxt = [(w, window_scores(w)) for w in range((c + 1) * wpc, (c + 2) * wpc)] if c + 1 < n_chunks else []
        mix = out_proj(rows)
        if pending is not None:
            finish(*pending)
        for w, scores in nxt:
            window_values(w, scores)
        pending = (rows, mix)
    finish(*pending)


def _zero_after(x):
    bits = pltpu.bitcast(jnp.max(x, axis=(0, 1), keepdims=True), jnp.uint32)
    return pltpu.bitcast((bits >> 16) >> 16, F32)


def _ffn_body(h_ref, wgu_a_ref, wgu_b_ref, wd_a_ref, wd_b_ref, g2_ref, b2_ref, o_ref, hb_ref, acc_ref,
              *, alpha, single_pos, n_row_tiles):
    i = pl.program_id(0)
    s = pl.program_id(1)
    n_steps = pl.num_programs(1)
    s_eff = jnp.where(i % 2 == 0, s, n_steps - 1 - s)
    tf = wd_a_ref.shape[0]
    live = i < n_row_tiles
    first = s == 0
    pair = s_eff != single_pos

    @pl.when(jnp.logical_and(first, i == 0))
    def _():
        acc_ref[...] = jnp.zeros(acc_ref.shape, F32)

    @pl.when(jnp.logical_and(first, live))
    def _():
        hb_ref[...] = h_ref[...].astype(BF16)

    def gate_up(w_ref):
        return jnp.dot(hb_ref[...], w_ref[...], preferred_element_type=F32)

    def activation(gu, anchor=None):
        gate = gu[:, 0:tf]
        if anchor is not None:
            gate = gate + anchor
        return (gate * (1.0 / (1.0 + jnp.exp(-gate))) * gu[:, tf:2 * tf]).astype(BF16)

    def finish_previous(parts=1):
        rows = acc_ref.shape[0] // parts
        ys = []
        for p in range(parts):
            sl = slice(p * rows, (p + 1) * rows)
            y = _layer_norm(acc_ref[sl, :], g2_ref[...], b2_ref[...])
            o_ref[sl, :] = y
            ys.append(y)
        return ys

    def step_body(pair_step, first_step):
        gus = [gate_up(wgu_a_ref)] + ([gate_up(wgu_b_ref)] if pair_step else [])
        anchors = [None] * len(gus)
        if first_step:
            anchors = [_zero_after(y) for y in finish_previous(len(gus))]
        acts = [activation(gu, an) for gu, an in zip(gus, anchors)]
        down = jnp.dot(acts[0], wd_a_ref[...], preferred_element_type=F32)
        base = alpha * h_ref[...] if first_step else acc_ref[...]
        acc_ref[...] = base + down
        if pair_step:
            acc_ref[...] += jnp.dot(acts[1], wd_b_ref[...], preferred_element_type=F32)

    for pair_step, first_step in ((True, True), (True, False), (False, False)):
        cond = jnp.logical_and(live, jnp.logical_and(pair == pair_step, first == first_step))
        pl.when(cond)(functools.partial(step_body, pair_step, first_step))

    @pl.when(jnp.logical_and(first, jnp.logical_not(live)))
    def _():
        finish_previous()


def _rope_inv_freq():
    inv_freq = ROPE_THETA ** (-jnp.arange(0, ROT_DIM, 2, dtype=F32) / ROT_DIM)
    return inv_freq.reshape(ROT_DIM // 2, 1)


def _layer(h, pos_row, w_in, conv_w, sinks, g_attn, g_conv, w_out, ln1_g, ln1_b,
           w_gate, w_up, w_down, ln2_g, ln2_b, *, seq, alpha):
    m, d = h.shape
    aw = g_attn.shape[0]
    cw = g_conv.shape[0]
    kvw = N_KV_HEADS * HEAD_DIM
    n_q_heads = aw // HEAD_DIM
    d_ff = w_gate.shape[1]
    in_width = w_in.shape[1]
    assert in_width == aw + 2 * kvw + 3 * cw and d == aw + cw
    assert seq % TM_PROJ == 0 and seq % TM_ATTN == 0 and TM_ATTN % SUB_ATTN == 0 and SUB_ATTN % WINDOW == 0
    assert m % TM_FFN == 0 and d_ff % TF_FFN == 0

    invf = _rope_inv_freq()
    row = lambda a: a.reshape(1, -1).astype(F32)
    cparams = functools.partial(pltpu.CompilerParams, vmem_limit_bytes=VMEM_LIMIT_BYTES)
    w_in_b = w_in.astype(BF16)

    tm = TM_PROJ
    tf = TF_FFN
    steps = m // tm
    assert d % (steps * BF16_ROWS) == 0
    slab = d // steps
    const = lambda i: (0, 0)
    by_row = lambda i: (i, 0)
    q, k2, vt, co, w_gu, w_out_b = pl.pallas_call(
        functools.partial(_proj_body, dims=(aw, kvw, cw, HEAD_DIM ** -0.5), tiles_per_seq=seq // tm, tf=tf),
        grid=(steps,),
        in_specs=[
            pl.BlockSpec((tm, d), by_row),
            pl.BlockSpec((1, tm), lambda i: (0, i)),
            _resident((ROT_DIM // 2, 1), const),
            _resident((d, in_width), const),
            _resident((CONV_WIDTH, cw), const),
            _resident((1, cw), const),
            pl.BlockSpec((slab, d_ff), by_row),
            pl.BlockSpec((slab, d_ff), by_row),
            pl.BlockSpec((slab, d), by_row),
        ],
        out_specs=[
            pl.BlockSpec((aw, tm), lambda i: (0, i)),
            pl.BlockSpec((tm, 4 * kvw), by_row),
            pl.BlockSpec((kvw, tm), lambda i: (0, i)),
            pl.BlockSpec((tm, cw), by_row),
            pl.BlockSpec((slab, 2 * d_ff), by_row),
            pl.BlockSpec((slab, d), by_row),
        ],
        out_shape=[
            jax.ShapeDtypeStruct((aw, m), BF16),
            jax.ShapeDtypeStruct((m, 4 * kvw), BF16),
            jax.ShapeDtypeStruct((kvw, m), BF16),
            jax.ShapeDtypeStruct((m, cw), BF16),
            jax.ShapeDtypeStruct((d, 2 * d_ff), BF16),
            jax.ShapeDtypeStruct((d, d), BF16),
        ],
        scratch_shapes=[pltpu.VMEM((tm + 2 * SUBLANES, cw), F32)],
        compiler_params=cparams(dimension_semantics=("arbitrary",)),
        name="proj_rope_conv",
    )(h, pos_row, invf, w_in_b, conv_w.astype(F32), row(g_conv), w_gate, w_up, w_out)

    tm = TM_ATTN
    wpt = tm // WINDOW
    cur = lambda i: (i, 0)
    prev = lambda i: (jnp.maximum(i * wpt - 1, 0), 0)
    cur_t = lambda i: (0, i)
    prev_t = lambda i: (0, jnp.maximum(i * wpt - 1, 0))
    g_attn_cols = jnp.broadcast_to(g_attn.astype(F32)[:, None], (aw, LANES))
    steps = m // tm
    assert d_ff % (steps * BF16_ROWS) == 0
    slab = d_ff // steps
    h1, w_down_b = pl.pallas_call(
        functools.partial(_attn_body, n_q_heads=n_q_heads, alpha=alpha, tiles_per_seq=seq // tm),
        grid=(steps,),
        in_specs=[
            pl.BlockSpec(memory_space=pltpu.SMEM),
            pl.BlockSpec((aw, tm), cur_t),
            pl.BlockSpec((tm, 4 * kvw), cur),
            pl.BlockSpec((WINDOW, 4 * kvw), prev),
            pl.BlockSpec((kvw, tm), cur_t),
            pl.BlockSpec((kvw, WINDOW), prev_t),
            pl.BlockSpec((tm, cw), cur),
            pl.BlockSpec((tm, d), cur),
            _resident((aw, LANES), const),
            _resident((d, d), const),
            _resident((1, d), const),
            _resident((1, d), const),
            pl.BlockSpec((slab, d), cur),
        ],
        out_specs=[pl.BlockSpec((tm, d), cur), pl.BlockSpec((slab, d), cur)],
        out_shape=[jax.ShapeDtypeStruct((m, d), F32), jax.ShapeDtypeStruct((d_ff, d), BF16)],
        scratch_shapes=[pltpu.VMEM((aw, tm), F32)],
        compiler_params=cparams(dimension_semantics=("arbitrary",)),
        name="swa_outproj_ln",
    )(sinks.astype(F32), q, k2, k2, vt, vt, co, h, g_attn_cols, w_out_b,
      row(ln1_g), row(ln1_b), w_down)

    tm, tf = TM_FFN, TF_FFN
    n_i, n_j = m // tm, d_ff // tf
    n_pairs = n_j // 2
    n_steps = n_pairs + n_j % 2
    single_pos = (n_pairs + 1) // 2 if n_j % 2 else n_steps
    assert n_pairs >= 2 and single_pos not in (0, n_steps - 1)
    last_step = n_steps - 1 if (n_i - 1) % 2 == 0 else 0

    def step(i, s):
        return jnp.where(i < n_i, jnp.where(i % 2 == 0, s, n_steps - 1 - s), last_step)

    def pair_of(i, s):
        e = step(i, s)
        return jnp.where(e < single_pos, e, e - 1)

    tile_a = lambda i, s: jnp.where(step(i, s) == single_pos, n_j - 1, 2 * pair_of(i, s))
    tile_b = lambda i, s: 2 * pair_of(i, s) + 1
    row_i = lambda i, s: (jnp.minimum(i, n_i - 1), 0)
    out_i = lambda i, s: (jnp.clip(jnp.where(s == 0, i - 1, i), 0, n_i - 1), 0)
    out = pl.pallas_call(
        functools.partial(_ffn_body, alpha=alpha, single_pos=single_pos, n_row_tiles=n_i),
        grid=(n_i + 1, n_steps),
        in_specs=[
            pl.BlockSpec((tm, d), row_i),
            pl.BlockSpec((d, 2 * tf), lambda i, s: (0, tile_a(i, s))),
            pl.BlockSpec((d, 2 * tf), lambda i, s: (0, tile_b(i, s))),
            pl.BlockSpec((tf, d), lambda i, s: (tile_a(i, s), 0)),
            pl.BlockSpec((tf, d), lambda i, s: (tile_b(i, s), 0)),
            _resident((1, d), lambda i, s: (0, 0)),
            _resident((1, d), lambda i, s: (0, 0)),
        ],
        out_specs=pl.BlockSpec((tm, d), out_i),
        out_shape=jax.ShapeDtypeStruct((m, d), F32),
        scratch_shapes=[pltpu.VMEM((tm, d), BF16), pltpu.VMEM((tm, d), F32)],
        compiler_params=cparams(dimension_semantics=("arbitrary", "arbitrary")),
        name="swiglu_ffn_ln",
    )(h1, w_gu, w_gu, w_down_b, w_down_b, row(ln2_g), row(ln2_b))
    return out


def kernel(x, positions, w_in, conv_w, sinks, g_attn, g_conv, w_out, ln1_g, ln1_b, w_gate, w_up, w_down, ln2_g, ln2_b):
    b, s, d = x.shape
    depth = w_in.shape[0]
    alpha = (2 * depth) ** 0.25
    h = x.reshape(b * s, d)
    pos_row = positions.reshape(1, b * s)
    for l in range(depth):
        h = _layer(h, pos_row, w_in[l], conv_w[l], sinks[l], g_attn[l], g_conv[l], w_out[l],
                   ln1_g[l], ln1_b[l], w_gate[l], w_up[l], w_down[l], ln2_g[l], ln2_b[l],
                   seq=s, alpha=alpha)
    return h.reshape(b, s, d)
```

```python
import functools

import jax
import jax.numpy as jnp
from jax import lax
from jax.experimental import pallas as pl
from jax.experimental.pallas import tpu as pltpu

F32 = jnp.float32
BF16 = jnp.bfloat16

HEAD_DIM = 64
N_KV_HEADS = 4
WINDOW = 128
ROT_DIM = HEAD_DIM // 4
ROPE_THETA = 500000.0
CONV_WIDTH = 3
LN_EPS = 1e-5
RMS_EPS = 1e-6

LANES = 128
SUBLANES = 8
BF16_ROWS = 16
VMEM_LIMIT_BYTES = 60000 * 1024

TM_PROJ = 512
TM_ATTN = 512
SUB_ATTN = 256
TM_FFN = 512
TF_FFN = 512

_TN = (((0,), (0,)), ((), ()))


def _resident(block_shape, index_map):
    return pl.BlockSpec(block_shape, index_map, pipeline_mode=pl.Buffered(1))


def _layer_norm(x, g, b):
    mu = jnp.mean(x, axis=-1, keepdims=True)
    xc = x - mu
    var = jnp.mean(xc * xc, axis=-1, keepdims=True)
    return xc * lax.rsqrt(var + LN_EPS) * g + b


def _rms_norm(x, g):
    ms = jnp.mean(x * x, axis=-1, keepdims=True)
    return x * lax.rsqrt(ms + RMS_EPS) * g


def _conv_body(x_ref, w_ref, cw_ref, gc_ref, wg_ref, wu_ref, wo_ref,
               co_ref, wgu_ref, wob_ref, zbuf_ref, *, tiles_per_seq, tf):
    tm = x_ref.shape[0]
    cw = co_ref.shape[1]
    i = pl.program_id(0)

    @pl.when(i % tiles_per_seq == 0)
    def _():
        zbuf_ref[0:SUBLANES, :] = jnp.zeros((SUBLANES, cw), F32)

    for j in range(wg_ref.shape[1] // tf):
        src = slice(j * tf, (j + 1) * tf)
        wgu_ref[:, 2 * j * tf:(2 * j + 1) * tf] = wg_ref[:, src].astype(BF16)
        wgu_ref[:, (2 * j + 1) * tf:(2 * j + 2) * tf] = wu_ref[:, src].astype(BF16)
    wob_ref[...] = wo_ref[...].astype(BF16)

    xb = x_ref[...].astype(BF16)

    def proj(part):
        return jnp.dot(xb, w_ref[:, part * cw:(part + 1) * cw], preferred_element_type=F32)

    z = proj(0) * proj(2)
    zbuf_ref[SUBLANES:SUBLANES + tm, :] = z
    z1 = zbuf_ref[SUBLANES - 1:SUBLANES - 1 + tm, :]
    z2 = zbuf_ref[SUBLANES - 2:SUBLANES - 2 + tm, :]
    conv = cw_ref[0:1, :] * z2 + cw_ref[1:2, :] * z1 + cw_ref[2:3, :] * z
    y = proj(1) * conv
    co_ref[...] = _rms_norm(y, gc_ref[...]).astype(BF16)
    zbuf_ref[0:SUBLANES, :] = zbuf_ref[tm:tm + SUBLANES, :]


def _attn_body(sinks_ref, x_ref, pos_ref, invf_ref, wq_ref, co_ref,
               gat_ref, wo_ref, g1_ref, b1_ref, wd_ref, h_ref, wdb_ref,
               q_s, k_s, v_s, at_ref, *, n_q_heads, alpha, attn_scale, tiles_per_seq):
    aw, tm = q_s.shape
    kvw = v_s.shape[0]
    wdb_ref[...] = wd_ref[...].astype(BF16)
    W = WINDOW
    group = n_q_heads // N_KV_HEADS
    assert group == 4 and HEAD_DIM * 2 == LANES
    i = pl.program_id(0)
    prev_bias = jnp.where(i % tiles_per_seq == 0, -jnp.inf, 0.0)

    @pl.when(i == 0)
    def _():
        k_s[0:W, :] = jnp.zeros((W, k_s.shape[1]), BF16)
        v_s[:, 0:W] = jnp.zeros((kvw, W), BF16)

    kj = lax.broadcasted_iota(jnp.int32, (W, 2 * W), 0)
    qi = lax.broadcasted_iota(jnp.int32, (W, 2 * W), 1) % W
    use_cur = kj <= qi
    cur_mask = jnp.where(use_cur, 1.0, 0.0).astype(BF16)
    prev_mask = jnp.where(use_cur, 0.0, 1.0).astype(BF16)
    ones_rows = jnp.ones((BF16_ROWS, 2 * W), BF16)
    low = lax.broadcasted_iota(jnp.int32, (SUB_ATTN, LANES), 1) < HEAD_DIM
    half = ROT_DIM // 2

    def project(rows):
        xb = x_ref[rows, :].astype(BF16)
        ang = invf_ref[...] * pos_ref[:, rows].astype(F32)
        cos_t = jnp.cos(ang)
        sin_t = jnp.sin(ang)

        def proj(lo, width):
            return jnp.dot(xb, wq_ref[:, lo:lo + width], preferred_element_type=F32)

        def rope_t(tt, head):
            b0 = head * HEAD_DIM
            t1 = tt[b0:b0 + half, :]
            t2 = tt[b0 + half:b0 + ROT_DIM, :]
            return jnp.concatenate([t1 * cos_t - t2 * sin_t, t2 * cos_t + t1 * sin_t,
                                    tt[b0 + ROT_DIM:b0 + HEAD_DIM, :]], axis=0)

        kt = proj(aw, kvw).T
        k = jnp.concatenate([rope_t(kt, hd) for hd in range(kvw // HEAD_DIM)], axis=0).T
        krows = slice(W + rows.start, W + rows.stop)
        for c in range(kvw // LANES):
            kr = k[:, c * LANES:(c + 1) * LANES]
            sw = pltpu.roll(kr, HEAD_DIM, 1)
            even = slice(2 * c * LANES, (2 * c + 1) * LANES)
            odd = slice((2 * c + 1) * LANES, (2 * c + 2) * LANES)
            second = lambda blk: slice(2 * kvw + blk.start, 2 * kvw + blk.stop)
            k_s[krows, even] = jnp.where(low, kr, 0.0).astype(BF16)
            k_s[krows, odd] = jnp.where(low, sw, 0.0).astype(BF16)
            k_s[krows, second(even)] = jnp.where(low, 0.0, sw).astype(BF16)
            k_s[krows, second(odd)] = jnp.where(low, 0.0, kr).astype(BF16)

        qt = proj(0, aw).T
        for hd in range(aw // HEAD_DIM):
            q_s[hd * HEAD_DIM:(hd + 1) * HEAD_DIM, rows] = (rope_t(qt, hd) * attn_scale).astype(BF16)

        v_s[:, krows] = proj(aw + kvw, kvw).T.astype(BF16)

    copy_lanes = k_s.shape[1] // 2

    def window_scores(w):
        rows = slice(w * W, (w + 1) * W)
        scores = []
        for g in range(N_KV_HEADS):
            qg = jnp.concatenate([q_s[(2 * g) * LANES:(2 * g + 1) * LANES, rows],
                                  q_s[(2 * g + 1) * LANES:(2 * g + 2) * LANES, rows]], axis=1)
            for off in (0, copy_lanes):
                blk = slice(off + g * LANES, off + (g + 1) * LANES)
                kk = k_s[w * W:(w + 2) * W, blk]
                scores.append(jnp.dot(kk, qg, preferred_element_type=F32))
        return scores

    def window_values(w, scores):
        rows = slice(w * W, (w + 1) * W)
        probs = []
        for n, s in enumerate(scores):
            g, half_idx = divmod(n, 2)
            s_prev = s[0:W, :]
            if w == 0:
                s_prev = s_prev + prev_bias
            merged = jnp.where(use_cur, s[W:2 * W, :], s_prev)
            heads = (group * g + half_idx, group * g + 2 + half_idx)
            sink = jnp.concatenate([jnp.full((1, W), sinks_ref[heads[0]], F32),
                                    jnp.full((1, W), sinks_ref[heads[1]], F32)], axis=1)
            m = jnp.maximum(jnp.max(merged, axis=0, keepdims=True), sink)
            e = jnp.exp(merged - m).astype(BF16)
            ek = jnp.concatenate([e * prev_mask, e * cur_mask], axis=0)
            probs.append((heads, ek, jnp.exp(sink - m)))
        for n, (heads, ek, sink_e) in enumerate(probs):
            g = n // 2
            vt = v_s[g * HEAD_DIM:(g + 1) * HEAD_DIM, w * W:(w + 2) * W]
            vt_aug = jnp.concatenate([vt, ones_rows], axis=0)
            o = jnp.dot(vt_aug, ek, preferred_element_type=F32)
            denom = o[HEAD_DIM:HEAD_DIM + 1, :] + sink_e
            on = o[0:HEAD_DIM, :] * (1.0 / denom)
            for t in range(2):
                hd = heads[t]
                at_ref[hd * HEAD_DIM:(hd + 1) * HEAD_DIM, rows] = on[:, t * W:(t + 1) * W]

    def out_proj(rows):
        at = at_ref[:, rows]
        ms = jnp.mean(at * at, axis=0, keepdims=True)
        gat = jnp.concatenate([gat_ref[...]] * ((rows.stop - rows.start) // LANES), axis=1)
        atn = (at * lax.rsqrt(ms + RMS_EPS) * gat).astype(BF16)
        mix = lax.dot_general(atn, wo_ref[0:aw, :], _TN, preferred_element_type=F32)
        return mix + jnp.dot(co_ref[rows, :], wo_ref[aw:, :], preferred_element_type=F32)

    def finish(rows, mix):
        h_ref[rows, :] = _layer_norm(alpha * x_ref[rows, :] + mix, g1_ref[...], b1_ref[...])

    wpc = SUB_ATTN // W
    n_chunks = tm // SUB_ATTN
    chunk_rows = [slice(c * SUB_ATTN, (c + 1) * SUB_ATTN) for c in range(n_chunks)]
    windows = lambda c: range(c * wpc, (c + 1) * wpc)
    project(chunk_rows[0])
    scores = {w: window_scores(w) for w in windows(0)}
    pending = None
    for c in range(n_chunks):
        if c + 1 < n_chunks:
            project(chunk_rows[c + 1])
        for w in windows(c):
            window_values(w, scores.pop(w))
        if c + 1 < n_chunks:
            scores = {w: window_scores(w) for w in windows(c + 1)}
        mix = out_proj(chunk_rows[c])
        if pending is not None:
            finish(*pending)
        pending = (chunk_rows[c], mix)
    finish(*pending)

    k_s[0:W, :] = k_s[tm:tm + W, :]
    v_s[:, 0:W] = v_s[:, tm:tm + W]


def _zero_after(x):
    bits = pltpu.bitcast(jnp.max(x, axis=(0, 1), keepdims=True), jnp.uint32)
    return pltpu.bitcast((bits >> 16) >> 16, F32)


def _ffn_body(h_ref, wgu_a_ref, wgu_b_ref, wd_a_ref, wd_b_ref, g2_ref, b2_ref, o_ref, hb_ref, acc_ref,
              *, alpha, n_pairs, n_row_tiles):
    i = pl.program_id(0)
    s = pl.program_id(1)
    n_steps = pl.num_programs(1)
    s_eff = jnp.where(i % 2 == 0, s, n_steps - 1 - s)
    tf = wd_a_ref.shape[0]
    live = i < n_row_tiles
    first = s == 0
    pair = s_eff < n_pairs

    @pl.when(jnp.logical_and(first, i == 0))
    def _():
        acc_ref[...] = jnp.zeros(acc_ref.shape, F32)

    @pl.when(jnp.logical_and(first, live))
    def _():
        hb_ref[...] = h_ref[...].astype(BF16)

    def gate_up(w_ref):
        return jnp.dot(hb_ref[...], w_ref[...], preferred_element_type=F32)

    def activation(gu, anchor=None):
        gate = gu[:, 0:tf]
        if anchor is not None:
            gate = gate + anchor
        return (gate * (1.0 / (1.0 + jnp.exp(-gate))) * gu[:, tf:2 * tf]).astype(BF16)

    def finish_previous():
        y = _layer_norm(acc_ref[...], g2_ref[...], b2_ref[...])
        o_ref[...] = y
        return y

    def step_body(pair_step, first_step):
        gus = [gate_up(wgu_a_ref)] + ([gate_up(wgu_b_ref)] if pair_step else [])
        anchors = [None] * len(gus)
        if first_step:
            anchors[-1] = _zero_after(finish_previous())
        acts = [activation(gu, an) for gu, an in zip(gus, anchors)]
        down = jnp.dot(acts[0], wd_a_ref[...], preferred_element_type=F32)
        base = alpha * h_ref[...] if first_step else acc_ref[...]
        acc_ref[...] = base + down
        if pair_step:
            acc_ref[...] += jnp.dot(acts[1], wd_b_ref[...], preferred_element_type=F32)

    for pair_step in (True, False):
        for first_step in (True, False):
            cond = jnp.logical_and(live, jnp.logical_and(pair == pair_step, first == first_step))
            pl.when(cond)(functools.partial(step_body, pair_step, first_step))

    @pl.when(jnp.logical_and(first, jnp.logical_not(live)))
    def _():
        finish_previous()


def _rope_inv_freq():
    inv_freq = ROPE_THETA ** (-jnp.arange(0, ROT_DIM, 2, dtype=F32) / ROT_DIM)
    return inv_freq.reshape(ROT_DIM // 2, 1)


def _layer(h, pos_row, w_in, conv_w, sinks, g_attn, g_conv, w_out, ln1_g, ln1_b,
           w_gate, w_up, w_down, ln2_g, ln2_b, *, seq, alpha):
    m, d = h.shape
    aw = g_attn.shape[0]
    cw = g_conv.shape[0]
    kvw = N_KV_HEADS * HEAD_DIM
    n_q_heads = aw // HEAD_DIM
    d_ff = w_gate.shape[1]
    in_width = w_in.shape[1]
    qkv_width = aw + 2 * kvw
    assert in_width == qkv_width + 3 * cw and d == aw + cw
    assert seq % TM_PROJ == 0 and seq % TM_ATTN == 0 and TM_ATTN % SUB_ATTN == 0 and SUB_ATTN % WINDOW == 0
    assert m % TM_FFN == 0 and d_ff % TF_FFN == 0

    invf = _rope_inv_freq()
    row = lambda a: a.reshape(1, -1).astype(F32)
    cparams = functools.partial(pltpu.CompilerParams, vmem_limit_bytes=VMEM_LIMIT_BYTES)
    w_qkv_b = w_in[:, 0:qkv_width].astype(BF16)
    w_conv_b = w_in[:, qkv_width:].astype(BF16)

    tm = TM_PROJ
    tf = TF_FFN
    steps = m // tm
    assert d % (steps * BF16_ROWS) == 0
    slab = d // steps
    const = lambda i: (0, 0)
    by_row = lambda i: (i, 0)
    co, w_gu, w_out_b = pl.pallas_call(
        functools.partial(_conv_body, tiles_per_seq=seq // tm, tf=tf),
        grid=(steps,),
        in_specs=[
            pl.BlockSpec((tm, d), by_row),
            _resident((d, 3 * cw), const),
            _resident((CONV_WIDTH, cw), const),
            _resident((1, cw), const),
            pl.BlockSpec((slab, d_ff), by_row),
            pl.BlockSpec((slab, d_ff), by_row),
            pl.BlockSpec((slab, d), by_row),
        ],
        out_specs=[
            pl.BlockSpec((tm, cw), by_row),
            pl.BlockSpec((slab, 2 * d_ff), by_row),
            pl.BlockSpec((slab, d), by_row),
        ],
        out_shape=[
            jax.ShapeDtypeStruct((m, cw), BF16),
            jax.ShapeDtypeStruct((d, 2 * d_ff), BF16),
            jax.ShapeDtypeStruct((d, d), BF16),
        ],
        scratch_shapes=[pltpu.VMEM((tm + 2 * SUBLANES, cw), F32)],
        compiler_params=cparams(dimension_semantics=("arbitrary",)),
        name="conv_branch",
    )(h, w_conv_b, conv_w.astype(F32), row(g_conv), w_gate, w_up, w_out)

    tm = TM_ATTN
    cur = lambda i: (i, 0)
    g_attn_cols = jnp.broadcast_to(g_attn.astype(F32)[:, None], (aw, LANES))
    steps = m // tm
    assert d_ff % (steps * BF16_ROWS) == 0
    slab = d_ff // steps
    h1, w_down_b = pl.pallas_call(
        functools.partial(_attn_body, n_q_heads=n_q_heads, alpha=alpha, attn_scale=HEAD_DIM ** -0.5,
                          tiles_per_seq=seq // tm),
        grid=(steps,),
        in_specs=[
            pl.BlockSpec(memory_space=pltpu.SMEM),
            pl.BlockSpec((tm, d), cur),
            pl.BlockSpec((1, tm), lambda i: (0, i)),
            _resident((ROT_DIM // 2, 1), const),
            _resident((d, qkv_width), const),
            pl.BlockSpec((tm, cw), cur),
            _resident((aw, LANES), const),
            _resident((d, d), const),
            _resident((1, d), const),
            _resident((1, d), const),
            pl.BlockSpec((slab, d), cur),
        ],
        out_specs=[pl.BlockSpec((tm, d), cur), pl.BlockSpec((slab, d), cur)],
        out_shape=[jax.ShapeDtypeStruct((m, d), F32), jax.ShapeDtypeStruct((d_ff, d), BF16)],
        scratch_shapes=[
            pltpu.VMEM((aw, tm), BF16),
            pltpu.VMEM((WINDOW + tm, 4 * kvw), BF16),
            pltpu.VMEM((kvw, WINDOW + tm), BF16),
            pltpu.VMEM((aw, tm), F32),
        ],
        compiler_params=cparams(dimension_semantics=("arbitrary",)),
        name="qkv_swa_outproj_ln",
    )(sinks.astype(F32), h, pos_row, invf, w_qkv_b, co, g_attn_cols, w_out_b,
      row(ln1_g), row(ln1_b), w_down)

    tm, tf = TM_FFN, TF_FFN
    n_i, n_j = m // tm, d_ff // tf
    n_pairs = n_j // 2
    n_steps = n_pairs + n_j % 2
    last_step = n_steps - 1 if (n_i - 1) % 2 == 0 else 0

    def step(i, s):
        return jnp.where(i < n_i, jnp.where(i % 2 == 0, s, n_steps - 1 - s), last_step)

    tile_a = lambda i, s: 2 * step(i, s)
    tile_b = lambda i, s: jnp.minimum(2 * step(i, s) + 1, 2 * n_pairs - 1)
    row_i = lambda i, s: (jnp.minimum(i, n_i - 1), 0)
    out_i = lambda i, s: (jnp.clip(jnp.where(s == 0, i - 1, i), 0, n_i - 1), 0)
    out = pl.pallas_call(
        functools.partial(_ffn_body, alpha=alpha, n_pairs=n_pairs, n_row_tiles=n_i),
        grid=(n_i + 1, n_steps),
        in_specs=[
            pl.BlockSpec((tm, d), row_i),
            pl.BlockSpec((d, 2 * tf), lambda i, s: (0, tile_a(i, s))),
            pl.BlockSpec((d, 2 * tf), lambda i, s: (0, tile_b(i, s))),
            pl.BlockSpec((tf, d), lambda i, s: (tile_a(i, s), 0)),
            pl.BlockSpec((tf, d), lambda i, s: (tile_b(i, s), 0)),
            _resident((1, d), lambda i, s: (0, 0)),
            _resident((1, d), lambda i, s: (0, 0)),
        ],
        out_specs=pl.BlockSpec((tm, d), out_i),
        out_shape=jax.ShapeDtypeStruct((m, d), F32),
        scratch_shapes=[pltpu.VMEM((tm, d), BF16), pltpu.VMEM((tm, d), F32)],
        compiler_params=cparams(dimension_semantics=("arbitrary", "arbitrary")),
        name="swiglu_ffn_ln",
    )(h1, w_gu, w_gu, w_down_b, w_down_b, row(ln2_g), row(ln2_b))
    return out


def kernel(x, positions, w_in, conv_w, sinks, g_attn, g_conv, w_out, ln1_g, ln1_b, w_gate, w_up, w_down, ln2_g, ln2_b):
    b, s, d = x.shape
    depth = w_in.shape[0]
    alpha = (2 * depth) ** 0.25
    h = x.reshape(b * s, d)
    pos_row = positions.reshape(1, b * s)
    for l in range(depth):
        h = _layer(h, pos_row, w_in[l], conv_w[l], sinks[l], g_attn[l], g_conv[l], w_out[l],
                   ln1_g[l], ln1_b[l], w_gate[l], w_up[l], w_down[l], ln2_g[l], ln2_b[l],
                   seq=s, alpha=alpha)
    return h.reshape(b, s, d)
```

```python
import functools

import jax
import jax.numpy as jnp
from jax import lax
from jax.experimental import pallas as pl
from jax.experimental.pallas import tpu as pltpu

F32 = jnp.float32
BF16 = jnp.bfloat16

HEAD_DIM = 64
N_KV_HEADS = 4
WINDOW = 128
ROT_DIM = HEAD_DIM // 4
ROPE_THETA = 500000.0
CONV_WIDTH = 3
LN_EPS = 1e-5
RMS_EPS = 1e-6

LANES = 128
SUBLANES = 8
BF16_ROWS = 16
VMEM_LIMIT_BYTES = 60000 * 1024

TM_PROJ = 512
TM_ATTN = 512
SUB_ATTN = 256
SUB_MIX = 256
TM_FFN = 512
TF_FFN = 512

_TN = (((0,), (0,)), ((), ()))


def _resident(block_shape, index_map):
    return pl.BlockSpec(block_shape, index_map, pipeline_mode=pl.Buffered(1))


def _layer_norm(x, g, b):
    mu = jnp.mean(x, axis=-1, keepdims=True)
    xc = x - mu
    var = jnp.mean(xc * xc, axis=-1, keepdims=True)
    return xc * lax.rsqrt(var + LN_EPS) * g + b


def _rms_norm(x, g):
    ms = jnp.mean(x * x, axis=-1, keepdims=True)
    return x * lax.rsqrt(ms + RMS_EPS) * g


def _mix_body(x_ref, atn_ref, w_ref, cw_ref, gc_ref, wo_ref, g1_ref, b1_ref, wd_ref,
              h_ref, wdb_ref, zbuf_ref, *, alpha, tiles_per_seq):
    tm = x_ref.shape[0]
    aw = atn_ref.shape[0]
    cw = gc_ref.shape[1]
    i = pl.program_id(0)

    @pl.when(i % tiles_per_seq == 0)
    def _():
        zbuf_ref[0:SUBLANES, :] = jnp.zeros((SUBLANES, cw), F32)

    wdb_ref[...] = wd_ref[...].astype(BF16)

    xb = x_ref[...].astype(BF16)

    def proj(part):
        return jnp.dot(xb, w_ref[:, part * cw:(part + 1) * cw], preferred_element_type=F32)

    zbuf_ref[SUBLANES:SUBLANES + tm, :] = proj(0) * proj(2)
    yb = proj(1)
    for c in range(tm // SUB_MIX):
        rows = slice(c * SUB_MIX, (c + 1) * SUB_MIX)
        shifted = lambda back: zbuf_ref[SUBLANES - back + rows.start:SUBLANES - back + rows.stop, :]
        conv = cw_ref[0:1, :] * shifted(2) + cw_ref[1:2, :] * shifted(1) + cw_ref[2:3, :] * shifted(0)
        co = _rms_norm(yb[rows, :] * conv, gc_ref[...]).astype(BF16)
        mix = lax.dot_general(atn_ref[:, rows], wo_ref[0:aw, :], _TN, preferred_element_type=F32)
        mix = mix + jnp.dot(co, wo_ref[aw:, :], preferred_element_type=F32)
        h_ref[rows, :] = _layer_norm(alpha * x_ref[rows, :] + mix, g1_ref[...], b1_ref[...])
    zbuf_ref[0:SUBLANES, :] = zbuf_ref[tm:tm + SUBLANES, :]


def _attn_body(sinks_ref, x_ref, pos_ref, invf_ref, wq_ref, gat_ref, wc_ref, wg_ref, wu_ref, wo_ref,
               atn_ref, wcb_ref, wgu_ref, wob_ref,
               q_s, k_s, v_s, at_ref, *, n_q_heads, attn_scale, tiles_per_seq, tf):
    aw, tm = q_s.shape
    kvw = v_s.shape[0]
    wcb_ref[...] = wc_ref[:, wc_ref.shape[1] - wcb_ref.shape[1]:].astype(BF16)
    for j in range(wg_ref.shape[1] // tf):
        src = slice(j * tf, (j + 1) * tf)
        wgu_ref[:, 2 * j * tf:(2 * j + 1) * tf] = wg_ref[:, src].astype(BF16)
        wgu_ref[:, (2 * j + 1) * tf:(2 * j + 2) * tf] = wu_ref[:, src].astype(BF16)
    wob_ref[...] = wo_ref[...].astype(BF16)
    W = WINDOW
    group = n_q_heads // N_KV_HEADS
    assert group == 4 and HEAD_DIM * 2 == LANES
    i = pl.program_id(0)
    prev_bias = jnp.where(i % tiles_per_seq == 0, -jnp.inf, 0.0)

    @pl.when(i == 0)
    def _():
        k_s[0:W, :] = jnp.zeros((W, k_s.shape[1]), BF16)
        v_s[:, 0:W] = jnp.zeros((kvw, W), BF16)

    kj = lax.broadcasted_iota(jnp.int32, (W, 2 * W), 0)
    qi = lax.broadcasted_iota(jnp.int32, (W, 2 * W), 1) % W
    use_cur = kj <= qi
    cur_mask = jnp.where(use_cur, 1.0, 0.0).astype(BF16)
    prev_mask = jnp.where(use_cur, 0.0, 1.0).astype(BF16)
    ones_rows = jnp.ones((BF16_ROWS, 2 * W), BF16)
    low = lax.broadcasted_iota(jnp.int32, (SUB_ATTN, LANES), 1) < HEAD_DIM
    half = ROT_DIM // 2

    def project(rows):
        xb = x_ref[rows, :].astype(BF16)
        ang = invf_ref[...] * pos_ref[:, rows].astype(F32)
        cos_t = jnp.cos(ang)
        sin_t = jnp.sin(ang)

        def proj(lo, width):
            return jnp.dot(xb, wq_ref[:, lo:lo + width], preferred_element_type=F32)

        def rope_t(tt, head):
            b0 = head * HEAD_DIM
            t1 = tt[b0:b0 + half, :]
            t2 = tt[b0 + half:b0 + ROT_DIM, :]
            return jnp.concatenate([t1 * cos_t - t2 * sin_t, t2 * cos_t + t1 * sin_t,
                                    tt[b0 + ROT_DIM:b0 + HEAD_DIM, :]], axis=0)

        kt = proj(aw, kvw).T
        k = jnp.concatenate([rope_t(kt, hd) for hd in range(kvw // HEAD_DIM)], axis=0).T
        krows = slice(W + rows.start, W + rows.stop)
        for c in range(kvw // LANES):
            kr = k[:, c * LANES:(c + 1) * LANES]
            sw = pltpu.roll(kr, HEAD_DIM, 1)
            even = slice(2 * c * LANES, (2 * c + 1) * LANES)
            odd = slice((2 * c + 1) * LANES, (2 * c + 2) * LANES)
            second = lambda blk: slice(2 * kvw + blk.start, 2 * kvw + blk.stop)
            k_s[krows, even] = jnp.where(low, kr, 0.0).astype(BF16)
            k_s[krows, odd] = jnp.where(low, sw, 0.0).astype(BF16)
            k_s[krows, second(even)] = jnp.where(low, 0.0, sw).astype(BF16)
            k_s[krows, second(odd)] = jnp.where(low, 0.0, kr).astype(BF16)

        qt = proj(0, aw).T
        for hd in range(aw // HEAD_DIM):
            q_s[hd * HEAD_DIM:(hd + 1) * HEAD_DIM, rows] = (rope_t(qt, hd) * attn_scale).astype(BF16)

        v_s[:, krows] = proj(aw + kvw, kvw).T.astype(BF16)

    copy_lanes = k_s.shape[1] // 2

    def window_scores(w):
        rows = slice(w * W, (w + 1) * W)
        scores = []
        for g in range(N_KV_HEADS):
            qg = jnp.concatenate([q_s[(2 * g) * LANES:(2 * g + 1) * LANES, rows],
                                  q_s[(2 * g + 1) * LANES:(2 * g + 2) * LANES, rows]], axis=1)
            for off in (0, copy_lanes):
                blk = slice(off + g * LANES, off + (g + 1) * LANES)
                kk = k_s[w * W:(w + 2) * W, blk]
                scores.append(jnp.dot(kk, qg, preferred_element_type=F32))
        return scores

    def window_values(w, scores):
        rows = slice(w * W, (w + 1) * W)
        probs = []
        for n, s in enumerate(scores):
            g, half_idx = divmod(n, 2)
            s_prev = s[0:W, :]
            if w == 0:
                s_prev = s_prev + prev_bias
            merged = jnp.where(use_cur, s[W:2 * W, :], s_prev)
            heads = (group * g + half_idx, group * g + 2 + half_idx)
            sink = jnp.concatenate([jnp.full((1, W), sinks_ref[heads[0]], F32),
                                    jnp.full((1, W), sinks_ref[heads[1]], F32)], axis=1)
            m = jnp.maximum(jnp.max(merged, axis=0, keepdims=True), sink)
            e = jnp.exp(merged - m).astype(BF16)
            ek = jnp.concatenate([e * prev_mask, e * cur_mask], axis=0)
            probs.append((heads, ek, jnp.exp(sink - m)))
        for n, (heads, ek, sink_e) in enumerate(probs):
            g = n // 2
            vt = v_s[g * HEAD_DIM:(g + 1) * HEAD_DIM, w * W:(w + 2) * W]
            vt_aug = jnp.concatenate([vt, ones_rows], axis=0)
            o = jnp.dot(vt_aug, ek, preferred_element_type=F32)
            denom = o[HEAD_DIM:HEAD_DIM + 1, :] + sink_e
            on = o[0:HEAD_DIM, :] * (1.0 / denom)
            for t in range(2):
                hd = heads[t]
                at_ref[hd * HEAD_DIM:(hd + 1) * HEAD_DIM, rows] = on[:, t * W:(t + 1) * W]

    def normalise(rows):
        at = at_ref[:, rows]
        ms = jnp.mean(at * at, axis=0, keepdims=True)
        gat = jnp.concatenate([gat_ref[...]] * ((rows.stop - rows.start) // LANES), axis=1)
        atn_ref[:, rows] = (at * lax.rsqrt(ms + RMS_EPS) * gat).astype(BF16)

    wpc = SUB_ATTN // W
    n_chunks = tm // SUB_ATTN
    chunk_rows = [slice(c * SUB_ATTN, (c + 1) * SUB_ATTN) for c in range(n_chunks)]
    windows = lambda c: range(c * wpc, (c + 1) * wpc)
    project(chunk_rows[0])
    scores = {w: window_scores(w) for w in windows(0)}
    for c in range(n_chunks):
        if c + 1 < n_chunks:
            project(chunk_rows[c + 1])
        for w in windows(c):
            window_values(w, scores.pop(w))
        if c + 1 < n_chunks:
            scores = {w: window_scores(w) for w in windows(c + 1)}
        normalise(chunk_rows[c])

    k_s[0:W, :] = k_s[tm:tm + W, :]
    v_s[:, 0:W] = v_s[:, tm:tm + W]


def _zero_after(x):
    bits = pltpu.bitcast(jnp.max(x, axis=(0, 1), keepdims=True), jnp.uint32)
    return pltpu.bitcast((bits >> 16) >> 16, F32)


def _ffn_body(h_ref, wgu_a_ref, wgu_b_ref, wd_a_ref, wd_b_ref, g2_ref, b2_ref, o_ref, hb_ref, acc_ref,
              *, alpha, n_pairs, n_row_tiles):
    i = pl.program_id(0)
    s = pl.program_id(1)
    n_steps = pl.num_programs(1)
    s_eff = jnp.where(i % 2 == 0, s, n_steps - 1 - s)
    tf = wd_a_ref.shape[0]
    live = i < n_row_tiles
    first = s == 0
    pair = s_eff < n_pairs

    @pl.when(jnp.logical_and(first, i == 0))
    def _():
        acc_ref[...] = jnp.zeros(acc_ref.shape, F32)

    @pl.when(jnp.logical_and(first, live))
    def _():
        hb_ref[...] = h_ref[...].astype(BF16)

    def gate_up(w_ref):
        return jnp.dot(hb_ref[...], w_ref[...], preferred_element_type=F32)

    def activation(gu, anchor=None):
        gate = gu[:, 0:tf]
        if anchor is not None:
            gate = gate + anchor
        return (gate * (1.0 / (1.0 + jnp.exp(-gate))) * gu[:, tf:2 * tf]).astype(BF16)

    def finish_previous():
        y = _layer_norm(acc_ref[...], g2_ref[...], b2_ref[...])
        o_ref[...] = y
        return y

    def step_body(pair_step, first_step):
        gus = [gate_up(wgu_a_ref)] + ([gate_up(wgu_b_ref)] if pair_step else [])
        anchors = [None] * len(gus)
        if first_step:
            anchors[-1] = _zero_after(finish_previous())
        acts = [activation(gu, an) for gu, an in zip(gus, anchors)]
        down = jnp.dot(acts[0], wd_a_ref[...], preferred_element_type=F32)
        base = alpha * h_ref[...] if first_step else acc_ref[...]
        acc_ref[...] = base + down
        if pair_step:
            acc_ref[...] += jnp.dot(acts[1], wd_b_ref[...], preferred_element_type=F32)

    for pair_step in (True, False):
        for first_step in (True, False):
            cond = jnp.logical_and(live, jnp.logical_and(pair == pair_step, first == first_step))
            pl.when(cond)(functools.partial(step_body, pair_step, first_step))

    @pl.when(jnp.logical_and(first, jnp.logical_not(live)))
    def _():
        finish_previous()


def _rope_inv_freq():
    inv_freq = ROPE_THETA ** (-jnp.arange(0, ROT_DIM, 2, dtype=F32) / ROT_DIM)
    return inv_freq.reshape(ROT_DIM // 2, 1)


def _layer(h, pos_row, w_in, conv_w, sinks, g_attn, g_conv, w_out, ln1_g, ln1_b,
           w_gate, w_up, w_down, ln2_g, ln2_b, *, seq, alpha):
    m, d = h.shape
    aw = g_attn.shape[0]
    cw = g_conv.shape[0]
    kvw = N_KV_HEADS * HEAD_DIM
    n_q_heads = aw // HEAD_DIM
    d_ff = w_gate.shape[1]
    in_width = w_in.shape[1]
    qkv_width = aw + 2 * kvw
    assert in_width == qkv_width + 3 * cw and d == aw + cw
    assert seq % TM_PROJ == 0 and seq % TM_ATTN == 0 and TM_ATTN % SUB_ATTN == 0 and SUB_ATTN % WINDOW == 0
    assert m % TM_FFN == 0 and d_ff % TF_FFN == 0

    invf = _rope_inv_freq()
    row = lambda a: a.reshape(1, -1).astype(F32)
    cparams = functools.partial(pltpu.CompilerParams, vmem_limit_bytes=VMEM_LIMIT_BYTES)
    w_qkv_b = w_in[:, 0:qkv_width].astype(BF16)

    tm = TM_ATTN
    tf = TF_FFN
    steps = m // tm
    assert d % (steps * BF16_ROWS) == 0
    slab = d // steps
    const = lambda i: (0, 0)
    by_row = lambda i: (i, 0)
    by_col = lambda i: (0, i)
    g_attn_cols = jnp.broadcast_to(g_attn.astype(F32)[:, None], (aw, LANES))
    atn, w_conv_b, w_gu, w_out_b = pl.pallas_call(
        functools.partial(_attn_body, n_q_heads=n_q_heads, attn_scale=HEAD_DIM ** -0.5,
                          tiles_per_seq=seq // tm, tf=tf),
        grid=(steps,),
        in_specs=[
            pl.BlockSpec(memory_space=pltpu.SMEM),
            pl.BlockSpec((tm, d), by_row),
            pl.BlockSpec((1, tm), by_col),
            _resident((ROT_DIM // 2, 1), const),
            _resident((d, qkv_width), const),
            _resident((aw, LANES), const),
            pl.BlockSpec((slab, in_width), by_row),
            pl.BlockSpec((slab, d_ff), by_row),
            pl.BlockSpec((slab, d_ff), by_row),
            pl.BlockSpec((slab, d), by_row),
        ],
        out_specs=[
            pl.BlockSpec((aw, tm), by_col),
            pl.BlockSpec((slab, 3 * cw), by_row),
            pl.BlockSpec((slab, 2 * d_ff), by_row),
            pl.BlockSpec((slab, d), by_row),
        ],
        out_shape=[
            jax.ShapeDtypeStruct((aw, m), BF16),
            jax.ShapeDtypeStruct((d, 3 * cw), BF16),
            jax.ShapeDtypeStruct((d, 2 * d_ff), BF16),
            jax.ShapeDtypeStruct((d, d), BF16),
        ],
        scratch_shapes=[
            pltpu.VMEM((aw, tm), BF16),
            pltpu.VMEM((WINDOW + tm, 4 * kvw), BF16),
            pltpu.VMEM((kvw, WINDOW + tm), BF16),
            pltpu.VMEM((aw, tm), F32),
        ],
        compiler_params=cparams(dimension_semantics=("arbitrary",)),
        name="qkv_swa",
    )(sinks.astype(F32), h, pos_row, invf, w_qkv_b, g_attn_cols, w_in, w_gate, w_up, w_out)

    tm = TM_PROJ
    steps = m // tm
    assert d_ff % (steps * BF16_ROWS) == 0
    slab = d_ff // steps
    h1, w_down_b = pl.pallas_call(
        functools.partial(_mix_body, alpha=alpha, tiles_per_seq=seq // tm),
        grid=(steps,),
        in_specs=[
            pl.BlockSpec((tm, d), by_row),
            pl.BlockSpec((aw, tm), by_col),
            _resident((d, 3 * cw), const),
            _resident((CONV_WIDTH, cw), const),
            _resident((1, cw), const),
            _resident((d, d), const),
            _resident((1, d), const),
            _resident((1, d), const),
            pl.BlockSpec((slab, d), by_row),
        ],
        out_specs=[pl.BlockSpec((tm, d), by_row), pl.BlockSpec((slab, d), by_row)],
        out_shape=[jax.ShapeDtypeStruct((m, d), F32), jax.ShapeDtypeStruct((d_ff, d), BF16)],
        scratch_shapes=[pltpu.VMEM((tm + 2 * SUBLANES, cw), F32)],
        compiler_params=cparams(dimension_semantics=("arbitrary",)),
        name="conv_outproj_ln",
    )(h, atn, w_conv_b, conv_w.astype(F32), row(g_conv), w_out_b, row(ln1_g), row(ln1_b), w_down)

    tm, tf = TM_FFN, TF_FFN
    n_i, n_j = m // tm, d_ff // tf
    n_pairs = n_j // 2
    n_steps = n_pairs + n_j % 2
    last_step = n_steps - 1 if (n_i - 1) % 2 == 0 else 0

    def step(i, s):
        return jnp.where(i < n_i, jnp.where(i % 2 == 0, s, n_steps - 1 - s), last_step)

    tile_a = lambda i, s: 2 * step(i, s)
    tile_b = lambda i, s: jnp.minimum(2 * step(i, s) + 1, 2 * n_pairs - 1)
    row_i = lambda i, s: (jnp.minimum(i, n_i - 1), 0)
    out_i = lambda i, s: (jnp.clip(jnp.where(s == 0, i - 1, i), 0, n_i - 1), 0)
    out = pl.pallas_call(
        functools.partial(_ffn_body, alpha=alpha, n_pairs=n_pairs, n_row_tiles=n_i),
        grid=(n_i + 1, n_steps),
        in_specs=[
            pl.BlockSpec((tm, d), row_i),
            pl.BlockSpec((d, 2 * tf), lambda i, s: (0, tile_a(i, s))),
            pl.BlockSpec((d, 2 * tf), lambda i, s: (0, tile_b(i, s))),
            pl.BlockSpec((tf, d), lambda i, s: (tile_a(i, s), 0)),
            pl.BlockSpec((tf, d), lambda i, s: (tile_b(i, s), 0)),
            _resident((1, d), lambda i, s: (0, 0)),
            _resident((1, d), lambda i, s: (0, 0)),
        ],
        out_specs=pl.BlockSpec((tm, d), out_i),
        out_shape=jax.ShapeDtypeStruct((m, d), F32),
        scratch_shapes=[pltpu.VMEM((tm, d), BF16), pltpu.VMEM((tm, d), F32)],
        compiler_params=cparams(dimension_semantics=("arbitrary", "arbitrary")),
        name="swiglu_ffn_ln",
    )(h1, w_gu, w_gu, w_down_b, w_down_b, row(ln2_g), row(ln2_b))
    return out


def kernel(x, positions, w_in, conv_w, sinks, g_attn, g_conv, w_out, ln1_g, ln1_b, w_gate, w_up, w_down, ln2_g, ln2_b):
    b, s, d = x.shape
    depth = w_in.shape[0]
    alpha = (2 * depth) ** 0.25
    h = x.reshape(b * s, d)
    pos_row = positions.reshape(1, b * s)
    for l in range(depth):
        h = _layer(h, pos_row, w_in[l], conv_w[l], sinks[l], g_attn[l], g_conv[l], w_out[l],
                   ln1_g[l], ln1_b[l], w_gate[l], w_up[l], w_down[l], ln2_g[l], ln2_b[l],
                   seq=s, alpha=alpha)
    return h.reshape(b, s, d)
```

```python
import functools

import jax
import jax.numpy as jnp
from jax import lax
from jax.experimental import pallas as pl
from jax.experimental.pallas import tpu as pltpu

F32 = jnp.float32
BF16 = jnp.bfloat16

HEAD_DIM = 64
N_KV_HEADS = 4
WINDOW = 128
ROT_DIM = HEAD_DIM // 4
ROPE_THETA = 500000.0
CONV_WIDTH = 3
LN_EPS = 1e-5
RMS_EPS = 1e-6

LANES = 128
SUBLANES = 8
BF16_ROWS = 16
VMEM_LIMIT_BYTES = 60000 * 1024

TM_PROJ = 512
TM_ATTN = 512
SUB_ATTN = 256
SUB_MIX = 256
TM_FFN = 512
TF_FFN = 512

_TN = (((0,), (0,)), ((), ()))


def _resident(block_shape, index_map):
    return pl.BlockSpec(block_shape, index_map, pipeline_mode=pl.Buffered(1))


def _layer_norm(x, g, b):
    mu = jnp.mean(x, axis=-1, keepdims=True)
    xc = x - mu
    var = jnp.mean(xc * xc, axis=-1, keepdims=True)
    return xc * lax.rsqrt(var + LN_EPS) * g + b


def _rms_norm(x, g):
    ms = jnp.mean(x * x, axis=-1, keepdims=True)
    return x * lax.rsqrt(ms + RMS_EPS) * g


def _zero_after(x):
    bits = pltpu.bitcast(jnp.max(x, axis=(0, 1), keepdims=True), jnp.uint32)
    return pltpu.bitcast((bits >> 16) >> 16, F32)


def _mix_body(x_ref, atn_ref, w_ref, cw_ref, gc_ref, wo_ref, g1_ref, b1_ref, wd_ref,
              h_ref, wdb_ref, zbuf_ref, pre_ref, *, alpha, tiles_per_seq, n_row_tiles):
    tm = x_ref.shape[0]
    aw = atn_ref.shape[0]
    cw = gc_ref.shape[1]
    i = pl.program_id(0)
    live = i < n_row_tiles

    @pl.when(i == 0)
    def _():
        pre_ref[...] = jnp.zeros(pre_ref.shape, F32)

    @pl.when(i % tiles_per_seq == 0)
    def _():
        zbuf_ref[0:SUBLANES, :] = jnp.zeros((SUBLANES, cw), F32)

    def finish_previous():
        y = _layer_norm(pre_ref[...], g1_ref[...], b1_ref[...])
        h_ref[...] = y
        return y

    @pl.when(live)
    def _():
        wdb_ref[...] = wd_ref[...].astype(BF16)

        xb = x_ref[...].astype(BF16)

        def proj(part):
            return jnp.dot(xb, w_ref[:, part * cw:(part + 1) * cw], preferred_element_type=F32)

        zbuf_ref[SUBLANES:SUBLANES + tm, :] = proj(0) * proj(2)
        yb = proj(1) + _zero_after(finish_previous())
        for c in range(tm // SUB_MIX):
            rows = slice(c * SUB_MIX, (c + 1) * SUB_MIX)
            shifted = lambda back: zbuf_ref[SUBLANES - back + rows.start:SUBLANES - back + rows.stop, :]
            conv = cw_ref[0:1, :] * shifted(2) + cw_ref[1:2, :] * shifted(1) + cw_ref[2:3, :] * shifted(0)
            co = _rms_norm(yb[rows, :] * conv, gc_ref[...]).astype(BF16)
            mix = lax.dot_general(atn_ref[:, rows], wo_ref[0:aw, :], _TN, preferred_element_type=F32)
            mix = mix + jnp.dot(co, wo_ref[aw:, :], preferred_element_type=F32)
            pre_ref[rows, :] = alpha * x_ref[rows, :] + mix
        zbuf_ref[0:SUBLANES, :] = zbuf_ref[tm:tm + SUBLANES, :]

    @pl.when(jnp.logical_not(live))
    def _():
        finish_previous()


def _attn_body(sinks_ref, x_ref, pos_ref, invf_ref, wq_ref, gat_ref, wc_ref, wg_ref, wu_ref, wo_ref,
               atn_ref, wcb_ref, wgu_ref, wob_ref,
               q_s, k_s, v_s, at_ref, wqb_s, *, n_q_heads, attn_scale, tiles_per_seq, tf):
    aw, tm = q_s.shape
    kvw = v_s.shape[0]
    wcb_ref[...] = wc_ref[:, wc_ref.shape[1] - wcb_ref.shape[1]:].astype(BF16)
    for j in range(wg_ref.shape[1] // tf):
        src = slice(j * tf, (j + 1) * tf)
        wgu_ref[:, 2 * j * tf:(2 * j + 1) * tf] = wg_ref[:, src].astype(BF16)
        wgu_ref[:, (2 * j + 1) * tf:(2 * j + 2) * tf] = wu_ref[:, src].astype(BF16)
    wob_ref[...] = wo_ref[...].astype(BF16)
    W = WINDOW
    group = n_q_heads // N_KV_HEADS
    assert group == 4 and HEAD_DIM * 2 == LANES
    i = pl.program_id(0)
    prev_bias = jnp.where(i % tiles_per_seq == 0, -jnp.inf, 0.0)

    @pl.when(i == 0)
    def _():
        wqb_s[...] = wq_ref[...].astype(BF16)
        k_s[0:W, :] = jnp.zeros((W, k_s.shape[1]), BF16)
        v_s[:, 0:W] = jnp.zeros((kvw, W), BF16)

    kj = lax.broadcasted_iota(jnp.int32, (W, 2 * W), 0)
    qi = lax.broadcasted_iota(jnp.int32, (W, 2 * W), 1) % W
    use_cur = kj <= qi
    cur_mask = jnp.where(use_cur, 1.0, 0.0).astype(BF16)
    prev_mask = jnp.where(use_cur, 0.0, 1.0).astype(BF16)
    ones_rows = jnp.ones((BF16_ROWS, 2 * W), BF16)
    low = lax.broadcasted_iota(jnp.int32, (SUB_ATTN, LANES), 1) < HEAD_DIM
    half = ROT_DIM // 2

    def project(rows):
        xb = x_ref[rows, :].astype(BF16)
        ang = invf_ref[...] * pos_ref[:, rows].astype(F32)
        cos_t = jnp.cos(ang)
        sin_t = jnp.sin(ang)

        def proj(lo, width):
            return jnp.dot(xb, wqb_s[:, lo:lo + width], preferred_element_type=F32)

        def rope_t(tt, head):
            b0 = head * HEAD_DIM
            t1 = tt[b0:b0 + half, :]
            t2 = tt[b0 + half:b0 + ROT_DIM, :]
            return jnp.concatenate([t1 * cos_t - t2 * sin_t, t2 * cos_t + t1 * sin_t,
                                    tt[b0 + ROT_DIM:b0 + HEAD_DIM, :]], axis=0)

        kt = proj(aw, kvw).T
        k = jnp.concatenate([rope_t(kt, hd) for hd in range(kvw // HEAD_DIM)], axis=0).T
        krows = slice(W + rows.start, W + rows.stop)
        for c in range(kvw // LANES):
            kr = k[:, c * LANES:(c + 1) * LANES]
            sw = pltpu.roll(kr, HEAD_DIM, 1)
            even = slice(2 * c * LANES, (2 * c + 1) * LANES)
            odd = slice((2 * c + 1) * LANES, (2 * c + 2) * LANES)
            second = lambda blk: slice(2 * kvw + blk.start, 2 * kvw + blk.stop)
            k_s[krows, even] = jnp.where(low, kr, 0.0).astype(BF16)
            k_s[krows, odd] = jnp.where(low, sw, 0.0).astype(BF16)
            k_s[krows, second(even)] = jnp.where(low, 0.0, sw).astype(BF16)
            k_s[krows, second(odd)] = jnp.where(low, 0.0, kr).astype(BF16)

        qt = proj(0, aw).T
        for hd in range(aw // HEAD_DIM):
            q_s[hd * HEAD_DIM:(hd + 1) * HEAD_DIM, rows] = (rope_t(qt, hd) * attn_scale).astype(BF16)

        v_s[:, krows] = proj(aw + kvw, kvw).T.astype(BF16)

    copy_lanes = k_s.shape[1] // 2

    def window_scores(w):
        rows = slice(w * W, (w + 1) * W)
        scores = []
        for g in range(N_KV_HEADS):
            qg = jnp.concatenate([q_s[(2 * g) * LANES:(2 * g + 1) * LANES, rows],
                                  q_s[(2 * g + 1) * LANES:(2 * g + 2) * LANES, rows]], axis=1)
            for off in (0, copy_lanes):
                blk = slice(off + g * LANES, off + (g + 1) * LANES)
                kk = k_s[w * W:(w + 2) * W, blk]
                scores.append(jnp.dot(kk, qg, preferred_element_type=F32))
        return scores

    def window_values(w, scores):
        rows = slice(w * W, (w + 1) * W)
        probs = []
        for n, s in enumerate(scores):
            g, half_idx = divmod(n, 2)
            s_prev = s[0:W, :]
            if w == 0:
                s_prev = s_prev + prev_bias
            merged = jnp.where(use_cur, s[W:2 * W, :], s_prev)
            heads = (group * g + half_idx, group * g + 2 + half_idx)
            sink = jnp.concatenate([jnp.full((1, W), sinks_ref[heads[0]], F32),
                                    jnp.full((1, W), sinks_ref[heads[1]], F32)], axis=1)
            m = jnp.maximum(jnp.max(merged, axis=0, keepdims=True), sink)
            e = jnp.exp(merged - m).astype(BF16)
            ek = jnp.concatenate([e * prev_mask, e * cur_mask], axis=0)
            probs.append((heads, ek, jnp.exp(sink - m)))
        for n, (heads, ek, sink_e) in enumerate(probs):
            g = n // 2
            vt = v_s[g * HEAD_DIM:(g + 1) * HEAD_DIM, w * W:(w + 2) * W]
            vt_aug = jnp.concatenate([vt, ones_rows], axis=0)
            o = jnp.dot(vt_aug, ek, preferred_element_type=F32)
            denom = o[HEAD_DIM:HEAD_DIM + 1, :] + sink_e
            on = o[0:HEAD_DIM, :] * (1.0 / denom)
            for t in range(2):
                hd = heads[t]
                at_ref[hd * HEAD_DIM:(hd + 1) * HEAD_DIM, rows] = on[:, t * W:(t + 1) * W]

    def normalise(rows):
        at = at_ref[:, rows]
        ms = jnp.mean(at * at, axis=0, keepdims=True)
        gat = jnp.concatenate([gat_ref[...]] * ((rows.stop - rows.start) // LANES), axis=1)
        atn_ref[:, rows] = (at * lax.rsqrt(ms + RMS_EPS) * gat).astype(BF16)

    wpc = SUB_ATTN // W
    n_chunks = tm // SUB_ATTN
    chunk_rows = [slice(c * SUB_ATTN, (c + 1) * SUB_ATTN) for c in range(n_chunks)]
    windows = lambda c: range(c * wpc, (c + 1) * wpc)
    project(chunk_rows[0])
    scores = {w: window_scores(w) for w in windows(0)}
    for c in range(n_chunks):
        if c + 1 < n_chunks:
            project(chunk_rows[c + 1])
        for w in windows(c):
            window_values(w, scores.pop(w))
        if c + 1 < n_chunks:
            scores = {w: window_scores(w) for w in windows(c + 1)}
        normalise(chunk_rows[c])

    k_s[0:W, :] = k_s[tm:tm + W, :]
    v_s[:, 0:W] = v_s[:, tm:tm + W]


def _ffn_body(h_ref, wgu_a_ref, wgu_b_ref, wd_a_ref, wd_b_ref, g2_ref, b2_ref, o_ref, hb_ref, acc_ref,
              *, alpha, n_pairs, n_row_tiles):
    i = pl.program_id(0)
    s = pl.program_id(1)
    n_steps = pl.num_programs(1)
    s_eff = jnp.where(i % 2 == 0, s, n_steps - 1 - s)
    tf = wd_a_ref.shape[0]
    live = i < n_row_tiles
    first = s == 0
    pair = s_eff < n_pairs

    @pl.when(jnp.logical_and(first, i == 0))
    def _():
        acc_ref[...] = jnp.zeros(acc_ref.shape, F32)

    @pl.when(jnp.logical_and(first, live))
    def _():
        hb_ref[...] = h_ref[...].astype(BF16)

    def gate_up(w_ref):
        return jnp.dot(hb_ref[...], w_ref[...], preferred_element_type=F32)

    def activation(gu, anchor=None):
        gate = gu[:, 0:tf]
        if anchor is not None:
            gate = gate + anchor
        return (gate * (1.0 / (1.0 + jnp.exp(-gate))) * gu[:, tf:2 * tf]).astype(BF16)

    def finish_previous():
        y = _layer_norm(acc_ref[...], g2_ref[...], b2_ref[...])
        o_ref[...] = y
        return y

    def step_body(pair_step, first_step):
        gus = [gate_up(wgu_a_ref)] + ([gate_up(wgu_b_ref)] if pair_step else [])
        anchors = [None] * len(gus)
        if first_step:
            anchors[-1] = _zero_after(finish_previous())
        acts = [activation(gu, an) for gu, an in zip(gus, anchors)]
        down = jnp.dot(acts[0], wd_a_ref[...], preferred_element_type=F32)
        base = alpha * h_ref[...] if first_step else acc_ref[...]
        acc_ref[...] = base + down
        if pair_step:
            acc_ref[...] += jnp.dot(acts[1], wd_b_ref[...], preferred_element_type=F32)

    for pair_step in (True, False):
        for first_step in (True, False):
            cond = jnp.logical_and(live, jnp.logical_and(pair == pair_step, first == first_step))
            pl.when(cond)(functools.partial(step_body, pair_step, first_step))

    @pl.when(jnp.logical_and(first, jnp.logical_not(live)))
    def _():
        finish_previous()


def _rope_inv_freq():
    inv_freq = ROPE_THETA ** (-jnp.arange(0, ROT_DIM, 2, dtype=F32) / ROT_DIM)
    return inv_freq.reshape(ROT_DIM // 2, 1)


def _layer(h, pos_row, w_in, conv_w, sinks, g_attn, g_conv, w_out, ln1_g, ln1_b,
           w_gate, w_up, w_down, ln2_g, ln2_b, *, seq, alpha):
    m, d = h.shape
    aw = g_attn.shape[0]
    cw = g_conv.shape[0]
    kvw = N_KV_HEADS * HEAD_DIM
    n_q_heads = aw // HEAD_DIM
    d_ff = w_gate.shape[1]
    in_width = w_in.shape[1]
    qkv_width = aw + 2 * kvw
    assert in_width == qkv_width + 3 * cw and d == aw + cw
    assert seq % TM_PROJ == 0 and seq % TM_ATTN == 0 and TM_ATTN % SUB_ATTN == 0 and SUB_ATTN % WINDOW == 0
    assert m % TM_FFN == 0 and d_ff % TF_FFN == 0

    invf = _rope_inv_freq()
    row = lambda a: a.reshape(1, -1).astype(F32)
    cparams = functools.partial(pltpu.CompilerParams, vmem_limit_bytes=VMEM_LIMIT_BYTES)

    tm = TM_ATTN
    tf = TF_FFN
    steps = m // tm
    assert d % (steps * BF16_ROWS) == 0
    slab = d // steps
    const = lambda i: (0, 0)
    by_row = lambda i: (i, 0)
    by_col = lambda i: (0, i)
    g_attn_cols = jnp.broadcast_to(g_attn.astype(F32)[:, None], (aw, LANES))
    atn, w_conv_b, w_gu, w_out_b = pl.pallas_call(
        functools.partial(_attn_body, n_q_heads=n_q_heads, attn_scale=HEAD_DIM ** -0.5,
                          tiles_per_seq=seq // tm, tf=tf),
        grid=(steps,),
        in_specs=[
            pl.BlockSpec(memory_space=pltpu.SMEM),
            pl.BlockSpec((tm, d), by_row),
            pl.BlockSpec((1, tm), by_col),
            _resident((ROT_DIM // 2, 1), const),
            _resident((d, qkv_width), const),
            _resident((aw, LANES), const),
            pl.BlockSpec((slab, in_width), by_row),
            pl.BlockSpec((slab, d_ff), by_row),
            pl.BlockSpec((slab, d_ff), by_row),
            pl.BlockSpec((slab, d), by_row),
        ],
        out_specs=[
            pl.BlockSpec((aw, tm), by_col),
            pl.BlockSpec((slab, 3 * cw), by_row),
            pl.BlockSpec((slab, 2 * d_ff), by_row),
            pl.BlockSpec((slab, d), by_row),
        ],
        out_shape=[
            jax.ShapeDtypeStruct((aw, m), BF16),
            jax.ShapeDtypeStruct((d, 3 * cw), BF16),
            jax.ShapeDtypeStruct((d, 2 * d_ff), BF16),
            jax.ShapeDtypeStruct((d, d), BF16),
        ],
        scratch_shapes=[
            pltpu.VMEM((aw, tm), BF16),
            pltpu.VMEM((WINDOW + tm, 4 * kvw), BF16),
            pltpu.VMEM((kvw, WINDOW + tm), BF16),
            pltpu.VMEM((aw, tm), F32),
            pltpu.VMEM((d, qkv_width), BF16),
        ],
        compiler_params=cparams(dimension_semantics=("arbitrary",)),
        name="qkv_swa",
    )(sinks.astype(F32), h, pos_row, invf, w_in, g_attn_cols, w_in, w_gate, w_up, w_out)

    tm = TM_PROJ
    steps = m // tm
    assert d_ff % (steps * BF16_ROWS) == 0
    slab = d_ff // steps
    held_row = lambda i: (jnp.minimum(i, steps - 1), 0)
    held_col = lambda i: (0, jnp.minimum(i, steps - 1))
    lagged = lambda i: (jnp.maximum(i - 1, 0), 0)
    h1, w_down_b = pl.pallas_call(
        functools.partial(_mix_body, alpha=alpha, tiles_per_seq=seq // tm, n_row_tiles=steps),
        grid=(steps + 1,),
        in_specs=[
            pl.BlockSpec((tm, d), held_row),
            pl.BlockSpec((aw, tm), held_col),
            _resident((d, 3 * cw), const),
            _resident((CONV_WIDTH, cw), const),
            _resident((1, cw), const),
            _resident((d, d), const),
            _resident((1, d), const),
            _resident((1, d), const),
            pl.BlockSpec((slab, d), held_row),
        ],
        out_specs=[pl.BlockSpec((tm, d), lagged), pl.BlockSpec((slab, d), held_row)],
        out_shape=[jax.ShapeDtypeStruct((m, d), F32), jax.ShapeDtypeStruct((d_ff, d), BF16)],
        scratch_shapes=[pltpu.VMEM((tm + 2 * SUBLANES, cw), F32), pltpu.VMEM((tm, d), F32)],
        compiler_params=cparams(dimension_semantics=("arbitrary",)),
        name="conv_outproj_ln",
    )(h, atn, w_conv_b, conv_w.astype(F32), row(g_conv), w_out_b, row(ln1_g), row(ln1_b), w_down)

    tm, tf = TM_FFN, TF_FFN
    n_i, n_j = m // tm, d_ff // tf
    n_pairs = n_j // 2
    n_steps = n_pairs + n_j % 2
    last_step = n_steps - 1 if (n_i - 1) % 2 == 0 else 0

    def step(i, s):
        return jnp.where(i < n_i, jnp.where(i % 2 == 0, s, n_steps - 1 - s), last_step)

    tile_a = lambda i, s: 2 * step(i, s)
    tile_b = lambda i, s: jnp.minimum(2 * step(i, s) + 1, 2 * n_pairs - 1)
    row_i = lambda i, s: (jnp.minimum(i, n_i - 1), 0)
    out_i = lambda i, s: (jnp.clip(jnp.where(s == 0, i - 1, i), 0, n_i - 1), 0)
    out = pl.pallas_call(
        functools.partial(_ffn_body, alpha=alpha, n_pairs=n_pairs, n_row_tiles=n_i),
        grid=(n_i + 1, n_steps),
        in_specs=[
            pl.BlockSpec((tm, d), row_i),
            pl.BlockSpec((d, 2 * tf), lambda i, s: (0, tile_a(i, s))),
            pl.BlockSpec((d, 2 * tf), lambda i, s: (0, tile_b(i, s))),
            pl.BlockSpec((tf, d), lambda i, s: (tile_a(i, s), 0)),
            pl.BlockSpec((tf, d), lambda i, s: (tile_b(i, s), 0)),
            _resident((1, d), lambda i, s: (0, 0)),
            _resident((1, d), lambda i, s: (0, 0)),
        ],
        out_specs=pl.BlockSpec((tm, d), out_i),
        out_shape=jax.ShapeDtypeStruct((m, d), F32),
        scratch_shapes=[pltpu.VMEM((tm, d), BF16), pltpu.VMEM((tm, d), F32)],
        compiler_params=cparams(dimension_semantics=("arbitrary", "arbitrary")),
        name="swiglu_ffn_ln",
    )(h1, w_gu, w_gu, w_down_b, w_down_b, row(ln2_g), row(ln2_b))
    return out


def kernel(x, positions, w_in, conv_w, sinks, g_attn, g_conv, w_out, ln1_g, ln1_b, w_gate, w_up, w_down, ln2_g, ln2_b):
    b, s, d = x.shape
    depth = w_in.shape[0]
    alpha = (2 * depth) ** 0.25
    h = x.reshape(b * s, d)
    pos_row = positions.reshape(1, b * s)
    for l in range(depth):
        h = _layer(h, pos_row, w_in[l], conv_w[l], sinks[l], g_attn[l], g_conv[l], w_out[l],
                   ln1_g[l], ln1_b[l], w_gate[l], w_up[l], w_down[l], ln2_g[l], ln2_b[l],
                   seq=s, alpha=alpha)
    return h.reshape(b, s, d)
```

```python
import functools

import jax
import jax.numpy as jnp
from jax import lax
from jax.experimental import pallas as pl
from jax.experimental.pallas import tpu as pltpu

F32 = jnp.float32
BF16 = jnp.bfloat16

HEAD_DIM = 64
N_KV_HEADS = 4
WINDOW = 128
ROT_DIM = HEAD_DIM // 4
ROPE_THETA = 500000.0
CONV_WIDTH = 3
LN_EPS = 1e-5
RMS_EPS = 1e-6

LANES = 128
SUBLANES = 8
BF16_ROWS = 16
VMEM_LIMIT_BYTES = 60000 * 1024

TM_PROJ = 512
TM_ATTN = 512
SUB_ATTN = 256
SUB_MIX = 256
LN_SLAB = 16
TM_FFN = 512
TF_FFN = 512

_TN = (((0,), (0,)), ((), ()))


def _resident(block_shape, index_map):
    return pl.BlockSpec(block_shape, index_map, pipeline_mode=pl.Buffered(1))


def _layer_norm(x, g, b):
    mu = jnp.mean(x, axis=-1, keepdims=True)
    xc = x - mu
    var = jnp.mean(xc * xc, axis=-1, keepdims=True)
    return xc * lax.rsqrt(var + LN_EPS) * g + b


def _rms_norm(x, g):
    ms = jnp.mean(x * x, axis=-1, keepdims=True)
    return x * lax.rsqrt(ms + RMS_EPS) * g


def _zero_after(x):
    bits = pltpu.bitcast(jnp.max(x, axis=(0, 1), keepdims=True), jnp.uint32)
    return pltpu.bitcast((bits >> 16) >> 16, F32)


def _mix_body(x_ref, atn_ref, w_ref, cw_ref, gc_ref, wo_ref, g1_ref, b1_ref, wd_ref,
              h_ref, wdb_ref, zbuf_ref, pre_ref, *, alpha, tiles_per_seq, n_row_tiles):
    tm = x_ref.shape[0]
    aw = atn_ref.shape[0]
    cw = gc_ref.shape[1]
    i = pl.program_id(0)
    live = i < n_row_tiles

    @pl.when(i == 0)
    def _():
        pre_ref[...] = jnp.zeros(pre_ref.shape, F32)

    @pl.when(i % tiles_per_seq == 0)
    def _():
        zbuf_ref[0:SUBLANES, :] = jnp.zeros((SUBLANES, cw), F32)

    def finish_previous():
        top = None
        for r in range(0, tm, LN_SLAB):
            y = _layer_norm(pre_ref[r:r + LN_SLAB, :], g1_ref[...], b1_ref[...])
            h_ref[r:r + LN_SLAB, :] = y
            top = y if top is None else jnp.maximum(top, y)
        return top

    @pl.when(live)
    def _():
        wdb_ref[...] = wd_ref[...].astype(BF16)

        xb = x_ref[...].astype(BF16)

        def proj(part):
            return jnp.dot(xb, w_ref[:, part * cw:(part + 1) * cw], preferred_element_type=F32)

        anchor = _zero_after(finish_previous())
        zbuf_ref[SUBLANES:SUBLANES + tm, :] = proj(0) * proj(2)
        yb = proj(1)
        chunks = [slice(c * SUB_MIX, (c + 1) * SUB_MIX) for c in range(tm // SUB_MIX)]
        attn_mix = [lax.dot_general(atn_ref[:, rows], wo_ref[0:aw, :], _TN, preferred_element_type=F32)
                    for rows in chunks]
        for rows, mix in zip(chunks, attn_mix):
            shifted = lambda back: zbuf_ref[SUBLANES - back + rows.start:SUBLANES - back + rows.stop, :]
            conv = cw_ref[0:1, :] * shifted(2) + cw_ref[1:2, :] * shifted(1) + (cw_ref[2:3, :] * shifted(0) + anchor)
            co = _rms_norm(yb[rows, :] * conv, gc_ref[...]).astype(BF16)
            mix = mix + jnp.dot(co, wo_ref[aw:, :], preferred_element_type=F32)
            pre_ref[rows, :] = alpha * x_ref[rows, :] + mix
        zbuf_ref[0:SUBLANES, :] = zbuf_ref[tm:tm + SUBLANES, :]

    @pl.when(jnp.logical_not(live))
    def _():
        finish_previous()


def _attn_body(sinks_ref, x_ref, pos_ref, invf_ref, wq_ref, gat_ref, wc_ref, wg_ref, wu_ref, wo_ref,
               atn_ref, wcb_ref, wgu_ref, wob_ref,
               q_s, k_s, v_s, at_ref, wqb_s, *, n_q_heads, attn_scale, tiles_per_seq, tf):
    aw, tm = q_s.shape
    kvw = v_s.shape[0]
    wcb_ref[...] = wc_ref[:, wc_ref.shape[1] - wcb_ref.shape[1]:].astype(BF16)
    for j in range(wg_ref.shape[1] // tf):
        src = slice(j * tf, (j + 1) * tf)
        wgu_ref[:, 2 * j * tf:(2 * j + 1) * tf] = wg_ref[:, src].astype(BF16)
        wgu_ref[:, (2 * j + 1) * tf:(2 * j + 2) * tf] = wu_ref[:, src].astype(BF16)
    wob_ref[...] = wo_ref[...].astype(BF16)
    W = WINDOW
    group = n_q_heads // N_KV_HEADS
    assert group == 4 and HEAD_DIM * 2 == LANES
    i = pl.program_id(0)
    prev_bias = jnp.where(i % tiles_per_seq == 0, -jnp.inf, 0.0)

    @pl.when(i == 0)
    def _():
        wqb_s[...] = wq_ref[...].astype(BF16)
        k_s[0:W, :] = jnp.zeros((W, k_s.shape[1]), BF16)
        v_s[:, 0:W] = jnp.zeros((kvw, W), BF16)

    kj = lax.broadcasted_iota(jnp.int32, (W, 2 * W), 0)
    qi = lax.broadcasted_iota(jnp.int32, (W, 2 * W), 1) % W
    use_cur = kj <= qi
    cur_mask = jnp.where(use_cur, 1.0, 0.0).astype(BF16)
    prev_mask = jnp.where(use_cur, 0.0, 1.0).astype(BF16)
    ones_rows = jnp.ones((BF16_ROWS, 2 * W), BF16)
    low = lax.broadcasted_iota(jnp.int32, (SUB_ATTN, LANES), 1) < HEAD_DIM
    half = ROT_DIM // 2

    def project(rows):
        xb = x_ref[rows, :].astype(BF16)
        ang = invf_ref[...] * pos_ref[:, rows].astype(F32)
        cos_t = jnp.cos(ang)
        sin_t = jnp.sin(ang)

        def proj(lo, width):
            return jnp.dot(xb, wqb_s[:, lo:lo + width], preferred_element_type=F32)

        def rope_t(tt, head):
            b0 = head * HEAD_DIM
            t1 = tt[b0:b0 + half, :]
            t2 = tt[b0 + half:b0 + ROT_DIM, :]
            return jnp.concatenate([t1 * cos_t - t2 * sin_t, t2 * cos_t + t1 * sin_t,
                                    tt[b0 + ROT_DIM:b0 + HEAD_DIM, :]], axis=0)

        krows = slice(W + rows.start, W + rows.stop)

        def project_k():
            kt = proj(aw, kvw).T
            k = jnp.concatenate([rope_t(kt, hd) for hd in range(kvw // HEAD_DIM)], axis=0).T
            for c in range(kvw // LANES):
                kr = k[:, c * LANES:(c + 1) * LANES]
                sw = pltpu.roll(kr, HEAD_DIM, 1)
                even = slice(2 * c * LANES, (2 * c + 1) * LANES)
                odd = slice((2 * c + 1) * LANES, (2 * c + 2) * LANES)
                second = lambda blk: slice(2 * kvw + blk.start, 2 * kvw + blk.stop)
                k_s[krows, even] = jnp.where(low, kr, 0.0).astype(BF16)
                k_s[krows, odd] = jnp.where(low, sw, 0.0).astype(BF16)
                k_s[krows, second(even)] = jnp.where(low, 0.0, sw).astype(BF16)
                k_s[krows, second(odd)] = jnp.where(low, 0.0, kr).astype(BF16)

        def project_q():
            qt = proj(0, aw).T
            for hd in range(aw // HEAD_DIM):
                q_s[hd * HEAD_DIM:(hd + 1) * HEAD_DIM, rows] = (rope_t(qt, hd) * attn_scale).astype(BF16)

        def project_v():
            v_s[:, krows] = proj(aw + kvw, kvw).T.astype(BF16)

        return [project_k, project_q, project_v]

    copy_lanes = k_s.shape[1] // 2

    def window_scores(w):
        rows = slice(w * W, (w + 1) * W)
        scores = []
        for g in range(N_KV_HEADS):
            qg = jnp.concatenate([q_s[(2 * g) * LANES:(2 * g + 1) * LANES, rows],
                                  q_s[(2 * g + 1) * LANES:(2 * g + 2) * LANES, rows]], axis=1)
            for off in (0, copy_lanes):
                blk = slice(off + g * LANES, off + (g + 1) * LANES)
                kk = k_s[w * W:(w + 2) * W, blk]
                scores.append(jnp.dot(kk, qg, preferred_element_type=F32))
        return scores

    def window_values(w, scores):
        rows = slice(w * W, (w + 1) * W)
        probs = []
        for n, s in enumerate(scores):
            g, half_idx = divmod(n, 2)
            s_prev = s[0:W, :]
            if w == 0:
                s_prev = s_prev + prev_bias
            merged = jnp.where(use_cur, s[W:2 * W, :], s_prev)
            heads = (group * g + half_idx, group * g + 2 + half_idx)
            sink = jnp.concatenate([jnp.full((1, W), sinks_ref[heads[0]], F32),
                                    jnp.full((1, W), sinks_ref[heads[1]], F32)], axis=1)
            m = jnp.maximum(jnp.max(merged, axis=0, keepdims=True), sink)
            e = jnp.exp(merged - m).astype(BF16)
            ek = jnp.concatenate([e * prev_mask, e * cur_mask], axis=0)
            probs.append((heads, ek, jnp.exp(sink - m)))
        for n, (heads, ek, sink_e) in enumerate(probs):
            g = n // 2
            vt = v_s[g * HEAD_DIM:(g + 1) * HEAD_DIM, w * W:(w + 2) * W]
            vt_aug = jnp.concatenate([vt, ones_rows], axis=0)
            o = jnp.dot(vt_aug, ek, preferred_element_type=F32)
            denom = o[HEAD_DIM:HEAD_DIM + 1, :] + sink_e
            on = o[0:HEAD_DIM, :] * (1.0 / denom)
            for t in range(2):
                hd = heads[t]
                at_ref[hd * HEAD_DIM:(hd + 1) * HEAD_DIM, rows] = on[:, t * W:(t + 1) * W]

    def normalise(rows):
        at = at_ref[:, rows]
        ms = jnp.mean(at * at, axis=0, keepdims=True)
        gat = jnp.concatenate([gat_ref[...]] * ((rows.stop - rows.start) // LANES), axis=1)
        atn_ref[:, rows] = (at * lax.rsqrt(ms + RMS_EPS) * gat).astype(BF16)

    wpc = SUB_ATTN // W
    n_chunks = tm // SUB_ATTN
    chunk_rows = [slice(c * SUB_ATTN, (c + 1) * SUB_ATTN) for c in range(n_chunks)]
    windows = lambda c: list(range(c * wpc, (c + 1) * wpc))
    for part in project(chunk_rows[0]):
        part()
    scores = {w: window_scores(w) for w in windows(0)}
    for c in range(n_chunks):
        parts = project(chunk_rows[c + 1]) if c + 1 < n_chunks else []
        todo = windows(c)
        while parts or todo:
            if parts:
                parts.pop(0)()
            if todo:
                w = todo.pop(0)
                window_values(w, scores.pop(w))
        if c + 1 < n_chunks:
            scores = {w: window_scores(w) for w in windows(c + 1)}
        normalise(chunk_rows[c])

    k_s[0:W, :] = k_s[tm:tm + W, :]
    v_s[:, 0:W] = v_s[:, tm:tm + W]


def _ffn_body(h_ref, wgu_a_ref, wgu_b_ref, wd_a_ref, wd_b_ref, g2_ref, b2_ref, o_ref, hb_ref, acc_ref,
              *, alpha, n_pairs, n_row_tiles):
    i = pl.program_id(0)
    s = pl.program_id(1)
    n_steps = pl.num_programs(1)
    s_eff = jnp.where(i % 2 == 0, s, n_steps - 1 - s)
    tf = wd_a_ref.shape[0]
    live = i < n_row_tiles
    first = s == 0
    pair = s_eff < n_pairs

    @pl.when(jnp.logical_and(first, i == 0))
    def _():
        acc_ref[...] = jnp.zeros(acc_ref.shape, F32)

    @pl.when(jnp.logical_and(first, live))
    def _():
        hb_ref[...] = h_ref[...].astype(BF16)

    def gate_up(w_ref):
        return jnp.dot(hb_ref[...], w_ref[...], preferred_element_type=F32)

    def activation(gu, anchor=None):
        gate = gu[:, 0:tf]
        if anchor is not None:
            gate = gate + anchor
        return (gate * (1.0 / (1.0 + jnp.exp(-gate))) * gu[:, tf:2 * tf]).astype(BF16)

    def finish_previous():
        y = _layer_norm(acc_ref[...], g2_ref[...], b2_ref[...])
        o_ref[...] = y
        return y

    def step_body(pair_step, first_step):
        gus = [gate_up(wgu_a_ref)] + ([gate_up(wgu_b_ref)] if pair_step else [])
        anchors = [None] * len(gus)
        if first_step:
            anchors[-1] = _zero_after(finish_previous())
        acts = [activation(gu, an) for gu, an in zip(gus, anchors)]
        down = jnp.dot(acts[0], wd_a_ref[...], preferred_element_type=F32)
        base = alpha * h_ref[...] if first_step else acc_ref[...]
        acc_ref[...] = base + down
        if pair_step:
            acc_ref[...] += jnp.dot(acts[1], wd_b_ref[...], preferred_element_type=F32)

    for pair_step in (True, False):
        for first_step in (True, False):
            cond = jnp.logical_and(live, jnp.logical_and(pair == pair_step, first == first_step))
            pl.when(cond)(functools.partial(step_body, pair_step, first_step))

    @pl.when(jnp.logical_and(first, jnp.logical_not(live)))
    def _():
        finish_previous()


def _rope_inv_freq():
    inv_freq = ROPE_THETA ** (-jnp.arange(0, ROT_DIM, 2, dtype=F32) / ROT_DIM)
    return inv_freq.reshape(ROT_DIM // 2, 1)


def _layer(h, pos_row, w_in, conv_w, sinks, g_attn, g_conv, w_out, ln1_g, ln1_b,
           w_gate, w_up, w_down, ln2_g, ln2_b, *, seq, alpha):
    m, d = h.shape
    aw = g_attn.shape[0]
    cw = g_conv.shape[0]
    kvw = N_KV_HEADS * HEAD_DIM
    n_q_heads = aw // HEAD_DIM
    d_ff = w_gate.shape[1]
    in_width = w_in.shape[1]
    qkv_width = aw + 2 * kvw
    assert in_width == qkv_width + 3 * cw and d == aw + cw
    assert seq % TM_PROJ == 0 and seq % TM_ATTN == 0 and TM_ATTN % SUB_ATTN == 0 and SUB_ATTN % WINDOW == 0
    assert m % TM_FFN == 0 and d_ff % TF_FFN == 0

    invf = _rope_inv_freq()
    row = lambda a: a.reshape(1, -1).astype(F32)
    cparams = functools.partial(pltpu.CompilerParams, vmem_limit_bytes=VMEM_LIMIT_BYTES)

    tm = TM_ATTN
    tf = TF_FFN
    steps = m // tm
    assert d % (steps * BF16_ROWS) == 0
    slab = d // steps
    const = lambda i: (0, 0)
    by_row = lambda i: (i, 0)
    by_col = lambda i: (0, i)
    g_attn_cols = jnp.broadcast_to(g_attn.astype(F32)[:, None], (aw, LANES))
    atn, w_conv_b, w_gu, w_out_b = pl.pallas_call(
        functools.partial(_attn_body, n_q_heads=n_q_heads, attn_scale=HEAD_DIM ** -0.5,
                          tiles_per_seq=seq // tm, tf=tf),
        grid=(steps,),
        in_specs=[
            pl.BlockSpec(memory_space=pltpu.SMEM),
            pl.BlockSpec((tm, d), by_row),
            pl.BlockSpec((1, tm), by_col),
            _resident((ROT_DIM // 2, 1), const),
            _resident((d, qkv_width), const),
            _resident((aw, LANES), const),
            pl.BlockSpec((slab, in_width), by_row),
            pl.BlockSpec((slab, d_ff), by_row),
            pl.BlockSpec((slab, d_ff), by_row),
            pl.BlockSpec((slab, d), by_row),
        ],
        out_specs=[
            pl.BlockSpec((aw, tm), by_col),
            pl.BlockSpec((slab, 3 * cw), by_row),
            pl.BlockSpec((slab, 2 * d_ff), by_row),
            pl.BlockSpec((slab, d), by_row),
        ],
        out_shape=[
            jax.ShapeDtypeStruct((aw, m), BF16),
            jax.ShapeDtypeStruct((d, 3 * cw), BF16),
            jax.ShapeDtypeStruct((d, 2 * d_ff), BF16),
            jax.ShapeDtypeStruct((d, d), BF16),
        ],
        scratch_shapes=[
            pltpu.VMEM((aw, tm), BF16),
            pltpu.VMEM((WINDOW + tm, 4 * kvw), BF16),
            pltpu.VMEM((kvw, WINDOW + tm), BF16),
            pltpu.VMEM((aw, tm), F32),
            pltpu.VMEM((d, qkv_width), BF16),
        ],
        compiler_params=cparams(dimension_semantics=("arbitrary",)),
        name="qkv_swa",
    )(sinks.astype(F32), h, pos_row, invf, w_in, g_attn_cols, w_in, w_gate, w_up, w_out)

    tm = TM_PROJ
    steps = m // tm
    assert d_ff % (steps * BF16_ROWS) == 0
    slab = d_ff // steps
    held_row = lambda i: (jnp.minimum(i, steps - 1), 0)
    held_col = lambda i: (0, jnp.minimum(i, steps - 1))
    lagged = lambda i: (jnp.maximum(i - 1, 0), 0)
    h1, w_down_b = pl.pallas_call(
        functools.partial(_mix_body, alpha=alpha, tiles_per_seq=seq // tm, n_row_tiles=steps),
        grid=(steps + 1,),
        in_specs=[
            pl.BlockSpec((tm, d), held_row),
            pl.BlockSpec((aw, tm), held_col),
            _resident((d, 3 * cw), const),
            _resident((CONV_WIDTH, cw), const),
            _resident((1, cw), const),
            _resident((d, d), const),
            _resident((1, d), const),
            _resident((1, d), const),
            pl.BlockSpec((slab, d), held_row),
        ],
        out_specs=[pl.BlockSpec((tm, d), lagged), pl.BlockSpec((slab, d), held_row)],
        out_shape=[jax.ShapeDtypeStruct((m, d), F32), jax.ShapeDtypeStruct((d_ff, d), BF16)],
        scratch_shapes=[pltpu.VMEM((tm + 2 * SUBLANES, cw), F32), pltpu.VMEM((tm, d), F32)],
        compiler_params=cparams(dimension_semantics=("arbitrary",)),
        name="conv_outproj_ln",
    )(h, atn, w_conv_b, conv_w.astype(F32), row(g_conv), w_out_b, row(ln1_g), row(ln1_b), w_down)

    tm, tf = TM_FFN, TF_FFN
    n_i, n_j = m // tm, d_ff // tf
    n_pairs = n_j // 2
    n_steps = n_pairs + n_j % 2
    last_step = n_steps - 1 if (n_i - 1) % 2 == 0 else 0

    def step(i, s):
        return jnp.where(i < n_i, jnp.where(i % 2 == 0, s, n_steps - 1 - s), last_step)

    tile_a = lambda i, s: 2 * step(i, s)
    tile_b = lambda i, s: jnp.minimum(2 * step(i, s) + 1, 2 * n_pairs - 1)
    row_i = lambda i, s: (jnp.minimum(i, n_i - 1), 0)
    out_i = lambda i, s: (jnp.clip(jnp.where(s == 0, i - 1, i), 0, n_i - 1), 0)
    out = pl.pallas_call(
        functools.partial(_ffn_body, alpha=alpha, n_pairs=n_pairs, n_row_tiles=n_i),
        grid=(n_i + 1, n_steps),
        in_specs=[
            pl.BlockSpec((tm, d), row_i),
            pl.BlockSpec((d, 2 * tf), lambda i, s: (0, tile_a(i, s))),
            pl.BlockSpec((d, 2 * tf), lambda i, s: (0, tile_b(i, s))),
            pl.BlockSpec((tf, d), lambda i, s: (tile_a(i, s), 0)),
            pl.BlockSpec((tf, d), lambda i, s: (tile_b(i, s), 0)),
            _resident((1, d), lambda i, s: (0, 0)),
            _resident((1, d), lambda i, s: (0, 0)),
        ],
        out_specs=pl.BlockSpec((tm, d), out_i),
        out_shape=jax.ShapeDtypeStruct((m, d), F32),
        scratch_shapes=[pltpu.VMEM((tm, d), BF16), pltpu.VMEM((tm, d), F32)],
        compiler_params=cparams(dimension_semantics=("arbitrary", "arbitrary")),
        name="swiglu_ffn_ln",
    )(h1, w_gu, w_gu, w_down_b, w_down_b, row(ln2_g), row(ln2_b))
    return out


def kernel(x, positions, w_in, conv_w, sinks, g_attn, g_conv, w_out, ln1_g, ln1_b, w_gate, w_up, w_down, ln2_g, ln2_b):
    b, s, d = x.shape
    depth = w_in.shape[0]
    alpha = (2 * depth) ** 0.25
    h = x.reshape(b * s, d)
    pos_row = positions.reshape(1, b * s)
    for l in range(depth):
        h = _layer(h, pos_row, w_in[l], conv_w[l], sinks[l], g_attn[l], g_conv[l], w_out[l],
                   ln1_g[l], ln1_b[l], w_gate[l], w_up[l], w_down[l], ln2_g[l], ln2_b[l],
                   seq=s, alpha=alpha)
    return h.reshape(b, s, d)
```

```python
import functools

import jax
import jax.numpy as jnp
from jax import lax
from jax.experimental import pallas as pl
from jax.experimental.pallas import tpu as pltpu

F32 = jnp.float32
BF16 = jnp.bfloat16

HEAD_DIM = 64
N_KV_HEADS = 4
WINDOW = 128
ROT_DIM = HEAD_DIM // 4
ROPE_THETA = 500000.0
CONV_WIDTH = 3
LN_EPS = 1e-5
RMS_EPS = 1e-6

LANES = 128
SUBLANES = 8
BF16_ROWS = 16
VMEM_LIMIT_BYTES = 60000 * 1024

TM_PROJ = 512
TM_ATTN = 512
SUB_ATTN = 256
SUB_MIX = 256
LN_SLAB = 16
TM_FFN = 512
TF_FFN = 512

_TN = (((0,), (0,)), ((), ()))


def _resident(block_shape, index_map):
    return pl.BlockSpec(block_shape, index_map, pipeline_mode=pl.Buffered(1))


def _layer_norm(x, g, b):
    mu = jnp.mean(x, axis=-1, keepdims=True)
    xc = x - mu
    var = jnp.mean(xc * xc, axis=-1, keepdims=True)
    return xc * lax.rsqrt(var + LN_EPS) * g + b


def _rms_norm(x, g):
    ms = jnp.mean(x * x, axis=-1, keepdims=True)
    return x * lax.rsqrt(ms + RMS_EPS) * g


def _zero_after(x):
    bits = pltpu.bitcast(jnp.max(x, axis=(0, 1), keepdims=True), jnp.uint32)
    return pltpu.bitcast((bits >> 16) >> 16, F32)


def _mix_body(x_ref, atn_ref, w_ref, cw_ref, gc_ref, wo_ref, g1_ref, b1_ref, wd_ref,
              h_ref, wdb_ref, zbuf_ref, pre_ref, *, alpha, tiles_per_seq, n_row_tiles):
    tm = x_ref.shape[0]
    aw = atn_ref.shape[0]
    cw = gc_ref.shape[1]
    i = pl.program_id(0)
    live = i < n_row_tiles

    @pl.when(i == 0)
    def _():
        pre_ref[...] = jnp.zeros(pre_ref.shape, F32)

    @pl.when(i % tiles_per_seq == 0)
    def _():
        zbuf_ref[0:SUBLANES, :] = jnp.zeros((SUBLANES, cw), F32)

    def finish_previous():
        top = None
        for r in range(0, tm, LN_SLAB):
            y = _layer_norm(pre_ref[r:r + LN_SLAB, :], g1_ref[...], b1_ref[...])
            h_ref[r:r + LN_SLAB, :] = y
            top = y if top is None else jnp.maximum(top, y)
        return top

    @pl.when(live)
    def _():
        wdb_ref[...] = wd_ref[...].astype(BF16)

        xb = x_ref[...].astype(BF16)

        def proj(part):
            return jnp.dot(xb, w_ref[:, part * cw:(part + 1) * cw], preferred_element_type=F32)

        anchor = _zero_after(finish_previous())
        zbuf_ref[SUBLANES:SUBLANES + tm, :] = proj(0) * proj(2)
        yb = proj(1)
        chunks = [slice(c * SUB_MIX, (c + 1) * SUB_MIX) for c in range(tm // SUB_MIX)]
        attn_mix = [lax.dot_general(atn_ref[:, rows], wo_ref[0:aw, :], _TN, preferred_element_type=F32)
                    for rows in chunks]
        for rows, mix in zip(chunks, attn_mix):
            slabs = []
            for r in range(rows.start, rows.stop, 2 * LN_SLAB):
                shifted = lambda back: zbuf_ref[SUBLANES - back + r:SUBLANES - back + r + 2 * LN_SLAB, :]
                conv = (cw_ref[0:1, :] * shifted(2) + cw_ref[1:2, :] * shifted(1)
                        + (cw_ref[2:3, :] * shifted(0) + anchor))
                slabs.append(_rms_norm(yb[r:r + 2 * LN_SLAB, :] * conv, gc_ref[...]).astype(BF16))
            co = jnp.concatenate(slabs, axis=0)
            mix = mix + jnp.dot(co, wo_ref[aw:, :], preferred_element_type=F32)
            pre_ref[rows, :] = alpha * x_ref[rows, :] + mix
        zbuf_ref[0:SUBLANES, :] = zbuf_ref[tm:tm + SUBLANES, :]

    @pl.when(jnp.logical_not(live))
    def _():
        finish_previous()


def _attn_body(sinks_ref, x_ref, pos_ref, invf_ref, wq_ref, gat_ref, wc_ref, wg_ref, wu_ref, wo_ref,
               atn_ref, wcb_ref, wgu_ref, wob_ref,
               q_s, k_s, v_s, at_ref, wqb_s, *, n_q_heads, attn_scale, tiles_per_seq, tf):
    aw, tm = q_s.shape
    kvw = v_s.shape[0]
    wcb_ref[...] = wc_ref[:, wc_ref.shape[1] - wcb_ref.shape[1]:].astype(BF16)
    for j in range(wg_ref.shape[1] // tf):
        src = slice(j * tf, (j + 1) * tf)
        wgu_ref[:, 2 * j * tf:(2 * j + 1) * tf] = wg_ref[:, src].astype(BF16)
        wgu_ref[:, (2 * j + 1) * tf:(2 * j + 2) * tf] = wu_ref[:, src].astype(BF16)
    wob_ref[...] = wo_ref[...].astype(BF16)
    W = WINDOW
    group = n_q_heads // N_KV_HEADS
    assert group == 4 and HEAD_DIM * 2 == LANES
    i = pl.program_id(0)
    prev_bias = jnp.where(i % tiles_per_seq == 0, -jnp.inf, 0.0)

    @pl.when(i == 0)
    def _():
        wqb_s[...] = wq_ref[...].astype(BF16)
        k_s[0:W, :] = jnp.zeros((W, k_s.shape[1]), BF16)
        v_s[:, 0:W] = jnp.zeros((kvw, W), BF16)

    kj = lax.broadcasted_iota(jnp.int32, (W, 2 * W), 0)
    qi = lax.broadcasted_iota(jnp.int32, (W, 2 * W), 1) % W
    use_cur = kj <= qi
    cur_mask = jnp.where(use_cur, 1.0, 0.0).astype(BF16)
    prev_mask = jnp.where(use_cur, 0.0, 1.0).astype(BF16)
    ones_rows = jnp.ones((BF16_ROWS, 2 * W), BF16)
    low = lax.broadcasted_iota(jnp.int32, (SUB_ATTN, LANES), 1) < HEAD_DIM
    half = ROT_DIM // 2

    def project(rows):
        xb = x_ref[rows, :].astype(BF16)
        ang = invf_ref[...] * pos_ref[:, rows].astype(F32)
        cos_t = jnp.cos(ang)
        sin_t = jnp.sin(ang)

        def proj(lo, width):
            return jnp.dot(xb, wqb_s[:, lo:lo + width], preferred_element_type=F32)

        def rope_t(tt, head):
            b0 = head * HEAD_DIM
            t1 = tt[b0:b0 + half, :]
            t2 = tt[b0 + half:b0 + ROT_DIM, :]
            return jnp.concatenate([t1 * cos_t - t2 * sin_t, t2 * cos_t + t1 * sin_t,
                                    tt[b0 + ROT_DIM:b0 + HEAD_DIM, :]], axis=0)

        krows = slice(W + rows.start, W + rows.stop)

        def project_k():
            kt = proj(aw, kvw).T
            k = jnp.concatenate([rope_t(kt, hd) for hd in range(kvw // HEAD_DIM)], axis=0).T
            for c in range(kvw // LANES):
                kr = k[:, c * LANES:(c + 1) * LANES]
                sw = pltpu.roll(kr, HEAD_DIM, 1)
                even = slice(2 * c * LANES, (2 * c + 1) * LANES)
                odd = slice((2 * c + 1) * LANES, (2 * c + 2) * LANES)
                second = lambda blk: slice(2 * kvw + blk.start, 2 * kvw + blk.stop)
                k_s[krows, even] = jnp.where(low, kr, 0.0).astype(BF16)
                k_s[krows, odd] = jnp.where(low, sw, 0.0).astype(BF16)
                k_s[krows, second(even)] = jnp.where(low, 0.0, sw).astype(BF16)
                k_s[krows, second(odd)] = jnp.where(low, 0.0, kr).astype(BF16)

        def project_q():
            qt = proj(0, aw).T
            for hd in range(aw // HEAD_DIM):
                q_s[hd * HEAD_DIM:(hd + 1) * HEAD_DIM, rows] = (rope_t(qt, hd) * attn_scale).astype(BF16)

        def project_v():
            v_s[:, krows] = proj(aw + kvw, kvw).T.astype(BF16)

        return [project_k, project_q, project_v]

    copy_lanes = k_s.shape[1] // 2

    def window_scores(w):
        rows = slice(w * W, (w + 1) * W)
        scores = []
        for g in range(N_KV_HEADS):
            qg = jnp.concatenate([q_s[(2 * g) * LANES:(2 * g + 1) * LANES, rows],
                                  q_s[(2 * g + 1) * LANES:(2 * g + 2) * LANES, rows]], axis=1)
            for off in (0, copy_lanes):
                blk = slice(off + g * LANES, off + (g + 1) * LANES)
                kk = k_s[w * W:(w + 2) * W, blk]
                scores.append(jnp.dot(kk, qg, preferred_element_type=F32))
        return scores

    def window_values(w, scores):
        rows = slice(w * W, (w + 1) * W)
        probs = []
        for n, s in enumerate(scores):
            g, half_idx = divmod(n, 2)
            s_prev = s[0:W, :]
            if w == 0:
                s_prev = s_prev + prev_bias
            merged = jnp.where(use_cur, s[W:2 * W, :], s_prev)
            heads = (group * g + half_idx, group * g + 2 + half_idx)
            sink = jnp.concatenate([jnp.full((1, W), sinks_ref[heads[0]], F32),
                                    jnp.full((1, W), sinks_ref[heads[1]], F32)], axis=1)
            m = jnp.maximum(jnp.max(merged, axis=0, keepdims=True), sink)
            e = jnp.exp(merged - m).astype(BF16)
            ek = jnp.concatenate([e * prev_mask, e * cur_mask], axis=0)
            probs.append((heads, ek, jnp.exp(sink - m)))
        for n, (heads, ek, sink_e) in enumerate(probs):
            g = n // 2
            vt = v_s[g * HEAD_DIM:(g + 1) * HEAD_DIM, w * W:(w + 2) * W]
            vt_aug = jnp.concatenate([vt, ones_rows], axis=0)
            o = jnp.dot(vt_aug, ek, preferred_element_type=F32)
            denom = o[HEAD_DIM:HEAD_DIM + 1, :] + sink_e
            on = o[0:HEAD_DIM, :] * (1.0 / denom)
            for t in range(2):
                hd = heads[t]
                at_ref[hd * HEAD_DIM:(hd + 1) * HEAD_DIM, rows] = on[:, t * W:(t + 1) * W]

    def normalise(rows):
        at = at_ref[:, rows]
        ms = jnp.mean(at * at, axis=0, keepdims=True)
        gat = jnp.concatenate([gat_ref[...]] * ((rows.stop - rows.start) // LANES), axis=1)
        atn_ref[:, rows] = (at * lax.rsqrt(ms + RMS_EPS) * gat).astype(BF16)

    wpc = SUB_ATTN // W
    n_chunks = tm // SUB_ATTN
    chunk_rows = [slice(c * SUB_ATTN, (c + 1) * SUB_ATTN) for c in range(n_chunks)]
    windows = lambda c: list(range(c * wpc, (c + 1) * wpc))
    for part in project(chunk_rows[0]):
        part()
    scores = {w: window_scores(w) for w in windows(0)}
    for c in range(n_chunks):
        parts = project(chunk_rows[c + 1]) if c + 1 < n_chunks else []
        todo = windows(c)
        while parts or todo:
            if parts:
                parts.pop(0)()
            if todo:
                w = todo.pop(0)
                window_values(w, scores.pop(w))
        if c + 1 < n_chunks:
            scores = {w: window_scores(w) for w in windows(c + 1)}
        normalise(chunk_rows[c])

    k_s[0:W, :] = k_s[tm:tm + W, :]
    v_s[:, 0:W] = v_s[:, tm:tm + W]


def _ffn_body(h_ref, wgu_a_ref, wgu_b_ref, wd_a_ref, wd_b_ref, g2_ref, b2_ref, o_ref, hb_ref, acc_ref,
              *, alpha, n_pairs, n_row_tiles):
    i = pl.program_id(0)
    s = pl.program_id(1)
    n_steps = pl.num_programs(1)
    s_eff = jnp.where(i % 2 == 0, s, n_steps - 1 - s)
    tf = wd_a_ref.shape[0]
    live = i < n_row_tiles
    first = s == 0
    pair = s_eff < n_pairs

    @pl.when(jnp.logical_and(first, i == 0))
    def _():
        acc_ref[...] = jnp.zeros(acc_ref.shape, F32)

    @pl.when(jnp.logical_and(first, live))
    def _():
        hb_ref[...] = h_ref[...].astype(BF16)

    def gate_up(w_ref):
        return jnp.dot(hb_ref[...], w_ref[...], preferred_element_type=F32)

    def activation(gu, anchor=None):
        gate = gu[:, 0:tf]
        if anchor is not None:
            gate = gate + anchor
        return (gate * (1.0 / (1.0 + jnp.exp(-gate))) * gu[:, tf:2 * tf]).astype(BF16)

    def finish_previous():
        top = None
        for r in range(0, acc_ref.shape[0], LN_SLAB):
            y = _layer_norm(acc_ref[r:r + LN_SLAB, :], g2_ref[...], b2_ref[...])
            o_ref[r:r + LN_SLAB, :] = y
            top = y if top is None else jnp.maximum(top, y)
        return top

    def step_body(pair_step, first_step):
        gus = [gate_up(wgu_a_ref)] + ([gate_up(wgu_b_ref)] if pair_step else [])
        anchors = [None] * len(gus)
        if first_step:
            anchors[-1] = _zero_after(finish_previous())
        acts = [activation(gu, an) for gu, an in zip(gus, anchors)]
        down = jnp.dot(acts[0], wd_a_ref[...], preferred_element_type=F32)
        base = alpha * h_ref[...] if first_step else acc_ref[...]
        acc_ref[...] = base + down
        if pair_step:
            acc_ref[...] += jnp.dot(acts[1], wd_b_ref[...], preferred_element_type=F32)

    for pair_step in (True, False):
        for first_step in (True, False):
            cond = jnp.logical_and(live, jnp.logical_and(pair == pair_step, first == first_step))
            pl.when(cond)(functools.partial(step_body, pair_step, first_step))

    @pl.when(jnp.logical_and(first, jnp.logical_not(live)))
    def _():
        finish_previous()


def _rope_inv_freq():
    inv_freq = ROPE_THETA ** (-jnp.arange(0, ROT_DIM, 2, dtype=F32) / ROT_DIM)
    return inv_freq.reshape(ROT_DIM // 2, 1)


def _layer(h, pos_row, w_in, conv_w, sinks, g_attn, g_conv, w_out, ln1_g, ln1_b,
           w_gate, w_up, w_down, ln2_g, ln2_b, *, seq, alpha):
    m, d = h.shape
    aw = g_attn.shape[0]
    cw = g_conv.shape[0]
    kvw = N_KV_HEADS * HEAD_DIM
    n_q_heads = aw // HEAD_DIM
    d_ff = w_gate.shape[1]
    in_width = w_in.shape[1]
    qkv_width = aw + 2 * kvw
    assert in_width == qkv_width + 3 * cw and d == aw + cw
    assert seq % TM_PROJ == 0 and seq % TM_ATTN == 0 and TM_ATTN % SUB_ATTN == 0 and SUB_ATTN % WINDOW == 0
    assert m % TM_FFN == 0 and d_ff % TF_FFN == 0

    invf = _rope_inv_freq()
    row = lambda a: a.reshape(1, -1).astype(F32)
    cparams = functools.partial(pltpu.CompilerParams, vmem_limit_bytes=VMEM_LIMIT_BYTES)

    tm = TM_ATTN
    tf = TF_FFN
    steps = m // tm
    assert d % (steps * BF16_ROWS) == 0
    slab = d // steps
    const = lambda i: (0, 0)
    by_row = lambda i: (i, 0)
    by_col = lambda i: (0, i)
    g_attn_cols = jnp.broadcast_to(g_attn.astype(F32)[:, None], (aw, LANES))
    atn, w_conv_b, w_gu, w_out_b = pl.pallas_call(
        functools.partial(_attn_body, n_q_heads=n_q_heads, attn_scale=HEAD_DIM ** -0.5,
                          tiles_per_seq=seq // tm, tf=tf),
        grid=(steps,),
        in_specs=[
            pl.BlockSpec(memory_space=pltpu.SMEM),
            pl.BlockSpec((tm, d), by_row),
            pl.BlockSpec((1, tm), by_col),
            _resident((ROT_DIM // 2, 1), const),
            _resident((d, qkv_width), const),
            _resident((aw, LANES), const),
            pl.BlockSpec((slab, in_width), by_row),
            pl.BlockSpec((slab, d_ff), by_row),
            pl.BlockSpec((slab, d_ff), by_row),
            pl.BlockSpec((slab, d), by_row),
        ],
        out_specs=[
            pl.BlockSpec((aw, tm), by_col),
            pl.BlockSpec((slab, 3 * cw), by_row),
            pl.BlockSpec((slab, 2 * d_ff), by_row),
            pl.BlockSpec((slab, d), by_row),
        ],
        out_shape=[
            jax.ShapeDtypeStruct((aw, m), BF16),
            jax.ShapeDtypeStruct((d, 3 * cw), BF16),
            jax.ShapeDtypeStruct((d, 2 * d_ff), BF16),
            jax.ShapeDtypeStruct((d, d), BF16),
        ],
        scratch_shapes=[
            pltpu.VMEM((aw, tm), BF16),
            pltpu.VMEM((WINDOW + tm, 4 * kvw), BF16),
            pltpu.VMEM((kvw, WINDOW + tm), BF16),
            pltpu.VMEM((aw, tm), F32),
            pltpu.VMEM((d, qkv_width), BF16),
        ],
        compiler_params=cparams(dimension_semantics=("arbitrary",)),
        name="qkv_swa",
    )(sinks.astype(F32), h, pos_row, invf, w_in, g_attn_cols, w_in, w_gate, w_up, w_out)

    tm = TM_PROJ
    steps = m // tm
    assert d_ff % (steps * BF16_ROWS) == 0
    slab = d_ff // steps
    held_row = lambda i: (jnp.minimum(i, steps - 1), 0)
    held_col = lambda i: (0, jnp.minimum(i, steps - 1))
    lagged = lambda i: (jnp.maximum(i - 1, 0), 0)
    h1, w_down_b = pl.pallas_call(
        functools.partial(_mix_body, alpha=alpha, tiles_per_seq=seq // tm, n_row_tiles=steps),
        grid=(steps + 1,),
        in_specs=[
            pl.BlockSpec((tm, d), held_row),
            pl.BlockSpec((aw, tm), held_col),
            _resident((d, 3 * cw), const),
            _resident((CONV_WIDTH, cw), const),
            _resident((1, cw), const),
            _resident((d, d), const),
            _resident((1, d), const),
            _resident((1, d), const),
            pl.BlockSpec((slab, d), held_row),
        ],
        out_specs=[pl.BlockSpec((tm, d), lagged), pl.BlockSpec((slab, d), held_row)],
        out_shape=[jax.ShapeDtypeStruct((m, d), F32), jax.ShapeDtypeStruct((d_ff, d), BF16)],
        scratch_shapes=[pltpu.VMEM((tm + 2 * SUBLANES, cw), F32), pltpu.VMEM((tm, d), F32)],
        compiler_params=cparams(dimension_semantics=("arbitrary",)),
        name="conv_outproj_ln",
    )(h, atn, w_conv_b, conv_w.astype(F32), row(g_conv), w_out_b, row(ln1_g), row(ln1_b), w_down)

    tm, tf = TM_FFN, TF_FFN
    n_i, n_j = m // tm, d_ff // tf
    n_pairs = n_j // 2
    n_steps = n_pairs + n_j % 2
    last_step = n_steps - 1 if (n_i - 1) % 2 == 0 else 0

    def step(i, s):
        return jnp.where(i < n_i, jnp.where(i % 2 == 0, s, n_steps - 1 - s), last_step)

    tile_a = lambda i, s: 2 * step(i, s)
    tile_b = lambda i, s: jnp.minimum(2 * step(i, s) + 1, 2 * n_pairs - 1)
    row_i = lambda i, s: (jnp.minimum(i, n_i - 1), 0)
    out_i = lambda i, s: (jnp.clip(jnp.where(s == 0, i - 1, i), 0, n_i - 1), 0)
    out = pl.pallas_call(
        functools.partial(_ffn_body, alpha=alpha, n_pairs=n_pairs, n_row_tiles=n_i),
        grid=(n_i + 1, n_steps),
        in_specs=[
            pl.BlockSpec((tm, d), row_i),
            pl.BlockSpec((d, 2 * tf), lambda i, s: (0, tile_a(i, s))),
            pl.BlockSpec((d, 2 * tf), lambda i, s: (0, tile_b(i, s))),
            pl.BlockSpec((tf, d), lambda i, s: (tile_a(i, s), 0)),
            pl.BlockSpec((tf, d), lambda i, s: (tile_b(i, s), 0)),
            _resident((1, d), lambda i, s: (0, 0)),
            _resident((1, d), lambda i, s: (0, 0)),
        ],
        out_specs=pl.BlockSpec((tm, d), out_i),
        out_shape=jax.ShapeDtypeStruct((m, d), F32),
        scratch_shapes=[pltpu.VMEM((tm, d), BF16), pltpu.VMEM((tm, d), F32)],
        compiler_params=cparams(dimension_semantics=("arbitrary", "arbitrary")),
        name="swiglu_ffn_ln",
    )(h1, w_gu, w_gu, w_down_b, w_down_b, row(ln2_g), row(ln2_b))
    return out


def kernel(x, positions, w_in, conv_w, sinks, g_attn, g_conv, w_out, ln1_g, ln1_b, w_gate, w_up, w_down, ln2_g, ln2_b):
    b, s, d = x.shape
    depth = w_in.shape[0]
    alpha = (2 * depth) ** 0.25
    h = x.reshape(b * s, d)
    pos_row = positions.reshape(1, b * s)
    for l in range(depth):
        h = _layer(h, pos_row, w_in[l], conv_w[l], sinks[l], g_attn[l], g_conv[l], w_out[l],
                   ln1_g[l], ln1_b[l], w_gate[l], w_up[l], w_down[l], ln2_g[l], ln2_b[l],
                   seq=s, alpha=alpha)
    return h.reshape(b, s, d)
```

```python
import functools

import jax
import jax.numpy as jnp
from jax import lax
from jax.experimental import pallas as pl
from jax.experimental.pallas import tpu as pltpu

F32 = jnp.float32
BF16 = jnp.bfloat16

HEAD_DIM = 64
N_KV_HEADS = 4
WINDOW = 128
ROT_DIM = HEAD_DIM // 4
ROPE_THETA = 500000.0
CONV_WIDTH = 3
LN_EPS = 1e-5
RMS_EPS = 1e-6

LANES = 128
SUBLANES = 8
BF16_ROWS = 16
VMEM_LIMIT_BYTES = 60000 * 1024

TM_PROJ = 512
TM_ATTN = 512
SUB_ATTN = 256
SUB_MIX = 256
LN_SLAB = 16
TM_FFN = 512
TF_FFN = 512

_TN = (((0,), (0,)), ((), ()))


def _resident(block_shape, index_map):
    return pl.BlockSpec(block_shape, index_map, pipeline_mode=pl.Buffered(1))


def _layer_norm(x, g, b):
    mu = jnp.mean(x, axis=-1, keepdims=True)
    xc = x - mu
    var = jnp.mean(xc * xc, axis=-1, keepdims=True)
    return xc * lax.rsqrt(var + LN_EPS) * g + b


def _rms_norm(x, g):
    ms = jnp.mean(x * x, axis=-1, keepdims=True)
    return x * lax.rsqrt(ms + RMS_EPS) * g


def _zero_after(x):
    bits = pltpu.bitcast(jnp.max(x, axis=(0, 1), keepdims=True), jnp.uint32)
    return pltpu.bitcast((bits >> 16) >> 16, F32)


def _mix_body(x_ref, atn_ref, w_ref, cw_ref, gc_ref, wo_ref, g1_ref, b1_ref, wd_ref,
              h_ref, wdb_ref, zbuf_ref, pre_ref, *, alpha, tiles_per_seq, n_row_tiles):
    tm = x_ref.shape[0]
    aw = atn_ref.shape[0]
    cw = gc_ref.shape[1]
    i = pl.program_id(0)
    live = i < n_row_tiles

    @pl.when(i == 0)
    def _():
        pre_ref[...] = jnp.zeros(pre_ref.shape, F32)

    @pl.when(i % tiles_per_seq == 0)
    def _():
        zbuf_ref[0:SUBLANES, :] = jnp.zeros((SUBLANES, cw), F32)

    def finish_previous():
        top = None
        for r in range(0, tm, LN_SLAB):
            y = _layer_norm(pre_ref[r:r + LN_SLAB, :], g1_ref[...], b1_ref[...])
            h_ref[r:r + LN_SLAB, :] = y
            top = y if top is None else jnp.maximum(top, y)
        return top

    @pl.when(live)
    def _():
        wdb_ref[...] = wd_ref[...].astype(BF16)

        xb = x_ref[...].astype(BF16)

        def proj(part):
            return jnp.dot(xb, w_ref[:, part * cw:(part + 1) * cw], preferred_element_type=F32)

        anchor = _zero_after(finish_previous())
        zbuf_ref[SUBLANES:SUBLANES + tm, :] = proj(0) * proj(2)
        yb = proj(1)
        chunks = [slice(c * SUB_MIX, (c + 1) * SUB_MIX) for c in range(tm // SUB_MIX)]
        attn_mix = [lax.dot_general(atn_ref[:, rows], wo_ref[0:aw, :], _TN, preferred_element_type=F32)
                    for rows in chunks]
        for rows, mix in zip(chunks, attn_mix):
            slabs = []
            for r in range(rows.start, rows.stop, 2 * LN_SLAB):
                shifted = lambda back: zbuf_ref[SUBLANES - back + r:SUBLANES - back + r + 2 * LN_SLAB, :]
                conv = (cw_ref[0:1, :] * shifted(2) + cw_ref[1:2, :] * shifted(1)
                        + (cw_ref[2:3, :] * shifted(0) + anchor))
                slabs.append(_rms_norm(yb[r:r + 2 * LN_SLAB, :] * conv, gc_ref[...]).astype(BF16))
            co = jnp.concatenate(slabs, axis=0)
            mix = mix + jnp.dot(co, wo_ref[aw:, :], preferred_element_type=F32)
            pre_ref[rows, :] = alpha * x_ref[rows, :] + mix
        zbuf_ref[0:SUBLANES, :] = zbuf_ref[tm:tm + SUBLANES, :]

    @pl.when(jnp.logical_not(live))
    def _():
        finish_previous()


def _attn_body(sinks_ref, x_ref, pos_ref, invf_ref, wq_ref, gat_ref, wc_ref, wg_ref, wu_ref, wo_ref,
               atn_ref, wcb_ref, wgu_ref, wob_ref,
               q_s, k_s, v_s, at_ref, wqb_s, *, n_q_heads, attn_scale, tiles_per_seq, tf):
    aw, tm = q_s.shape
    kvw = v_s.shape[0]
    wcb_ref[...] = wc_ref[:, wc_ref.shape[1] - wcb_ref.shape[1]:].astype(BF16)
    for j in range(wg_ref.shape[1] // tf):
        src = slice(j * tf, (j + 1) * tf)
        wgu_ref[:, 2 * j * tf:(2 * j + 1) * tf] = wg_ref[:, src].astype(BF16)
        wgu_ref[:, (2 * j + 1) * tf:(2 * j + 2) * tf] = wu_ref[:, src].astype(BF16)
    wob_ref[...] = wo_ref[...].astype(BF16)
    W = WINDOW
    group = n_q_heads // N_KV_HEADS
    assert group == 4 and HEAD_DIM * 2 == LANES
    i = pl.program_id(0)
    prev_bias = jnp.where(i % tiles_per_seq == 0, -jnp.inf, 0.0)

    @pl.when(i == 0)
    def _():
        wqb_s[...] = wq_ref[...].astype(BF16)
        k_s[0:W, :] = jnp.zeros((W, k_s.shape[1]), BF16)
        v_s[:, 0:W] = jnp.zeros((kvw, W), BF16)

    kj = lax.broadcasted_iota(jnp.int32, (W, 2 * W), 0)
    qi = lax.broadcasted_iota(jnp.int32, (W, 2 * W), 1) % W
    use_cur = kj <= qi
    cur_mask = jnp.where(use_cur, 1.0, 0.0).astype(BF16)
    prev_mask = jnp.where(use_cur, 0.0, 1.0).astype(BF16)
    ones_rows = jnp.ones((BF16_ROWS, 2 * W), BF16)
    low = lax.broadcasted_iota(jnp.int32, (SUB_ATTN, LANES), 1) < HEAD_DIM
    half = ROT_DIM // 2

    def project(rows):
        xb = x_ref[rows, :].astype(BF16)
        ang = invf_ref[...] * pos_ref[:, rows].astype(F32)
        cos_t = jnp.cos(ang)
        sin_t = jnp.sin(ang)

        def proj(lo, width):
            return jnp.dot(xb, wqb_s[:, lo:lo + width], preferred_element_type=F32)

        def rope_t(tt, head):
            b0 = head * HEAD_DIM
            t1 = tt[b0:b0 + half, :]
            t2 = tt[b0 + half:b0 + ROT_DIM, :]
            return jnp.concatenate([t1 * cos_t - t2 * sin_t, t2 * cos_t + t1 * sin_t,
                                    tt[b0 + ROT_DIM:b0 + HEAD_DIM, :]], axis=0)

        krows = slice(W + rows.start, W + rows.stop)

        def project_k():
            kt = proj(aw, kvw).T
            k = jnp.concatenate([rope_t(kt, hd) for hd in range(kvw // HEAD_DIM)], axis=0).T
            for c in range(kvw // LANES):
                kr = k[:, c * LANES:(c + 1) * LANES]
                sw = pltpu.roll(kr, HEAD_DIM, 1)
                even = slice(2 * c * LANES, (2 * c + 1) * LANES)
                odd = slice((2 * c + 1) * LANES, (2 * c + 2) * LANES)
                second = lambda blk: slice(2 * kvw + blk.start, 2 * kvw + blk.stop)
                k_s[krows, even] = jnp.where(low, kr, 0.0).astype(BF16)
                k_s[krows, odd] = jnp.where(low, sw, 0.0).astype(BF16)
                k_s[krows, second(even)] = jnp.where(low, 0.0, sw).astype(BF16)
                k_s[krows, second(odd)] = jnp.where(low, 0.0, kr).astype(BF16)

        def project_q():
            qt = proj(0, aw).T
            for hd in range(aw // HEAD_DIM):
                q_s[hd * HEAD_DIM:(hd + 1) * HEAD_DIM, rows] = (rope_t(qt, hd) * attn_scale).astype(BF16)

        def project_v():
            v_s[:, krows] = proj(aw + kvw, kvw).T.astype(BF16)

        return [project_k, project_q, project_v]

    copy_lanes = k_s.shape[1] // 2

    def window_scores(w):
        rows = slice(w * W, (w + 1) * W)
        scores = []
        for g in range(N_KV_HEADS):
            qg = jnp.concatenate([q_s[(2 * g) * LANES:(2 * g + 1) * LANES, rows],
                                  q_s[(2 * g + 1) * LANES:(2 * g + 2) * LANES, rows]], axis=1)
            for off in (0, copy_lanes):
                blk = slice(off + g * LANES, off + (g + 1) * LANES)
                kk = k_s[w * W:(w + 2) * W, blk]
                scores.append(jnp.dot(kk, qg, preferred_element_type=F32))
        return scores

    def window_values(w, scores):
        rows = slice(w * W, (w + 1) * W)
        probs = []
        for n, s in enumerate(scores):
            g, half_idx = divmod(n, 2)
            s_prev = s[0:W, :]
            if w == 0:
                s_prev = s_prev + prev_bias
            merged = jnp.where(use_cur, s[W:2 * W, :], s_prev)
            heads = (group * g + half_idx, group * g + 2 + half_idx)
            sink = jnp.concatenate([jnp.full((1, W), sinks_ref[heads[0]], F32),
                                    jnp.full((1, W), sinks_ref[heads[1]], F32)], axis=1)
            m = jnp.maximum(jnp.max(merged, axis=0, keepdims=True), sink)
            e = jnp.exp(merged - m).astype(BF16)
            ek = jnp.concatenate([e * prev_mask, e * cur_mask], axis=0)
            probs.append((heads, ek, jnp.exp(sink - m)))
        for n, (heads, ek, sink_e) in enumerate(probs):
            g = n // 2
            vt = v_s[g * HEAD_DIM:(g + 1) * HEAD_DIM, w * W:(w + 2) * W]
            vt_aug = jnp.concatenate([vt, ones_rows], axis=0)
            o = jnp.dot(vt_aug, ek, preferred_element_type=F32)
            denom = o[HEAD_DIM:HEAD_DIM + 1, :] + sink_e
            on = o[0:HEAD_DIM, :] * (1.0 / denom)
            for t in range(2):
                hd = heads[t]
                at_ref[hd * HEAD_DIM:(hd + 1) * HEAD_DIM, rows] = on[:, t * W:(t + 1) * W]

    def normalise(rows):
        at = at_ref[:, rows]
        ms = jnp.mean(at * at, axis=0, keepdims=True)
        gat = jnp.concatenate([gat_ref[...]] * ((rows.stop - rows.start) // LANES), axis=1)
        atn_ref[:, rows] = (at * lax.rsqrt(ms + RMS_EPS) * gat).astype(BF16)

    wpc = SUB_ATTN // W
    n_chunks = tm // SUB_ATTN
    chunk_rows = [slice(c * SUB_ATTN, (c + 1) * SUB_ATTN) for c in range(n_chunks)]
    windows = lambda c: list(range(c * wpc, (c + 1) * wpc))
    for part in project(chunk_rows[0]):
        part()
    scores = {w: window_scores(w) for w in windows(0)}
    for c in range(n_chunks):
        parts = project(chunk_rows[c + 1]) if c + 1 < n_chunks else []
        todo = windows(c)
        while parts or todo:
            if parts:
                parts.pop(0)()
            if todo:
                w = todo.pop(0)
                window_values(w, scores.pop(w))
        if c + 1 < n_chunks:
            scores = {w: window_scores(w) for w in windows(c + 1)}
        normalise(chunk_rows[c])

    k_s[0:W, :] = k_s[tm:tm + W, :]
    v_s[:, 0:W] = v_s[:, tm:tm + W]


def _ffn_body(h_ref, wgu_a_ref, wgu_b_ref, wd_a_ref, wd_b_ref, g2_ref, b2_ref, o_ref, hb_ref, acc_ref,
              *, alpha, n_pairs, n_row_tiles):
    i = pl.program_id(0)
    s = pl.program_id(1)
    n_steps = pl.num_programs(1)
    s_eff = jnp.where(i % 2 == 0, s, n_steps - 1 - s)
    tf = wd_a_ref.shape[0]
    live = i < n_row_tiles
    first = s == 0
    pair = s_eff < n_pairs

    @pl.when(jnp.logical_and(first, i == 0))
    def _():
        acc_ref[...] = jnp.zeros(acc_ref.shape, F32)

    @pl.when(jnp.logical_and(first, live))
    def _():
        hb_ref[...] = h_ref[...].astype(BF16)

    def gate_up(w_ref):
        return jnp.dot(hb_ref[...], w_ref[...], preferred_element_type=F32)

    def activation(gu, anchor=None, on_gate=True):
        gate = gu[:, 0:tf]
        if anchor is not None and on_gate:
            gate = gate + anchor
        act = (gate * (1.0 / (1.0 + jnp.exp(-gate))) * gu[:, tf:2 * tf]).astype(BF16)
        if anchor is not None and not on_gate:
            act = act + anchor.astype(BF16)
        return act

    def finish_previous():
        top = None
        for r in range(0, acc_ref.shape[0], LN_SLAB):
            y = _layer_norm(acc_ref[r:r + LN_SLAB, :], g2_ref[...], b2_ref[...])
            o_ref[r:r + LN_SLAB, :] = y
            top = y if top is None else jnp.maximum(top, y)
        return top

    def step_body(pair_step, first_step):
        gus = [gate_up(wgu_a_ref)] + ([gate_up(wgu_b_ref)] if pair_step else [])
        anchors = [None] * len(gus)
        if first_step:
            anchors[-1] = _zero_after(finish_previous())
        acts = [activation(gu, an, on_gate=pair_step) for gu, an in zip(gus, anchors)]
        down = jnp.dot(acts[0], wd_a_ref[...], preferred_element_type=F32)
        base = alpha * h_ref[...] if first_step else acc_ref[...]
        acc_ref[...] = base + down
        if pair_step:
            acc_ref[...] += jnp.dot(acts[1], wd_b_ref[...], preferred_element_type=F32)

    for pair_step in (True, False):
        for first_step in (True, False):
            cond = jnp.logical_and(live, jnp.logical_and(pair == pair_step, first == first_step))
            pl.when(cond)(functools.partial(step_body, pair_step, first_step))

    @pl.when(jnp.logical_and(first, jnp.logical_not(live)))
    def _():
        finish_previous()


def _rope_inv_freq():
    inv_freq = ROPE_THETA ** (-jnp.arange(0, ROT_DIM, 2, dtype=F32) / ROT_DIM)
    return inv_freq.reshape(ROT_DIM // 2, 1)


def _layer(h, pos_row, w_in, conv_w, sinks, g_attn, g_conv, w_out, ln1_g, ln1_b,
           w_gate, w_up, w_down, ln2_g, ln2_b, *, seq, alpha):
    m, d = h.shape
    aw = g_attn.shape[0]
    cw = g_conv.shape[0]
    kvw = N_KV_HEADS * HEAD_DIM
    n_q_heads = aw // HEAD_DIM
    d_ff = w_gate.shape[1]
    in_width = w_in.shape[1]
    qkv_width = aw + 2 * kvw
    assert in_width == qkv_width + 3 * cw and d == aw + cw
    assert seq % TM_PROJ == 0 and seq % TM_ATTN == 0 and TM_ATTN % SUB_ATTN == 0 and SUB_ATTN % WINDOW == 0
    assert m % TM_FFN == 0 and d_ff % TF_FFN == 0

    invf = _rope_inv_freq()
    row = lambda a: a.reshape(1, -1).astype(F32)
    cparams = functools.partial(pltpu.CompilerParams, vmem_limit_bytes=VMEM_LIMIT_BYTES)

    tm = TM_ATTN
    tf = TF_FFN
    steps = m // tm
    assert d % (steps * BF16_ROWS) == 0
    slab = d // steps
    const = lambda i: (0, 0)
    by_row = lambda i: (i, 0)
    by_col = lambda i: (0, i)
    g_attn_cols = jnp.broadcast_to(g_attn.astype(F32)[:, None], (aw, LANES))
    atn, w_conv_b, w_gu, w_out_b = pl.pallas_call(
        functools.partial(_attn_body, n_q_heads=n_q_heads, attn_scale=HEAD_DIM ** -0.5,
                          tiles_per_seq=seq // tm, tf=tf),
        grid=(steps,),
        in_specs=[
            pl.BlockSpec(memory_space=pltpu.SMEM),
            pl.BlockSpec((tm, d), by_row),
            pl.BlockSpec((1, tm), by_col),
            _resident((ROT_DIM // 2, 1), const),
            _resident((d, qkv_width), const),
            _resident((aw, LANES), const),
            pl.BlockSpec((slab, in_width), by_row),
            pl.BlockSpec((slab, d_ff), by_row),
            pl.BlockSpec((slab, d_ff), by_row),
            pl.BlockSpec((slab, d), by_row),
        ],
        out_specs=[
            pl.BlockSpec((aw, tm), by_col),
            pl.BlockSpec((slab, 3 * cw), by_row),
            pl.BlockSpec((slab, 2 * d_ff), by_row),
            pl.BlockSpec((slab, d), by_row),
        ],
        out_shape=[
            jax.ShapeDtypeStruct((aw, m), BF16),
            jax.ShapeDtypeStruct((d, 3 * cw), BF16),
            jax.ShapeDtypeStruct((d, 2 * d_ff), BF16),
            jax.ShapeDtypeStruct((d, d), BF16),
        ],
        scratch_shapes=[
            pltpu.VMEM((aw, tm), BF16),
            pltpu.VMEM((WINDOW + tm, 4 * kvw), BF16),
            pltpu.VMEM((kvw, WINDOW + tm), BF16),
            pltpu.VMEM((aw, tm), F32),
            pltpu.VMEM((d, qkv_width), BF16),
        ],
        compiler_params=cparams(dimension_semantics=("arbitrary",)),
        name="qkv_swa",
    )(sinks.astype(F32), h, pos_row, invf, w_in, g_attn_cols, w_in, w_gate, w_up, w_out)

    tm = TM_PROJ
    steps = m // tm
    assert d_ff % (steps * BF16_ROWS) == 0
    slab = d_ff // steps
    held_row = lambda i: (jnp.minimum(i, steps - 1), 0)
    held_col = lambda i: (0, jnp.minimum(i, steps - 1))
    lagged = lambda i: (jnp.maximum(i - 1, 0), 0)
    h1, w_down_b = pl.pallas_call(
        functools.partial(_mix_body, alpha=alpha, tiles_per_seq=seq // tm, n_row_tiles=steps),
        grid=(steps + 1,),
        in_specs=[
            pl.BlockSpec((tm, d), held_row),
            pl.BlockSpec((aw, tm), held_col),
            _resident((d, 3 * cw), const),
            _resident((CONV_WIDTH, cw), const),
            _resident((1, cw), const),
            _resident((d, d), const),
            _resident((1, d), const),
            _resident((1, d), const),
            pl.BlockSpec((slab, d), held_row),
        ],
        out_specs=[pl.BlockSpec((tm, d), lagged), pl.BlockSpec((slab, d), held_row)],
        out_shape=[jax.ShapeDtypeStruct((m, d), F32), jax.ShapeDtypeStruct((d_ff, d), BF16)],
        scratch_shapes=[pltpu.VMEM((tm + 2 * SUBLANES, cw), F32), pltpu.VMEM((tm, d), F32)],
        compiler_params=cparams(dimension_semantics=("arbitrary",)),
        name="conv_outproj_ln",
    )(h, atn, w_conv_b, conv_w.astype(F32), row(g_conv), w_out_b, row(ln1_g), row(ln1_b), w_down)

    tm, tf = TM_FFN, TF_FFN
    n_i, n_j = m // tm, d_ff // tf
    n_pairs = n_j // 2
    n_steps = n_pairs + n_j % 2
    last_step = n_steps - 1 if (n_i - 1) % 2 == 0 else 0

    def step(i, s):
        return jnp.where(i < n_i, jnp.where(i % 2 == 0, s, n_steps - 1 - s), last_step)

    tile_a = lambda i, s: 2 * step(i, s)
    tile_b = lambda i, s: jnp.minimum(2 * step(i, s) + 1, 2 * n_pairs - 1)
    row_i = lambda i, s: (jnp.minimum(i, n_i - 1), 0)
    out_i = lambda i, s: (jnp.clip(jnp.where(s == 0, i - 1, i), 0, n_i - 1), 0)
    out = pl.pallas_call(
        functools.partial(_ffn_body, alpha=alpha, n_pairs=n_pairs, n_row_tiles=n_i),
        grid=(n_i + 1, n_steps),
        in_specs=[
            pl.BlockSpec((tm, d), row_i),
            pl.BlockSpec((d, 2 * tf), lambda i, s: (0, tile_a(i, s))),
            pl.BlockSpec((d, 2 * tf), lambda i, s: (0, tile_b(i, s))),
            pl.BlockSpec((tf, d), lambda i, s: (tile_a(i, s), 0)),
            pl.BlockSpec((tf, d), lambda i, s: (tile_b(i, s), 0)),
            _resident((1, d), lambda i, s: (0, 0)),
            _resident((1, d), lambda i, s: (0, 0)),
        ],
        out_specs=pl.BlockSpec((tm, d), out_i),
        out_shape=jax.ShapeDtypeStruct((m, d), F32),
        scratch_shapes=[pltpu.VMEM((tm, d), BF16), pltpu.VMEM((tm, d), F32)],
        compiler_params=cparams(dimension_semantics=("arbitrary", "arbitrary")),
        name="swiglu_ffn_ln",
    )(h1, w_gu, w_gu, w_down_b, w_down_b, row(ln2_g), row(ln2_b))
    return out


def kernel(x, positions, w_in, conv_w, sinks, g_attn, g_conv, w_out, ln1_g, ln1_b, w_gate, w_up, w_down, ln2_g, ln2_b):
    b, s, d = x.shape
    depth = w_in.shape[0]
    alpha = (2 * depth) ** 0.25
    h = x.reshape(b * s, d)
    pos_row = positions.reshape(1, b * s)
    for l in range(depth):
        h = _layer(h, pos_row, w_in[l], conv_w[l], sinks[l], g_attn[l], g_conv[l], w_out[l],
                   ln1_g[l], ln1_b[l], w_gate[l], w_up[l], w_down[l], ln2_g[l], ln2_b[l],
                   seq=s, alpha=alpha)
    return h.reshape(b, s, d)
```

```python
import functools

import jax
import jax.numpy as jnp
from jax import lax
from jax.experimental import pallas as pl
from jax.experimental.pallas import tpu as pltpu

F32 = jnp.float32
BF16 = jnp.bfloat16

HEAD_DIM = 64
N_KV_HEADS = 4
WINDOW = 128
ROT_DIM = HEAD_DIM // 4
ROPE_THETA = 500000.0
CONV_WIDTH = 3
LN_EPS = 1e-5
RMS_EPS = 1e-6

LANES = 128
SUBLANES = 8
BF16_ROWS = 16
VMEM_LIMIT_BYTES = 60000 * 1024

TM_PROJ = 512
TM_ATTN = 512
SUB_ATTN = 256
SUB_MIX = 256
LN_SLAB = 16
TM_FFN = 512
TF_FFN = 512

_TN = (((0,), (0,)), ((), ()))


def _resident(block_shape, index_map):
    return pl.BlockSpec(block_shape, index_map, pipeline_mode=pl.Buffered(1))


def _layer_norm(x, g, b):
    mu = jnp.mean(x, axis=-1, keepdims=True)
    xc = x - mu
    var = jnp.mean(xc * xc, axis=-1, keepdims=True)
    return xc * lax.rsqrt(var + LN_EPS) * g + b


def _rms_norm(x, g):
    ms = jnp.mean(x * x, axis=-1, keepdims=True)
    return x * lax.rsqrt(ms + RMS_EPS) * g


def _zero_after(x):
    bits = pltpu.bitcast(jnp.max(x, axis=(0, 1), keepdims=True), jnp.uint32)
    return pltpu.bitcast((bits >> 16) >> 16, F32)


def _mix_body(x_ref, atn_ref, w_ref, cw_ref, gc_ref, wo_ref, g1_ref, b1_ref, wd_ref,
              h_ref, wdb_ref, zbuf_ref, pre_ref, *, alpha, tiles_per_seq, n_row_tiles):
    tm = x_ref.shape[0]
    aw = atn_ref.shape[0]
    cw = gc_ref.shape[1]
    i = pl.program_id(0)
    live = i < n_row_tiles

    @pl.when(i == 0)
    def _():
        pre_ref[...] = jnp.zeros(pre_ref.shape, F32)

    @pl.when(i % tiles_per_seq == 0)
    def _():
        zbuf_ref[0:SUBLANES, :] = jnp.zeros((SUBLANES, cw), F32)

    def finish_previous():
        top = None
        for r in range(0, tm, LN_SLAB):
            y = _layer_norm(pre_ref[r:r + LN_SLAB, :], g1_ref[...], b1_ref[...])
            h_ref[r:r + LN_SLAB, :] = y
            top = y if top is None else jnp.maximum(top, y)
        return top

    @pl.when(live)
    def _():
        wdb_ref[...] = wd_ref[...].astype(BF16)

        xb = x_ref[...].astype(BF16)

        def proj(part):
            return jnp.dot(xb, w_ref[:, part * cw:(part + 1) * cw], preferred_element_type=F32)

        anchor = _zero_after(finish_previous())
        zbuf_ref[SUBLANES:SUBLANES + tm, :] = proj(0) * proj(2)
        yb = proj(1)
        chunks = [slice(c * SUB_MIX, (c + 1) * SUB_MIX) for c in range(tm // SUB_MIX)]
        attn_mix = [lax.dot_general(atn_ref[:, rows], wo_ref[0:aw, :], _TN, preferred_element_type=F32)
                    for rows in chunks]
        for rows, mix in zip(chunks, attn_mix):
            slabs = []
            for r in range(rows.start, rows.stop, 2 * LN_SLAB):
                shifted = lambda back: zbuf_ref[SUBLANES - back + r:SUBLANES - back + r + 2 * LN_SLAB, :]
                conv = (cw_ref[0:1, :] * shifted(2) + cw_ref[1:2, :] * shifted(1)
                        + (cw_ref[2:3, :] * shifted(0) + anchor))
                slabs.append(_rms_norm(yb[r:r + 2 * LN_SLAB, :] * conv, gc_ref[...]).astype(BF16))
            co = jnp.concatenate(slabs, axis=0)
            mix = mix + jnp.dot(co, wo_ref[aw:, :], preferred_element_type=F32)
            pre_ref[rows, :] = alpha * x_ref[rows, :] + mix
        zbuf_ref[0:SUBLANES, :] = zbuf_ref[tm:tm + SUBLANES, :]

    @pl.when(jnp.logical_not(live))
    def _():
        finish_previous()


def _attn_body(sinks_ref, x_ref, pos_ref, invf_ref, wq_ref, gat_ref, wc_ref, wg_ref, wu_ref, wo_ref,
               atn_ref, wcb_ref, wgu_ref, wob_ref,
               q_s, k_s, v_s, at_ref, wqb_s, *, n_q_heads, attn_scale, tiles_per_seq, tf):
    aw, tm = q_s.shape
    kvw = v_s.shape[0]
    wcb_ref[...] = wc_ref[:, wc_ref.shape[1] - wcb_ref.shape[1]:].astype(BF16)
    for j in range(wg_ref.shape[1] // tf):
        src = slice(j * tf, (j + 1) * tf)
        wgu_ref[:, 2 * j * tf:(2 * j + 1) * tf] = wg_ref[:, src].astype(BF16)
        wgu_ref[:, (2 * j + 1) * tf:(2 * j + 2) * tf] = wu_ref[:, src].astype(BF16)
    wob_ref[...] = wo_ref[...].astype(BF16)
    W = WINDOW
    group = n_q_heads // N_KV_HEADS
    assert group == 4 and HEAD_DIM * 2 == LANES
    i = pl.program_id(0)
    prev_bias = jnp.where(i % tiles_per_seq == 0, -jnp.inf, 0.0)

    @pl.when(i == 0)
    def _():
        wqb_s[...] = wq_ref[...].astype(BF16)
        k_s[0:W, :] = jnp.zeros((W, k_s.shape[1]), BF16)
        v_s[:, 0:W] = jnp.zeros((kvw, W), BF16)

    kj = lax.broadcasted_iota(jnp.int32, (W, 2 * W), 0)
    qi = lax.broadcasted_iota(jnp.int32, (W, 2 * W), 1) % W
    use_cur = kj <= qi
    cur_mask = jnp.where(use_cur, 1.0, 0.0).astype(BF16)
    prev_mask = jnp.where(use_cur, 0.0, 1.0).astype(BF16)
    ones_rows = jnp.ones((BF16_ROWS, 2 * W), BF16)
    low = lax.broadcasted_iota(jnp.int32, (SUB_ATTN, LANES), 1) < HEAD_DIM
    half = ROT_DIM // 2

    def project(rows):
        xb = x_ref[rows, :].astype(BF16)
        ang = invf_ref[...] * pos_ref[:, rows].astype(F32)
        cos_t = jnp.cos(ang)
        sin_t = jnp.sin(ang)

        def proj(lo, width):
            return jnp.dot(xb, wqb_s[:, lo:lo + width], preferred_element_type=F32)

        def rope_t(tt, head):
            b0 = head * HEAD_DIM
            t1 = tt[b0:b0 + half, :]
            t2 = tt[b0 + half:b0 + ROT_DIM, :]
            return jnp.concatenate([t1 * cos_t - t2 * sin_t, t2 * cos_t + t1 * sin_t,
                                    tt[b0 + ROT_DIM:b0 + HEAD_DIM, :]], axis=0)

        krows = slice(W + rows.start, W + rows.stop)

        def project_k():
            kt = proj(aw, kvw).T
            k = jnp.concatenate([rope_t(kt, hd) for hd in range(kvw // HEAD_DIM)], axis=0).T
            for c in range(kvw // LANES):
                kr = k[:, c * LANES:(c + 1) * LANES]
                sw = pltpu.roll(kr, HEAD_DIM, 1)
                even = slice(2 * c * LANES, (2 * c + 1) * LANES)
                odd = slice((2 * c + 1) * LANES, (2 * c + 2) * LANES)
                second = lambda blk: slice(2 * kvw + blk.start, 2 * kvw + blk.stop)
                k_s[krows, even] = jnp.where(low, kr, 0.0).astype(BF16)
                k_s[krows, odd] = jnp.where(low, sw, 0.0).astype(BF16)
                k_s[krows, second(even)] = jnp.where(low, 0.0, sw).astype(BF16)
                k_s[krows, second(odd)] = jnp.where(low, 0.0, kr).astype(BF16)

        def project_q():
            qt = proj(0, aw).T
            for hd in range(aw // HEAD_DIM):
                q_s[hd * HEAD_DIM:(hd + 1) * HEAD_DIM, rows] = (rope_t(qt, hd) * attn_scale).astype(BF16)

        def project_v():
            v_s[:, krows] = proj(aw + kvw, kvw).T.astype(BF16)

        return [project_k, project_q, project_v]

    copy_lanes = k_s.shape[1] // 2

    def window_scores(w):
        rows = slice(w * W, (w + 1) * W)
        scores = []
        for g in range(N_KV_HEADS):
            qg = jnp.concatenate([q_s[(2 * g) * LANES:(2 * g + 1) * LANES, rows],
                                  q_s[(2 * g + 1) * LANES:(2 * g + 2) * LANES, rows]], axis=1)
            for off in (0, copy_lanes):
                blk = slice(off + g * LANES, off + (g + 1) * LANES)
                kk = k_s[w * W:(w + 2) * W, blk]
                scores.append(jnp.dot(kk, qg, preferred_element_type=F32))
        return scores

    def window_values(w, scores):
        rows = slice(w * W, (w + 1) * W)
        probs = []
        for n, s in enumerate(scores):
            g, half_idx = divmod(n, 2)
            s_prev = s[0:W, :]
            if w == 0:
                s_prev = s_prev + prev_bias
            merged = jnp.where(use_cur, s[W:2 * W, :], s_prev)
            heads = (group * g + half_idx, group * g + 2 + half_idx)
            sink = jnp.concatenate([jnp.full((1, W), sinks_ref[heads[0]], F32),
                                    jnp.full((1, W), sinks_ref[heads[1]], F32)], axis=1)
            m = jnp.maximum(jnp.max(merged, axis=0, keepdims=True), sink)
            e = jnp.exp(merged - m).astype(BF16)
            ek = jnp.concatenate([e * prev_mask, e * cur_mask], axis=0)
            probs.append((heads, ek, jnp.exp(sink - m)))
        for n, (heads, ek, sink_e) in enumerate(probs):
            g = n // 2
            vt = v_s[g * HEAD_DIM:(g + 1) * HEAD_DIM, w * W:(w + 2) * W]
            vt_aug = jnp.concatenate([vt, ones_rows], axis=0)
            o = jnp.dot(vt_aug, ek, preferred_element_type=F32)
            denom = o[HEAD_DIM:HEAD_DIM + 1, :] + sink_e
            on = o[0:HEAD_DIM, :] * (1.0 / denom)
            for t in range(2):
                hd = heads[t]
                at_ref[hd * HEAD_DIM:(hd + 1) * HEAD_DIM, rows] = on[:, t * W:(t + 1) * W]

    def normalise(rows):
        at = at_ref[:, rows]
        ms = jnp.mean(at * at, axis=0, keepdims=True)
        gat = jnp.concatenate([gat_ref[...]] * ((rows.stop - rows.start) // LANES), axis=1)
        atn_ref[:, rows] = (at * lax.rsqrt(ms + RMS_EPS) * gat).astype(BF16)

    wpc = SUB_ATTN // W
    n_chunks = tm // SUB_ATTN
    chunk_rows = [slice(c * SUB_ATTN, (c + 1) * SUB_ATTN) for c in range(n_chunks)]
    windows = lambda c: list(range(c * wpc, (c + 1) * wpc))
    for part in project(chunk_rows[0]):
        part()
    scores = {w: window_scores(w) for w in windows(0)}
    for c in range(n_chunks):
        parts = project(chunk_rows[c + 1]) if c + 1 < n_chunks else []
        todo = windows(c)
        while parts or todo:
            if parts:
                parts.pop(0)()
            if todo:
                w = todo.pop(0)
                window_values(w, scores.pop(w))
        if c + 1 < n_chunks:
            scores = {w: window_scores(w) for w in windows(c + 1)}
        normalise(chunk_rows[c])

    k_s[0:W, :] = k_s[tm:tm + W, :]
    v_s[:, 0:W] = v_s[:, tm:tm + W]


def _ffn_body(h_ref, wgu_a_ref, wgu_b_ref, wd_a_ref, wd_b_ref, g2_ref, b2_ref, o_ref, hb_ref, acc_ref,
              *, alpha, n_single, n_row_tiles):
    i = pl.program_id(0)
    s = pl.program_id(1)
    tf = wd_a_ref.shape[0]
    live = i < n_row_tiles
    first = s == 0
    pair = s >= n_single

    @pl.when(jnp.logical_and(first, i == 0))
    def _():
        acc_ref[...] = jnp.zeros(acc_ref.shape, F32)

    @pl.when(jnp.logical_and(first, live))
    def _():
        hb_ref[...] = h_ref[...].astype(BF16)

    def gate_up(w_ref):
        return jnp.dot(hb_ref[...], w_ref[...], preferred_element_type=F32)

    def activation(gu, anchor=None, on_gate=True):
        gate = gu[:, 0:tf]
        if anchor is not None and on_gate:
            gate = gate + anchor
        act = (gate * (1.0 / (1.0 + jnp.exp(-gate))) * gu[:, tf:2 * tf]).astype(BF16)
        if anchor is not None and not on_gate:
            act = act + anchor.astype(BF16)
        return act

    def finish_previous():
        top = None
        for r in range(0, acc_ref.shape[0], LN_SLAB):
            y = _layer_norm(acc_ref[r:r + LN_SLAB, :], g2_ref[...], b2_ref[...])
            o_ref[r:r + LN_SLAB, :] = y
            top = y if top is None else jnp.maximum(top, y)
        return top

    def step_body(pair_step, first_step):
        gus = [gate_up(wgu_a_ref)] + ([gate_up(wgu_b_ref)] if pair_step else [])
        anchors = [None] * len(gus)
        if first_step:
            anchors[-1] = _zero_after(finish_previous())
        acts = [activation(gu, an, on_gate=pair_step) for gu, an in zip(gus, anchors)]
        down = jnp.dot(acts[0], wd_a_ref[...], preferred_element_type=F32)
        base = alpha * h_ref[...] if first_step else acc_ref[...]
        acc_ref[...] = base + down
        if pair_step:
            acc_ref[...] += jnp.dot(acts[1], wd_b_ref[...], preferred_element_type=F32)

    for pair_step in (True, False):
        for first_step in (True, False):
            cond = jnp.logical_and(live, jnp.logical_and(pair == pair_step, first == first_step))
            pl.when(cond)(functools.partial(step_body, pair_step, first_step))

    @pl.when(jnp.logical_and(first, jnp.logical_not(live)))
    def _():
        finish_previous()


def _rope_inv_freq():
    inv_freq = ROPE_THETA ** (-jnp.arange(0, ROT_DIM, 2, dtype=F32) / ROT_DIM)
    return inv_freq.reshape(ROT_DIM // 2, 1)


def _layer(h, pos_row, w_in, conv_w, sinks, g_attn, g_conv, w_out, ln1_g, ln1_b,
           w_gate, w_up, w_down, ln2_g, ln2_b, *, seq, alpha):
    m, d = h.shape
    aw = g_attn.shape[0]
    cw = g_conv.shape[0]
    kvw = N_KV_HEADS * HEAD_DIM
    n_q_heads = aw // HEAD_DIM
    d_ff = w_gate.shape[1]
    in_width = w_in.shape[1]
    qkv_width = aw + 2 * kvw
    assert in_width == qkv_width + 3 * cw and d == aw + cw
    assert seq % TM_PROJ == 0 and seq % TM_ATTN == 0 and TM_ATTN % SUB_ATTN == 0 and SUB_ATTN % WINDOW == 0
    assert m % TM_FFN == 0 and d_ff % TF_FFN == 0

    invf = _rope_inv_freq()
    row = lambda a: a.reshape(1, -1).astype(F32)
    cparams = functools.partial(pltpu.CompilerParams, vmem_limit_bytes=VMEM_LIMIT_BYTES)

    tm = TM_ATTN
    tf = TF_FFN
    steps = m // tm
    assert d % (steps * BF16_ROWS) == 0
    slab = d // steps
    const = lambda i: (0, 0)
    by_row = lambda i: (i, 0)
    by_col = lambda i: (0, i)
    g_attn_cols = jnp.broadcast_to(g_attn.astype(F32)[:, None], (aw, LANES))
    atn, w_conv_b, w_gu, w_out_b = pl.pallas_call(
        functools.partial(_attn_body, n_q_heads=n_q_heads, attn_scale=HEAD_DIM ** -0.5,
                          tiles_per_seq=seq // tm, tf=tf),
        grid=(steps,),
        in_specs=[
            pl.BlockSpec(memory_space=pltpu.SMEM),
            pl.BlockSpec((tm, d), by_row),
            pl.BlockSpec((1, tm), by_col),
            _resident((ROT_DIM // 2, 1), const),
            _resident((d, qkv_width), const),
            _resident((aw, LANES), const),
            pl.BlockSpec((slab, in_width), by_row),
            pl.BlockSpec((slab, d_ff), by_row),
            pl.BlockSpec((slab, d_ff), by_row),
            pl.BlockSpec((slab, d), by_row),
        ],
        out_specs=[
            pl.BlockSpec((aw, tm), by_col),
            pl.BlockSpec((slab, 3 * cw), by_row),
            pl.BlockSpec((slab, 2 * d_ff), by_row),
            pl.BlockSpec((slab, d), by_row),
        ],
        out_shape=[
            jax.ShapeDtypeStruct((aw, m), BF16),
            jax.ShapeDtypeStruct((d, 3 * cw), BF16),
            jax.ShapeDtypeStruct((d, 2 * d_ff), BF16),
            jax.ShapeDtypeStruct((d, d), BF16),
        ],
        scratch_shapes=[
            pltpu.VMEM((aw, tm), BF16),
            pltpu.VMEM((WINDOW + tm, 4 * kvw), BF16),
            pltpu.VMEM((kvw, WINDOW + tm), BF16),
            pltpu.VMEM((aw, tm), F32),
            pltpu.VMEM((d, qkv_width), BF16),
        ],
        compiler_params=cparams(dimension_semantics=("arbitrary",)),
        name="qkv_swa",
    )(sinks.astype(F32), h, pos_row, invf, w_in, g_attn_cols, w_in, w_gate, w_up, w_out)

    tm = TM_PROJ
    steps = m // tm
    assert d_ff % (steps * BF16_ROWS) == 0
    slab = d_ff // steps
    held_row = lambda i: (jnp.minimum(i, steps - 1), 0)
    held_col = lambda i: (0, jnp.minimum(i, steps - 1))
    lagged = lambda i: (jnp.maximum(i - 1, 0), 0)
    h1, w_down_b = pl.pallas_call(
        functools.partial(_mix_body, alpha=alpha, tiles_per_seq=seq // tm, n_row_tiles=steps),
        grid=(steps + 1,),
        in_specs=[
            pl.BlockSpec((tm, d), held_row),
            pl.BlockSpec((aw, tm), held_col),
            _resident((d, 3 * cw), const),
            _resident((CONV_WIDTH, cw), const),
            _resident((1, cw), const),
            _resident((d, d), const),
            _resident((1, d), const),
            _resident((1, d), const),
            pl.BlockSpec((slab, d), held_row),
        ],
        out_specs=[pl.BlockSpec((tm, d), lagged), pl.BlockSpec((slab, d), held_row)],
        out_shape=[jax.ShapeDtypeStruct((m, d), F32), jax.ShapeDtypeStruct((d_ff, d), BF16)],
        scratch_shapes=[pltpu.VMEM((tm + 2 * SUBLANES, cw), F32), pltpu.VMEM((tm, d), F32)],
        compiler_params=cparams(dimension_semantics=("arbitrary",)),
        name="conv_outproj_ln",
    )(h, atn, w_conv_b, conv_w.astype(F32), row(g_conv), w_out_b, row(ln1_g), row(ln1_b), w_down)

    tm, tf = TM_FFN, TF_FFN
    n_i, n_j = m // tm, d_ff // tf
    n_single = n_j % 2
    n_steps = n_j // 2 + n_single

    def step(i, s):
        return jnp.where(i < n_i, s, n_steps - 1)

    def tile_a(i, s):
        s = step(i, s)
        return jnp.where(s < n_single, 0, 2 * (s - n_single) + n_single)

    def tile_b(i, s):
        return jnp.where(step(i, s) < n_single, n_j - 1, tile_a(i, s) + 1)

    row_i = lambda i, s: (jnp.minimum(i, n_i - 1), 0)
    out_i = lambda i, s: (jnp.clip(jnp.where(s == 0, i - 1, i), 0, n_i - 1), 0)
    out = pl.pallas_call(
        functools.partial(_ffn_body, alpha=alpha, n_single=n_single, n_row_tiles=n_i),
        grid=(n_i + 1, n_steps),
        in_specs=[
            pl.BlockSpec((tm, d), row_i),
            pl.BlockSpec((d, 2 * tf), lambda i, s: (0, tile_a(i, s))),
            pl.BlockSpec((d, 2 * tf), lambda i, s: (0, tile_b(i, s))),
            pl.BlockSpec((tf, d), lambda i, s: (tile_a(i, s), 0)),
            pl.BlockSpec((tf, d), lambda i, s: (tile_b(i, s), 0)),
            _resident((1, d), lambda i, s: (0, 0)),
            _resident((1, d), lambda i, s: (0, 0)),
        ],
        out_specs=pl.BlockSpec((tm, d), out_i),
        out_shape=jax.ShapeDtypeStruct((m, d), F32),
        scratch_shapes=[pltpu.VMEM((tm, d), BF16), pltpu.VMEM((tm, d), F32)],
        compiler_params=cparams(dimension_semantics=("arbitrary", "arbitrary")),
        name="swiglu_ffn_ln",
    )(h1, w_gu, w_gu, w_down_b, w_down_b, row(ln2_g), row(ln2_b))
    return out


def kernel(x, positions, w_in, conv_w, sinks, g_attn, g_conv, w_out, ln1_g, ln1_b, w_gate, w_up, w_down, ln2_g, ln2_b):
    b, s, d = x.shape
    depth = w_in.shape[0]
    alpha = (2 * depth) ** 0.25
    h = x.reshape(b * s, d)
    pos_row = positions.reshape(1, b * s)
    for l in range(depth):
        h = _layer(h, pos_row, w_in[l], conv_w[l], sinks[l], g_attn[l], g_conv[l], w_out[l],
                   ln1_g[l], ln1_b[l], w_gate[l], w_up[l], w_down[l], ln2_g[l], ln2_b[l],
                   seq=s, alpha=alpha)
    return h.reshape(b, s, d)
```

```python
import functools

import jax
import jax.numpy as jnp
from jax import lax
from jax.experimental import pallas as pl
from jax.experimental.pallas import tpu as pltpu

F32 = jnp.float32
BF16 = jnp.bfloat16

HEAD_DIM = 64
N_KV_HEADS = 4
WINDOW = 128
ROT_DIM = HEAD_DIM // 4
ROPE_THETA = 500000.0
CONV_WIDTH = 3
LN_EPS = 1e-5
RMS_EPS = 1e-6

LANES = 128
SUBLANES = 8
BF16_ROWS = 16
VMEM_LIMIT_BYTES = 60000 * 1024

TM_PROJ = 512
TM_ATTN = 512
SUB_ATTN = 256
SUB_MIX = 256
LN_SLAB = 16
TM_FFN = 512
TF_FFN = 512

_TN = (((0,), (0,)), ((), ()))


def _resident(block_shape, index_map):
    return pl.BlockSpec(block_shape, index_map, pipeline_mode=pl.Buffered(1))


def _layer_norm(x, g, b):
    mu = jnp.mean(x, axis=-1, keepdims=True)
    xc = x - mu
    var = jnp.mean(xc * xc, axis=-1, keepdims=True)
    return xc * lax.rsqrt(var + LN_EPS) * g + b


def _rms_norm(x, g):
    ms = jnp.mean(x * x, axis=-1, keepdims=True)
    return x * lax.rsqrt(ms + RMS_EPS) * g


def _zero_after(x):
    bits = pltpu.bitcast(jnp.max(x, axis=(0, 1), keepdims=True), jnp.uint32)
    return pltpu.bitcast((bits >> 16) >> 16, F32)


def _mix_body(x_ref, atn_ref, w_ref, cw_ref, gc_ref, wo_ref, g1_ref, b1_ref, wd_ref,
              h_ref, wdb_ref, zbuf_ref, pre_ref, *, alpha, tiles_per_seq, n_row_tiles):
    tm = x_ref.shape[0]
    aw = atn_ref.shape[0]
    cw = gc_ref.shape[1]
    i = pl.program_id(0)
    live = i < n_row_tiles

    @pl.when(i == 0)
    def _():
        pre_ref[...] = jnp.zeros(pre_ref.shape, F32)

    @pl.when(i % tiles_per_seq == 0)
    def _():
        zbuf_ref[0:SUBLANES, :] = jnp.zeros((SUBLANES, cw), F32)

    def finish_previous():
        top = None
        for r in range(0, tm, LN_SLAB):
            y = _layer_norm(pre_ref[r:r + LN_SLAB, :], g1_ref[...], b1_ref[...])
            h_ref[r:r + LN_SLAB, :] = y
            top = y if top is None else jnp.maximum(top, y)
        return top

    @pl.when(live)
    def _():
        wdb_ref[...] = wd_ref[...].astype(BF16)

        xb = x_ref[...].astype(BF16)

        def proj(part):
            return jnp.dot(xb, w_ref[:, part * cw:(part + 1) * cw], preferred_element_type=F32)

        anchor = _zero_after(finish_previous())
        zbuf_ref[SUBLANES:SUBLANES + tm, :] = proj(0) * proj(2)
        yb = proj(1)
        chunks = [slice(c * SUB_MIX, (c + 1) * SUB_MIX) for c in range(tm // SUB_MIX)]
        attn_mix = [lax.dot_general(atn_ref[:, rows], wo_ref[0:aw, :], _TN, preferred_element_type=F32)
                    for rows in chunks]
        for rows, mix in zip(chunks, attn_mix):
            slabs = []
            for r in range(rows.start, rows.stop, 2 * LN_SLAB):
                shifted = lambda back: zbuf_ref[SUBLANES - back + r:SUBLANES - back + r + 2 * LN_SLAB, :]
                conv = (cw_ref[0:1, :] * shifted(2) + cw_ref[1:2, :] * shifted(1)
                        + (cw_ref[2:3, :] * shifted(0) + anchor))
                slabs.append(_rms_norm(yb[r:r + 2 * LN_SLAB, :] * conv, gc_ref[...]).astype(BF16))
            co = jnp.concatenate(slabs, axis=0)
            mix = mix + jnp.dot(co, wo_ref[aw:, :], preferred_element_type=F32)
            pre_ref[rows, :] = alpha * x_ref[rows, :] + mix
        zbuf_ref[0:SUBLANES, :] = zbuf_ref[tm:tm + SUBLANES, :]

    @pl.when(jnp.logical_not(live))
    def _():
        finish_previous()


def _attn_body(sinks_ref, x_ref, pos_ref, invf_ref, wq_ref, gat_ref, wc_ref, wg_ref, wu_ref, wo_ref,
               atn_ref, wcb_ref, wgu_ref, wob_ref,
               q_s, k_s, v_s, at_ref, wqb_s, *, n_q_heads, attn_scale, tiles_per_seq, tf):
    aw, tm = q_s.shape
    kvw = v_s.shape[0]
    wcb_ref[...] = wc_ref[:, wc_ref.shape[1] - wcb_ref.shape[1]:].astype(BF16)
    for j in range(wg_ref.shape[1] // tf):
        src = slice(j * tf, (j + 1) * tf)
        wgu_ref[:, 2 * j * tf:(2 * j + 1) * tf] = wg_ref[:, src].astype(BF16)
        wgu_ref[:, (2 * j + 1) * tf:(2 * j + 2) * tf] = wu_ref[:, src].astype(BF16)
    wob_ref[...] = wo_ref[...].astype(BF16)
    W = WINDOW
    group = n_q_heads // N_KV_HEADS
    assert group == 4 and HEAD_DIM * 2 == LANES
    i = pl.program_id(0)
    prev_bias = jnp.where(i % tiles_per_seq == 0, -jnp.inf, 0.0)

    @pl.when(i == 0)
    def _():
        wqb_s[...] = wq_ref[...].astype(BF16)
        k_s[0:W, :] = jnp.zeros((W, k_s.shape[1]), BF16)
        v_s[:, 0:W] = jnp.zeros((kvw, W), BF16)

    kj = lax.broadcasted_iota(jnp.int32, (W, 2 * W), 0)
    qi = lax.broadcasted_iota(jnp.int32, (W, 2 * W), 1) % W
    use_cur = kj <= qi
    cur_mask = jnp.where(use_cur, 1.0, 0.0).astype(BF16)
    prev_mask = jnp.where(use_cur, 0.0, 1.0).astype(BF16)
    ones_rows = jnp.ones((BF16_ROWS, 2 * W), BF16)
    low = lax.broadcasted_iota(jnp.int32, (SUB_ATTN, LANES), 1) < HEAD_DIM
    half = ROT_DIM // 2

    def project(rows):
        xb = x_ref[rows, :].astype(BF16)
        ang = invf_ref[...] * pos_ref[:, rows].astype(F32)
        cos_t = jnp.cos(ang)
        sin_t = jnp.sin(ang)

        def proj(lo, width):
            return jnp.dot(xb, wqb_s[:, lo:lo + width], preferred_element_type=F32)

        def rope_t(tt, head):
            b0 = head * HEAD_DIM
            t1 = tt[b0:b0 + half, :]
            t2 = tt[b0 + half:b0 + ROT_DIM, :]
            return jnp.concatenate([t1 * cos_t - t2 * sin_t, t2 * cos_t + t1 * sin_t,
                                    tt[b0 + ROT_DIM:b0 + HEAD_DIM, :]], axis=0)

        krows = slice(W + rows.start, W + rows.stop)

        def project_k():
            kt = proj(aw, kvw).T
            k = jnp.concatenate([rope_t(kt, hd) for hd in range(kvw // HEAD_DIM)], axis=0).T
            for c in range(kvw // LANES):
                kr = k[:, c * LANES:(c + 1) * LANES]
                sw = pltpu.roll(kr, HEAD_DIM, 1)
                even = slice(2 * c * LANES, (2 * c + 1) * LANES)
                odd = slice((2 * c + 1) * LANES, (2 * c + 2) * LANES)
                second = lambda blk: slice(2 * kvw + blk.start, 2 * kvw + blk.stop)
                k_s[krows, even] = jnp.where(low, kr, 0.0).astype(BF16)
                k_s[krows, odd] = jnp.where(low, sw, 0.0).astype(BF16)
                k_s[krows, second(even)] = jnp.where(low, 0.0, sw).astype(BF16)
                k_s[krows, second(odd)] = jnp.where(low, 0.0, kr).astype(BF16)

        def project_q():
            qt = proj(0, aw).T
            for hd in range(aw // HEAD_DIM):
                q_s[hd * HEAD_DIM:(hd + 1) * HEAD_DIM, rows] = (rope_t(qt, hd) * attn_scale).astype(BF16)

        def project_v():
            v_s[:, krows] = proj(aw + kvw, kvw).T.astype(BF16)

        return [project_k, project_q, project_v]

    copy_lanes = k_s.shape[1] // 2

    def window_scores(w):
        rows = slice(w * W, (w + 1) * W)
        scores = []
        for g in range(N_KV_HEADS):
            qg = jnp.concatenate([q_s[(2 * g) * LANES:(2 * g + 1) * LANES, rows],
                                  q_s[(2 * g + 1) * LANES:(2 * g + 2) * LANES, rows]], axis=1)
            for off in (0, copy_lanes):
                blk = slice(off + g * LANES, off + (g + 1) * LANES)
                kk = k_s[w * W:(w + 2) * W, blk]
                scores.append(jnp.dot(kk, qg, preferred_element_type=F32))
        return scores

    def window_values(w, scores):
        rows = slice(w * W, (w + 1) * W)
        probs = []
        for n, s in enumerate(scores):
            g, half_idx = divmod(n, 2)
            s_prev = s[0:W, :]
            if w == 0:
                s_prev = s_prev + prev_bias
            merged = jnp.where(use_cur, s[W:2 * W, :], s_prev)
            heads = (group * g + half_idx, group * g + 2 + half_idx)
            sink = jnp.concatenate([jnp.full((1, W), sinks_ref[heads[0]], F32),
                                    jnp.full((1, W), sinks_ref[heads[1]], F32)], axis=1)
            m = jnp.maximum(jnp.max(merged, axis=0, keepdims=True), sink)
            e = jnp.exp(merged - m).astype(BF16)
            ek = jnp.concatenate([e * prev_mask, e * cur_mask], axis=0)
            probs.append((heads, ek, jnp.exp(sink - m)))
        for n, (heads, ek, sink_e) in enumerate(probs):
            g = n // 2
            vt = v_s[g * HEAD_DIM:(g + 1) * HEAD_DIM, w * W:(w + 2) * W]
            vt_aug = jnp.concatenate([vt, ones_rows], axis=0)
            o = jnp.dot(vt_aug, ek, preferred_element_type=F32)
            denom = o[HEAD_DIM:HEAD_DIM + 1, :] + sink_e
            on = o[0:HEAD_DIM, :] * (1.0 / denom)
            for t in range(2):
                hd = heads[t]
                at_ref[hd * HEAD_DIM:(hd + 1) * HEAD_DIM, rows] = on[:, t * W:(t + 1) * W]

    def normalise(rows):
        at = at_ref[:, rows]
        ms = jnp.mean(at * at, axis=0, keepdims=True)
        gat = jnp.concatenate([gat_ref[...]] * ((rows.stop - rows.start) // LANES), axis=1)
        atn_ref[:, rows] = (at * lax.rsqrt(ms + RMS_EPS) * gat).astype(BF16)

    wpc = SUB_ATTN // W
    n_chunks = tm // SUB_ATTN
    chunk_rows = [slice(c * SUB_ATTN, (c + 1) * SUB_ATTN) for c in range(n_chunks)]
    windows = lambda c: list(range(c * wpc, (c + 1) * wpc))
    for part in project(chunk_rows[0]):
        part()
    scores = {w: window_scores(w) for w in windows(0)}
    for c in range(n_chunks):
        parts = project(chunk_rows[c + 1]) if c + 1 < n_chunks else []
        todo = windows(c)
        while parts or todo:
            if parts:
                parts.pop(0)()
            if todo:
                w = todo.pop(0)
                window_values(w, scores.pop(w))
        if c + 1 < n_chunks:
            scores = {w: window_scores(w) for w in windows(c + 1)}
        normalise(chunk_rows[c])

    k_s[0:W, :] = k_s[tm:tm + W, :]
    v_s[:, 0:W] = v_s[:, tm:tm + W]


def _ffn_body(h_ref, wgu_a_ref, wgu_b_ref, wd_a_ref, wd_b_ref, g2_ref, b2_ref, o_ref, hb_ref, acc_ref,
              *, alpha, n_single, n_row_tiles):
    i = pl.program_id(0)
    s = pl.program_id(1)
    tf = wd_a_ref.shape[0]
    live = i < n_row_tiles
    first = s == 0
    pair = s >= n_single

    @pl.when(jnp.logical_and(first, i == 0))
    def _():
        acc_ref[...] = jnp.zeros(acc_ref.shape, F32)

    @pl.when(jnp.logical_and(first, live))
    def _():
        hb_ref[...] = h_ref[...].astype(BF16)

    def gate_up(w_ref):
        return jnp.dot(hb_ref[...], w_ref[...], preferred_element_type=F32)

    def activation(gu, anchor=None, on_gate=True):
        gate = gu[:, 0:tf]
        if anchor is not None and on_gate:
            gate = gate + anchor
        act = (gate * (1.0 / (1.0 + jnp.exp(-gate))) * gu[:, tf:2 * tf]).astype(BF16)
        if anchor is not None and not on_gate:
            act = act + anchor.astype(BF16)
        return act

    def finish_previous():
        top = None
        for r in range(0, acc_ref.shape[0], LN_SLAB):
            y = _layer_norm(acc_ref[r:r + LN_SLAB, :], g2_ref[...], b2_ref[...])
            o_ref[r:r + LN_SLAB, :] = y
            top = y if top is None else jnp.maximum(top, y)
        return top

    def step_body(pair_step, first_step):
        tiles = [(wgu_a_ref, wd_a_ref), (wgu_b_ref, wd_b_ref)] if pair_step else [(wgu_b_ref, wd_b_ref)]
        gus = [gate_up(wgu_ref) for wgu_ref, _ in tiles]
        anchors = [None] * len(gus)
        if first_step:
            anchors[-1] = _zero_after(finish_previous())
        acts = [activation(gu, an, on_gate=pair_step) for gu, an in zip(gus, anchors)]
        down = jnp.dot(acts[0], tiles[0][1][...], preferred_element_type=F32)
        base = alpha * h_ref[...] if first_step else acc_ref[...]
        acc_ref[...] = base + down
        if pair_step:
            acc_ref[...] += jnp.dot(acts[1], tiles[1][1][...], preferred_element_type=F32)

    for pair_step in (True, False):
        for first_step in (True, False):
            cond = jnp.logical_and(live, jnp.logical_and(pair == pair_step, first == first_step))
            pl.when(cond)(functools.partial(step_body, pair_step, first_step))

    @pl.when(jnp.logical_and(first, jnp.logical_not(live)))
    def _():
        finish_previous()


def _rope_inv_freq():
    inv_freq = ROPE_THETA ** (-jnp.arange(0, ROT_DIM, 2, dtype=F32) / ROT_DIM)
    return inv_freq.reshape(ROT_DIM // 2, 1)


def _layer(h, pos_row, w_in, conv_w, sinks, g_attn, g_conv, w_out, ln1_g, ln1_b,
           w_gate, w_up, w_down, ln2_g, ln2_b, *, seq, alpha):
    m, d = h.shape
    aw = g_attn.shape[0]
    cw = g_conv.shape[0]
    kvw = N_KV_HEADS * HEAD_DIM
    n_q_heads = aw // HEAD_DIM
    d_ff = w_gate.shape[1]
    in_width = w_in.shape[1]
    qkv_width = aw + 2 * kvw
    assert in_width == qkv_width + 3 * cw and d == aw + cw
    assert seq % TM_PROJ == 0 and seq % TM_ATTN == 0 and TM_ATTN % SUB_ATTN == 0 and SUB_ATTN % WINDOW == 0
    assert m % TM_FFN == 0 and d_ff % TF_FFN == 0

    invf = _rope_inv_freq()
    row = lambda a: a.reshape(1, -1).astype(F32)
    cparams = functools.partial(pltpu.CompilerParams, vmem_limit_bytes=VMEM_LIMIT_BYTES)

    tm = TM_ATTN
    tf = TF_FFN
    steps = m // tm
    assert d % (steps * BF16_ROWS) == 0
    slab = d // steps
    const = lambda i: (0, 0)
    by_row = lambda i: (i, 0)
    by_col = lambda i: (0, i)
    g_attn_cols = jnp.broadcast_to(g_attn.astype(F32)[:, None], (aw, LANES))
    atn, w_conv_b, w_gu, w_out_b = pl.pallas_call(
        functools.partial(_attn_body, n_q_heads=n_q_heads, attn_scale=HEAD_DIM ** -0.5,
                          tiles_per_seq=seq // tm, tf=tf),
        grid=(steps,),
        in_specs=[
            pl.BlockSpec(memory_space=pltpu.SMEM),
            pl.BlockSpec((tm, d), by_row),
            pl.BlockSpec((1, tm), by_col),
            _resident((ROT_DIM // 2, 1), const),
            _resident((d, qkv_width), const),
            _resident((aw, LANES), const),
            pl.BlockSpec((slab, in_width), by_row),
            pl.BlockSpec((slab, d_ff), by_row),
            pl.BlockSpec((slab, d_ff), by_row),
            pl.BlockSpec((slab, d), by_row),
        ],
        out_specs=[
            pl.BlockSpec((aw, tm), by_col),
            pl.BlockSpec((slab, 3 * cw), by_row),
            pl.BlockSpec((slab, 2 * d_ff), by_row),
            pl.BlockSpec((slab, d), by_row),
        ],
        out_shape=[
            jax.ShapeDtypeStruct((aw, m), BF16),
            jax.ShapeDtypeStruct((d, 3 * cw), BF16),
            jax.ShapeDtypeStruct((d, 2 * d_ff), BF16),
            jax.ShapeDtypeStruct((d, d), BF16),
        ],
        scratch_shapes=[
            pltpu.VMEM((aw, tm), BF16),
            pltpu.VMEM((WINDOW + tm, 4 * kvw), BF16),
            pltpu.VMEM((kvw, WINDOW + tm), BF16),
            pltpu.VMEM((aw, tm), F32),
            pltpu.VMEM((d, qkv_width), BF16),
        ],
        compiler_params=cparams(dimension_semantics=("arbitrary",)),
        name="qkv_swa",
    )(sinks.astype(F32), h, pos_row, invf, w_in, g_attn_cols, w_in, w_gate, w_up, w_out)

    tm = TM_PROJ
    steps = m // tm
    assert d_ff % (steps * BF16_ROWS) == 0
    slab = d_ff // steps
    held_row = lambda i: (jnp.minimum(i, steps - 1), 0)
    held_col = lambda i: (0, jnp.minimum(i, steps - 1))
    lagged = lambda i: (jnp.maximum(i - 1, 0), 0)
    h1, w_down_b = pl.pallas_call(
        functools.partial(_mix_body, alpha=alpha, tiles_per_seq=seq // tm, n_row_tiles=steps),
        grid=(steps + 1,),
        in_specs=[
            pl.BlockSpec((tm, d), held_row),
            pl.BlockSpec((aw, tm), held_col),
            _resident((d, 3 * cw), const),
            _resident((CONV_WIDTH, cw), const),
            _resident((1, cw), const),
            _resident((d, d), const),
            _resident((1, d), const),
            _resident((1, d), const),
            pl.BlockSpec((slab, d), held_row),
        ],
        out_specs=[pl.BlockSpec((tm, d), lagged), pl.BlockSpec((slab, d), held_row)],
        out_shape=[jax.ShapeDtypeStruct((m, d), F32), jax.ShapeDtypeStruct((d_ff, d), BF16)],
        scratch_shapes=[pltpu.VMEM((tm + 2 * SUBLANES, cw), F32), pltpu.VMEM((tm, d), F32)],
        compiler_params=cparams(dimension_semantics=("arbitrary",)),
        name="conv_outproj_ln",
    )(h, atn, w_conv_b, conv_w.astype(F32), row(g_conv), w_out_b, row(ln1_g), row(ln1_b), w_down)

    tm, tf = TM_FFN, TF_FFN
    n_i, n_j = m // tm, d_ff // tf
    n_single = n_j % 2
    n_steps = n_j // 2 + n_single
    assert n_single == 1 and n_j >= 3

    def walk(i, s):
        i = jnp.minimum(i, n_i)
        s = jnp.where(i < n_i, s, 0)
        up = i % 2 == 0
        a = jnp.where(up, jnp.maximum(2 * s - 1, 1), n_j - 2 * jnp.maximum(s, 1))
        b = jnp.where(up, 2 * s, n_j - 1 - 2 * s)
        return a, b

    tile_a = lambda i, s: walk(i, s)[0]
    tile_b = lambda i, s: walk(i, s)[1]

    row_i = lambda i, s: (jnp.minimum(i, n_i - 1), 0)
    out_i = lambda i, s: (jnp.clip(jnp.where(s == 0, i - 1, i), 0, n_i - 1), 0)
    out = pl.pallas_call(
        functools.partial(_ffn_body, alpha=alpha, n_single=n_single, n_row_tiles=n_i),
        grid=(n_i + 1, n_steps),
        in_specs=[
            pl.BlockSpec((tm, d), row_i),
            pl.BlockSpec((d, 2 * tf), lambda i, s: (0, tile_a(i, s))),
            pl.BlockSpec((d, 2 * tf), lambda i, s: (0, tile_b(i, s))),
            pl.BlockSpec((tf, d), lambda i, s: (tile_a(i, s), 0)),
            pl.BlockSpec((tf, d), lambda i, s: (tile_b(i, s), 0)),
            _resident((1, d), lambda i, s: (0, 0)),
            _resident((1, d), lambda i, s: (0, 0)),
        ],
        out_specs=pl.BlockSpec((tm, d), out_i),
        out_shape=jax.ShapeDtypeStruct((m, d), F32),
        scratch_shapes=[pltpu.VMEM((tm, d), BF16), pltpu.VMEM((tm, d), F32)],
        compiler_params=cparams(dimension_semantics=("arbitrary", "arbitrary")),
        name="swiglu_ffn_ln",
    )(h1, w_gu, w_gu, w_down_b, w_down_b, row(ln2_g), row(ln2_b))
    return out


def kernel(x, positions, w_in, conv_w, sinks, g_attn, g_conv, w_out, ln1_g, ln1_b, w_gate, w_up, w_down, ln2_g, ln2_b):
    b, s, d = x.shape
    depth = w_in.shape[0]
    alpha = (2 * depth) ** 0.25
    h = x.reshape(b * s, d)
    pos_row = positions.reshape(1, b * s)
    for l in range(depth):
        h = _layer(h, pos_row, w_in[l], conv_w[l], sinks[l], g_attn[l], g_conv[l], w_out[l],
                   ln1_g[l], ln1_b[l], w_gate[l], w_up[l], w_down[l], ln2_g[l], ln2_b[l],
                   seq=s, alpha=alpha)
    return h.reshape(b, s, d)
```

```python
import functools

import jax
import jax.numpy as jnp
from jax import lax
from jax.experimental import pallas as pl
from jax.experimental.pallas import tpu as pltpu

F32 = jnp.float32
BF16 = jnp.bfloat16

HEAD_DIM = 64
N_KV_HEADS = 4
WINDOW = 128
ROT_DIM = HEAD_DIM // 4
ROPE_THETA = 500000.0
CONV_WIDTH = 3
LN_EPS = 1e-5
RMS_EPS = 1e-6

LANES = 128
SUBLANES = 8
BF16_ROWS = 16
VMEM_LIMIT_BYTES = 60000 * 1024

TM_PROJ = 512
TM_ATTN = 512
SUB_ATTN = 256
SUB_MIX = 256
LN_SLAB = 16
TM_FFN = 512
TF_FFN = 512

_TN = (((0,), (0,)), ((), ()))


def _resident(block_shape, index_map):
    return pl.BlockSpec(block_shape, index_map, pipeline_mode=pl.Buffered(1))


def _layer_norm(x, g, b):
    mu = jnp.mean(x, axis=-1, keepdims=True)
    xc = x - mu
    var = jnp.mean(xc * xc, axis=-1, keepdims=True)
    return xc * lax.rsqrt(var + LN_EPS) * g + b


def _rms_norm(x, g):
    ms = jnp.mean(x * x, axis=-1, keepdims=True)
    return x * lax.rsqrt(ms + RMS_EPS) * g


def _zero_after(x):
    bits = pltpu.bitcast(jnp.max(x, axis=(0, 1), keepdims=True), jnp.uint32)
    return pltpu.bitcast((bits >> 16) >> 16, F32)


def _mix_body(x_ref, atn_ref, w_ref, cw_ref, gc_ref, wo_ref, g1_ref, b1_ref, wd_ref,
              h_ref, wdb_ref, zbuf_ref, pre_ref, *, alpha, tiles_per_seq, n_row_tiles):
    tm = x_ref.shape[0]
    aw = atn_ref.shape[0]
    cw = gc_ref.shape[1]
    i = pl.program_id(0)
    live = i < n_row_tiles

    @pl.when(i == 0)
    def _():
        pre_ref[...] = jnp.zeros(pre_ref.shape, F32)

    @pl.when(i % tiles_per_seq == 0)
    def _():
        zbuf_ref[0:SUBLANES, :] = jnp.zeros((SUBLANES, cw), F32)

    def finish_previous():
        top = None
        for r in range(0, tm, LN_SLAB):
            y = _layer_norm(pre_ref[r:r + LN_SLAB, :], g1_ref[...], b1_ref[...])
            h_ref[r:r + LN_SLAB, :] = y
            top = y if top is None else jnp.maximum(top, y)
        return top

    @pl.when(live)
    def _():
        wdb_ref[...] = wd_ref[...].astype(BF16)

        xb = x_ref[...].astype(BF16)

        def proj(part):
            return jnp.dot(xb, w_ref[:, part * cw:(part + 1) * cw], preferred_element_type=F32)

        anchor = _zero_after(finish_previous())
        zbuf_ref[SUBLANES:SUBLANES + tm, :] = proj(0) * proj(2)
        yb = proj(1)
        chunks = [slice(c * SUB_MIX, (c + 1) * SUB_MIX) for c in range(tm // SUB_MIX)]
        attn_mix = [lax.dot_general(atn_ref[:, rows], wo_ref[0:aw, :], _TN, preferred_element_type=F32)
                    for rows in chunks]
        for rows, mix in zip(chunks, attn_mix):
            slabs = []
            for r in range(rows.start, rows.stop, 2 * LN_SLAB):
                shifted = lambda back: zbuf_ref[SUBLANES - back + r:SUBLANES - back + r + 2 * LN_SLAB, :]
                conv = (cw_ref[0:1, :] * shifted(2) + cw_ref[1:2, :] * shifted(1)
                        + (cw_ref[2:3, :] * shifted(0) + anchor))
                slabs.append(_rms_norm(yb[r:r + 2 * LN_SLAB, :] * conv, gc_ref[...]).astype(BF16))
            co = jnp.concatenate(slabs, axis=0)
            mix = mix + jnp.dot(co, wo_ref[aw:, :], preferred_element_type=F32)
            pre_ref[rows, :] = alpha * x_ref[rows, :] + mix
        zbuf_ref[0:SUBLANES, :] = zbuf_ref[tm:tm + SUBLANES, :]

    @pl.when(jnp.logical_not(live))
    def _():
        finish_previous()


def _attn_body(sinks_ref, x_ref, pos_ref, invf_ref, wq_ref, gat_ref, wc_ref, wg_ref, wu_ref, wo_ref,
               atn_ref, wcb_ref, wgu_ref, wob_ref,
               q_s, k_s, v_s, at_ref, wqb_s, *, n_q_heads, attn_scale, tiles_per_seq, tf):
    aw, tm = q_s.shape
    kvw = v_s.shape[0]
    wcb_ref[...] = wc_ref[:, wc_ref.shape[1] - wcb_ref.shape[1]:].astype(BF16)
    for j in range(wg_ref.shape[1] // tf):
        src = slice(j * tf, (j + 1) * tf)
        wgu_ref[:, 2 * j * tf:(2 * j + 1) * tf] = wg_ref[:, src].astype(BF16)
        wgu_ref[:, (2 * j + 1) * tf:(2 * j + 2) * tf] = wu_ref[:, src].astype(BF16)
    wob_ref[...] = wo_ref[...].astype(BF16)
    W = WINDOW
    group = n_q_heads // N_KV_HEADS
    assert group == 4 and HEAD_DIM * 2 == LANES
    i = pl.program_id(0)
    prev_bias = jnp.where(i % tiles_per_seq == 0, -jnp.inf, 0.0)

    @pl.when(i == 0)
    def _():
        wqb_s[...] = wq_ref[...].astype(BF16)
        k_s[0:W, :] = jnp.zeros((W, k_s.shape[1]), BF16)
        v_s[:, 0:W] = jnp.zeros((kvw, W), BF16)

    kj = lax.broadcasted_iota(jnp.int32, (W, 2 * W), 0)
    qi = lax.broadcasted_iota(jnp.int32, (W, 2 * W), 1) % W
    use_cur = kj <= qi
    cur_mask = jnp.where(use_cur, 1.0, 0.0).astype(BF16)
    prev_mask = jnp.where(use_cur, 0.0, 1.0).astype(BF16)
    ones_rows = jnp.ones((BF16_ROWS, 2 * W), BF16)
    low = lax.broadcasted_iota(jnp.int32, (SUB_ATTN, LANES), 1) < HEAD_DIM
    half = ROT_DIM // 2

    def project(rows):
        xb = x_ref[rows, :].astype(BF16)
        ang = invf_ref[...] * pos_ref[:, rows].astype(F32)
        cos_t = jnp.cos(ang)
        sin_t = jnp.sin(ang)

        def proj(lo, width):
            return jnp.dot(xb, wqb_s[:, lo:lo + width], preferred_element_type=F32)

        def rope_t(tt, head):
            b0 = head * HEAD_DIM
            t1 = tt[b0:b0 + half, :]
            t2 = tt[b0 + half:b0 + ROT_DIM, :]
            return jnp.concatenate([t1 * cos_t - t2 * sin_t, t2 * cos_t + t1 * sin_t,
                                    tt[b0 + ROT_DIM:b0 + HEAD_DIM, :]], axis=0)

        krows = slice(W + rows.start, W + rows.stop)

        def project_k():
            kt = proj(aw, kvw).T
            k = jnp.concatenate([rope_t(kt, hd) for hd in range(kvw // HEAD_DIM)], axis=0).T
            for c in range(kvw // LANES):
                kr = k[:, c * LANES:(c + 1) * LANES]
                sw = pltpu.roll(kr, HEAD_DIM, 1)
                even = slice(2 * c * LANES, (2 * c + 1) * LANES)
                odd = slice((2 * c + 1) * LANES, (2 * c + 2) * LANES)
                second = lambda blk: slice(2 * kvw + blk.start, 2 * kvw + blk.stop)
                k_s[krows, even] = jnp.where(low, kr, 0.0).astype(BF16)
                k_s[krows, odd] = jnp.where(low, sw, 0.0).astype(BF16)
                k_s[krows, second(even)] = jnp.where(low, 0.0, sw).astype(BF16)
                k_s[krows, second(odd)] = jnp.where(low, 0.0, kr).astype(BF16)

        def project_q():
            qt = proj(0, aw).T
            for hd in range(aw // HEAD_DIM):
                q_s[hd * HEAD_DIM:(hd + 1) * HEAD_DIM, rows] = (rope_t(qt, hd) * attn_scale).astype(BF16)

        def project_v():
            v_s[:, krows] = proj(aw + kvw, kvw).T.astype(BF16)

        return [project_k, project_q, project_v]

    copy_lanes = k_s.shape[1] // 2

    def window_scores(w):
        rows = slice(w * W, (w + 1) * W)
        scores = []
        for g in range(N_KV_HEADS):
            qg = jnp.concatenate([q_s[(2 * g) * LANES:(2 * g + 1) * LANES, rows],
                                  q_s[(2 * g + 1) * LANES:(2 * g + 2) * LANES, rows]], axis=1)
            for off in (0, copy_lanes):
                blk = slice(off + g * LANES, off + (g + 1) * LANES)
                kk = k_s[w * W:(w + 2) * W, blk]
                scores.append(jnp.dot(kk, qg, preferred_element_type=F32))
        return scores

    def window_values(w, scores):
        rows = slice(w * W, (w + 1) * W)
        probs = []
        for n, s in enumerate(scores):
            g, half_idx = divmod(n, 2)
            s_prev = s[0:W, :]
            if w == 0:
                s_prev = s_prev + prev_bias
            merged = jnp.where(use_cur, s[W:2 * W, :], s_prev)
            heads = (group * g + half_idx, group * g + 2 + half_idx)
            sink = jnp.concatenate([jnp.full((1, W), sinks_ref[heads[0]], F32),
                                    jnp.full((1, W), sinks_ref[heads[1]], F32)], axis=1)
            m = jnp.maximum(jnp.max(merged, axis=0, keepdims=True), sink)
            e = jnp.exp(merged - m).astype(BF16)
            ek = jnp.concatenate([e * prev_mask, e * cur_mask], axis=0)
            probs.append((heads, ek, jnp.exp(sink - m)))
        for n, (heads, ek, sink_e) in enumerate(probs):
            g = n // 2
            vt = v_s[g * HEAD_DIM:(g + 1) * HEAD_DIM, w * W:(w + 2) * W]
            vt_aug = jnp.concatenate([vt, ones_rows], axis=0)
            o = jnp.dot(vt_aug, ek, preferred_element_type=F32)
            denom = o[HEAD_DIM:HEAD_DIM + 1, :] + sink_e
            on = o[0:HEAD_DIM, :] * (1.0 / denom)
            for t in range(2):
                hd = heads[t]
                at_ref[hd * HEAD_DIM:(hd + 1) * HEAD_DIM, rows] = on[:, t * W:(t + 1) * W]

    def normalise(rows):
        at = at_ref[:, rows]
        ms = jnp.mean(at * at, axis=0, keepdims=True)
        gat = jnp.concatenate([gat_ref[...]] * ((rows.stop - rows.start) // LANES), axis=1)
        atn_ref[:, rows] = (at * lax.rsqrt(ms + RMS_EPS) * gat).astype(BF16)

    wpc = SUB_ATTN // W
    n_chunks = tm // SUB_ATTN
    chunk_rows = [slice(c * SUB_ATTN, (c + 1) * SUB_ATTN) for c in range(n_chunks)]
    windows = lambda c: list(range(c * wpc, (c + 1) * wpc))
    for part in project(chunk_rows[0]):
        part()
    scores = {w: window_scores(w) for w in windows(0)}
    for c in range(n_chunks):
        parts = project(chunk_rows[c + 1]) if c + 1 < n_chunks else []
        todo = windows(c)
        while parts or todo:
            if parts:
                parts.pop(0)()
            if todo:
                w = todo.pop(0)
                window_values(w, scores.pop(w))
        if c + 1 < n_chunks:
            scores = {w: window_scores(w) for w in windows(c + 1)}
        normalise(chunk_rows[c])

    k_s[0:W, :] = k_s[tm:tm + W, :]
    v_s[:, 0:W] = v_s[:, tm:tm + W]


def _ffn_body(h_ref, wgu_a_ref, wgu_b_ref, wd_a_ref, wd_b_ref, g2_ref, b2_ref, o_ref, hb_ref, acc_ref,
              *, alpha, n_single, n_row_tiles):
    i = pl.program_id(0)
    s = pl.program_id(1)
    tf = wd_a_ref.shape[0]
    live = i < n_row_tiles
    first = s == 0
    pair = s >= n_single

    @pl.when(jnp.logical_and(first, i == 0))
    def _():
        acc_ref[...] = jnp.zeros(acc_ref.shape, F32)

    def gate_up(w_ref):
        return jnp.dot(hb_ref[...], w_ref[...], preferred_element_type=F32)

    def activation(gu, anchor=None, on_gate=True):
        gate = gu[:, 0:tf]
        if anchor is not None and on_gate:
            gate = gate + anchor
        act = (gate * (1.0 / (1.0 + jnp.exp(-gate))) * gu[:, tf:2 * tf]).astype(BF16)
        if anchor is not None and not on_gate:
            act = act + anchor.astype(BF16)
        return act

    def finish_previous():
        top = None
        for r in range(0, acc_ref.shape[0], LN_SLAB):
            y = _layer_norm(acc_ref[r:r + LN_SLAB, :], g2_ref[...], b2_ref[...])
            o_ref[r:r + LN_SLAB, :] = y
            top = y if top is None else jnp.maximum(top, y)
        return top

    def step_body(pair_step, first_step):
        if first_step:
            hb_ref[...] = h_ref[...].astype(BF16)
        tiles = [(wgu_a_ref, wd_a_ref), (wgu_b_ref, wd_b_ref)] if pair_step else [(wgu_b_ref, wd_b_ref)]
        gus = [gate_up(wgu_ref) for wgu_ref, _ in tiles]
        anchors = [None] * len(gus)
        if first_step:
            anchors[-1] = _zero_after(finish_previous())
        acts = [activation(gu, an, on_gate=pair_step) for gu, an in zip(gus, anchors)]
        down = jnp.dot(acts[0], tiles[0][1][...], preferred_element_type=F32)
        base = alpha * h_ref[...] if first_step else acc_ref[...]
        acc_ref[...] = base + down
        if pair_step:
            acc_ref[...] += jnp.dot(acts[1], tiles[1][1][...], preferred_element_type=F32)

    for pair_step in (True, False):
        for first_step in (True, False):
            cond = jnp.logical_and(live, jnp.logical_and(pair == pair_step, first == first_step))
            pl.when(cond)(functools.partial(step_body, pair_step, first_step))

    @pl.when(jnp.logical_and(first, jnp.logical_not(live)))
    def _():
        finish_previous()


def _rope_inv_freq():
    inv_freq = ROPE_THETA ** (-jnp.arange(0, ROT_DIM, 2, dtype=F32) / ROT_DIM)
    return inv_freq.reshape(ROT_DIM // 2, 1)


def _layer(h, pos_row, w_in, conv_w, sinks, g_attn, g_conv, w_out, ln1_g, ln1_b,
           w_gate, w_up, w_down, ln2_g, ln2_b, *, seq, alpha):
    m, d = h.shape
    aw = g_attn.shape[0]
    cw = g_conv.shape[0]
    kvw = N_KV_HEADS * HEAD_DIM
    n_q_heads = aw // HEAD_DIM
    d_ff = w_gate.shape[1]
    in_width = w_in.shape[1]
    qkv_width = aw + 2 * kvw
    assert in_width == qkv_width + 3 * cw and d == aw + cw
    assert seq % TM_PROJ == 0 and seq % TM_ATTN == 0 and TM_ATTN % SUB_ATTN == 0 and SUB_ATTN % WINDOW == 0
    assert m % TM_FFN == 0 and d_ff % TF_FFN == 0

    invf = _rope_inv_freq()
    row = lambda a: a.reshape(1, -1).astype(F32)
    cparams = functools.partial(pltpu.CompilerParams, vmem_limit_bytes=VMEM_LIMIT_BYTES)

    tm = TM_ATTN
    tf = TF_FFN
    steps = m // tm
    assert d % (steps * BF16_ROWS) == 0
    slab = d // steps
    const = lambda i: (0, 0)
    by_row = lambda i: (i, 0)
    by_col = lambda i: (0, i)
    g_attn_cols = jnp.broadcast_to(g_attn.astype(F32)[:, None], (aw, LANES))
    atn, w_conv_b, w_gu, w_out_b = pl.pallas_call(
        functools.partial(_attn_body, n_q_heads=n_q_heads, attn_scale=HEAD_DIM ** -0.5,
                          tiles_per_seq=seq // tm, tf=tf),
        grid=(steps,),
        in_specs=[
            pl.BlockSpec(memory_space=pltpu.SMEM),
            pl.BlockSpec((tm, d), by_row),
            pl.BlockSpec((1, tm), by_col),
            _resident((ROT_DIM // 2, 1), const),
            _resident((d, qkv_width), const),
            _resident((aw, LANES), const),
            pl.BlockSpec((slab, in_width), by_row),
            pl.BlockSpec((slab, d_ff), by_row),
            pl.BlockSpec((slab, d_ff), by_row),
            pl.BlockSpec((slab, d), by_row),
        ],
        out_specs=[
            pl.BlockSpec((aw, tm), by_col),
            pl.BlockSpec((slab, 3 * cw), by_row),
            pl.BlockSpec((slab, 2 * d_ff), by_row),
            pl.BlockSpec((slab, d), by_row),
        ],
        out_shape=[
            jax.ShapeDtypeStruct((aw, m), BF16),
            jax.ShapeDtypeStruct((d, 3 * cw), BF16),
            jax.ShapeDtypeStruct((d, 2 * d_ff), BF16),
            jax.ShapeDtypeStruct((d, d), BF16),
        ],
        scratch_shapes=[
            pltpu.VMEM((aw, tm), BF16),
            pltpu.VMEM((WINDOW + tm, 4 * kvw), BF16),
            pltpu.VMEM((kvw, WINDOW + tm), BF16),
            pltpu.VMEM((aw, tm), F32),
            pltpu.VMEM((d, qkv_width), BF16),
        ],
        compiler_params=cparams(dimension_semantics=("arbitrary",)),
        name="qkv_swa",
    )(sinks.astype(F32), h, pos_row, invf, w_in, g_attn_cols, w_in, w_gate, w_up, w_out)

    tm = TM_PROJ
    steps = m // tm
    assert d_ff % (steps * BF16_ROWS) == 0
    slab = d_ff // steps
    held_row = lambda i: (jnp.minimum(i, steps - 1), 0)
    held_col = lambda i: (0, jnp.minimum(i, steps - 1))
    lagged = lambda i: (jnp.maximum(i - 1, 0), 0)
    h1, w_down_b = pl.pallas_call(
        functools.partial(_mix_body, alpha=alpha, tiles_per_seq=seq // tm, n_row_tiles=steps),
        grid=(steps + 1,),
        in_specs=[
            pl.BlockSpec((tm, d), held_row),
            pl.BlockSpec((aw, tm), held_col),
            _resident((d, 3 * cw), const),
            _resident((CONV_WIDTH, cw), const),
            _resident((1, cw), const),
            _resident((d, d), const),
            _resident((1, d), const),
            _resident((1, d), const),
            pl.BlockSpec((slab, d), held_row),
        ],
        out_specs=[pl.BlockSpec((tm, d), lagged), pl.BlockSpec((slab, d), held_row)],
        out_shape=[jax.ShapeDtypeStruct((m, d), F32), jax.ShapeDtypeStruct((d_ff, d), BF16)],
        scratch_shapes=[pltpu.VMEM((tm + 2 * SUBLANES, cw), F32), pltpu.VMEM((tm, d), F32)],
        compiler_params=cparams(dimension_semantics=("arbitrary",)),
        name="conv_outproj_ln",
    )(h, atn, w_conv_b, conv_w.astype(F32), row(g_conv), w_out_b, row(ln1_g), row(ln1_b), w_down)

    tm, tf = TM_FFN, TF_FFN
    n_i, n_j = m // tm, d_ff // tf
    n_single = n_j % 2
    n_steps = n_j // 2 + n_single
    assert n_single == 1 and n_j >= 3

    def walk(i, s):
        i = jnp.minimum(i, n_i)
        s = jnp.where(i < n_i, s, 0)
        up = i % 2 == 0
        a = jnp.where(up, jnp.maximum(2 * s - 1, 1), n_j - 2 * jnp.maximum(s, 1))
        b = jnp.where(up, 2 * s, n_j - 1 - 2 * s)
        return a, b

    tile_a = lambda i, s: walk(i, s)[0]
    tile_b = lambda i, s: walk(i, s)[1]

    row_i = lambda i, s: (jnp.minimum(i, n_i - 1), 0)
    out_i = lambda i, s: (jnp.clip(jnp.where(s == 0, i - 1, i), 0, n_i - 1), 0)
    out = pl.pallas_call(
        functools.partial(_ffn_body, alpha=alpha, n_single=n_single, n_row_tiles=n_i),
        grid=(n_i + 1, n_steps),
        in_specs=[
            pl.BlockSpec((tm, d), row_i),
            pl.BlockSpec((d, 2 * tf), lambda i, s: (0, tile_a(i, s))),
            pl.BlockSpec((d, 2 * tf), lambda i, s: (0, tile_b(i, s))),
            pl.BlockSpec((tf, d), lambda i, s: (tile_a(i, s), 0)),
            pl.BlockSpec((tf, d), lambda i, s: (tile_b(i, s), 0)),
            _resident((1, d), lambda i, s: (0, 0)),
            _resident((1, d), lambda i, s: (0, 0)),
        ],
        out_specs=pl.BlockSpec((tm, d), out_i),
        out_shape=jax.ShapeDtypeStruct((m, d), F32),
        scratch_shapes=[pltpu.VMEM((tm, d), BF16), pltpu.VMEM((tm, d), F32)],
        compiler_params=cparams(dimension_semantics=("arbitrary", "arbitrary")),
        name="swiglu_ffn_ln",
    )(h1, w_gu, w_gu, w_down_b, w_down_b, row(ln2_g), row(ln2_b))
    return out


def kernel(x, positions, w_in, conv_w, sinks, g_attn, g_conv, w_out, ln1_g, ln1_b, w_gate, w_up, w_down, ln2_g, ln2_b):
    b, s, d = x.shape
    depth = w_in.shape[0]
    alpha = (2 * depth) ** 0.25
    h = x.reshape(b * s, d)
    pos_row = positions.reshape(1, b * s)
    for l in range(depth):
        h = _layer(h, pos_row, w_in[l], conv_w[l], sinks[l], g_attn[l], g_conv[l], w_out[l],
                   ln1_g[l], ln1_b[l], w_gate[l], w_up[l], w_down[l], ln2_g[l], ln2_b[l],
                   seq=s, alpha=alpha)
    return h.reshape(b, s, d)
```
